```python
import math
import jax, jax.numpy as jnp
from jax import lax
import numpy as np

D_MODEL = 2048
BATCH = 4
SEQ = 2048
DEPTH = 1
DEC_BATCH = 32
DEC_SEQ = 64
PAST_LEN = 2048

CHUNK = 64
D_RNN = 1024
LRU_BLOCKS = 16
LRU_BLOCK = D_RNN // LRU_BLOCKS
CONV_W = 4
LRU_C = 8.0
RET_HEADS = 8
RET_DK = 128
RET_DV = 128
D_RET_K = RET_HEADS * RET_DK
D_RET_V = RET_HEADS * RET_DV
D_FF = 5632
DN_ALPHA = (2.0 * DEPTH) ** 0.25
DN_BETA = (8.0 * DEPTH) ** -0.25
LN_EPS = 1e-5
GN_EPS = 1e-5
ROPE_BASE = 10000.0
D_IN = 2 * D_RNN + 2 * D_RET_K + 2 * D_RET_V + 2 * D_MODEL
SPLIT_POINTS = (D_RNN, 2 * D_RNN, 2 * D_RNN + D_RET_K, 2 * D_RNN + 2 * D_RET_K,
                2 * D_RNN + 2 * D_RET_K + D_RET_V, 2 * D_RNN + 2 * D_RET_K + 2 * D_RET_V,
                2 * D_RNN + 2 * D_RET_K + 2 * D_RET_V + D_MODEL)

kernel_name = 'hawk_retnet_macaron_deepnorm_stream_step'


def layer_norm(x, g, b):
    xf = x.astype(jnp.float32)
    mu = jnp.mean(xf, -1, keepdims=True)
    var = jnp.mean(jnp.square(xf - mu), -1, keepdims=True)
    return ((xf - mu) * lax.rsqrt(var + LN_EPS) * g.astype(jnp.float32) + b.astype(jnp.float32)).astype(x.dtype)


def swiglu_ffn(x, w_gate, w_up, w_down):
    return (jax.nn.silu(x @ w_gate) * (x @ w_up)) @ w_down


def causal_dwconv(xb, conv_state, w, b):
    T = xb.shape[1]
    xpad = jnp.concatenate([conv_state.astype(xb.dtype), xb], axis=1)
    y = b
    for j in range(CONV_W):
        y = y + w[j] * xpad[:, j:j + T]
    return y, xpad[:, -(CONV_W - 1):]


def rg_lru(x, h0, rg_w, rg_b, ig_w, ig_b, lam):
    B, T, _ = x.shape
    xf = x.astype(jnp.float32)
    xg = xf.reshape(B, T, LRU_BLOCKS, LRU_BLOCK)
    r = jax.nn.sigmoid(jnp.einsum('btnk,nkj->btnj', xg, rg_w.astype(jnp.float32)).reshape(B, T, D_RNN) + rg_b)
    i = jax.nn.sigmoid(jnp.einsum('btnk,nkj->btnj', xg, ig_w.astype(jnp.float32)).reshape(B, T, D_RNN) + ig_b)
    log_a = -LRU_C * r * jax.nn.softplus(-lam.astype(jnp.float32))
    a = jnp.exp(log_a)
    u = jnp.sqrt(-jnp.expm1(2.0 * log_a)) * (i * xf)

    def combine(left, right):
        a1, b1 = left
        a2, b2 = right
        return a1 * a2, a2 * b1 + b2

    A, Bc = lax.associative_scan(combine, (a, u), axis=1)
    h = A * h0.astype(jnp.float32)[:, None] + Bc
    return h, h[:, -1]


def rotary(x, pos):
    d = x.shape[-1]
    inv_freq = ROPE_BASE ** (-jnp.arange(0, d, 2, dtype=jnp.float32) / d)
    ang = pos.astype(jnp.float32)[:, None] * inv_freq[None, :]
    cos = jnp.cos(ang)[None, :, None, :]
    sin = jnp.sin(ang)[None, :, None, :]
    x1, x2 = x[..., : d // 2], x[..., d // 2:]
    return jnp.concatenate([x1 * cos - x2 * sin, x1 * sin + x2 * cos], axis=-1)


def retention_chunkwise(q, k, v, S0):
    B, T, H, DK = q.shape
    DV = v.shape[-1]
    C = min(CHUNK, T)
    n = T // C
    log_g = jnp.log1p(-jnp.exp2(-5.0 - jnp.arange(H, dtype=jnp.float32)))
    idx = jnp.arange(C, dtype=jnp.float32)
    diff = idx[:, None] - idx[None, :]
    dmask = jnp.where(diff >= 0, jnp.exp(log_g[:, None, None] * jnp.maximum(diff, 0.0)), 0.0)
    q_dec = jnp.exp(log_g[:, None] * (idx[None, :] + 1.0)).T[None, :, :, None]
    k_dec = jnp.exp(log_g[:, None] * (C - 1.0 - idx[None, :])).T[None, :, :, None]
    chunk_dec = jnp.exp(log_g * C)[None, :, None, None]

    def to_chunks(t):
        return jnp.moveaxis(t.reshape(B, n, C, H, t.shape[-1]), 1, 0)

    def step(S, inp):
        qi, ki, vi = inp
        s = jnp.einsum('bihd,bjhd->bhij', qi, ki) * dmask
        o = jnp.einsum('bhij,bjhe->bihe', s, vi) + jnp.einsum('bihd,bhde->bihe', qi * q_dec, S)
        S = chunk_dec * S + jnp.einsum('bjhd,bjhe->bhde', ki * k_dec, vi)
        return S, o

    S, o = lax.scan(step, S0, (to_chunks(q), to_chunks(k), to_chunks(v)))
    return jnp.moveaxis(o, 0, 1).reshape(B, T, H, DV), S


def trunk_layer(x, pos, conv0, h0, S0,
                f1_g, f1_u, f1_d, ln1_g, ln1_b,
                w_in, conv_w, conv_b, rg_w, rg_b, ig_w, ig_b, lam,
                gn_g, gn_b, w_a_proj, w_b_proj, w_o, ln2_g, ln2_b,
                f2_g, f2_u, f2_d, ln3_g, ln3_b):
    B, T, _ = x.shape
    x = layer_norm(DN_ALPHA * x + 0.5 * swiglu_ffn(x, f1_g, f1_u, f1_d), ln1_g, ln1_b)

    z = x @ w_in
    xa, ga, q, k, v, g, gate_a, gate_b = jnp.split(z, list(SPLIT_POINTS), axis=-1)

    xc, conv_new = causal_dwconv(xa, conv0, conv_w, conv_b)
    h, h_last = rg_lru(xc, h0, rg_w, rg_b, ig_w, ig_b, lam)
    ya = (h * jax.nn.gelu(ga.astype(jnp.float32))).astype(x.dtype)

    qh = rotary(q.reshape(B, T, RET_HEADS, RET_DK).astype(jnp.float32), pos)
    kh = rotary(k.reshape(B, T, RET_HEADS, RET_DK).astype(jnp.float32), pos) * (RET_DK ** -0.5)
    vh = v.reshape(B, T, RET_HEADS, RET_DV).astype(jnp.float32)
    o, S_new = retention_chunkwise(qh, kh, vh, S0.astype(jnp.float32))
    mu = jnp.mean(o, -1, keepdims=True)
    var = jnp.mean(jnp.square(o - mu), -1, keepdims=True)
    o = ((o - mu) * lax.rsqrt(var + GN_EPS)).reshape(B, T, D_RET_V) * gn_g.astype(jnp.float32) + gn_b.astype(jnp.float32)
    yb = (jax.nn.silu(g.astype(jnp.float32)) * o).astype(x.dtype)

    merged = jax.nn.sigmoid(gate_a) * (ya @ w_a_proj) + jax.nn.sigmoid(gate_b) * (yb @ w_b_proj)
    x = layer_norm(DN_ALPHA * x + merged @ w_o, ln2_g, ln2_b)

    x = layer_norm(DN_ALPHA * x + 0.5 * swiglu_ffn(x, f2_g, f2_u, f2_d), ln3_g, ln3_b)
    return x, conv_new, h_last, S_new


def setup_inputs(seed: int = 0) -> dict:
    key = jax.random.key(seed)
    ks = jax.random.split(key, 32)
    f32 = jnp.float32
    L = DEPTH

    def nrm(k, shape, scale):
        return jax.random.normal(k, shape, f32) * scale

    u = jax.random.uniform(ks[13], (L, D_RNN), f32, 0.9, 0.999)
    s = u ** (1.0 / LRU_C)
    lru_lambda = jnp.log(s) - jnp.log1p(-s)
    return {
        'x_prompt': nrm(ks[0], (BATCH, SEQ, D_MODEL), 1.0),
        'x_sample': nrm(ks[1], (DEC_BATCH, DEC_SEQ, D_MODEL), 1.0),
        'state_conv': nrm(ks[2], (L, DEC_BATCH, CONV_W - 1, D_RNN), 1.0),
        'state_lru': nrm(ks[3], (L, DEC_BATCH, D_RNN), 0.5),
        'state_ret': nrm(ks[4], (L, DEC_BATCH, RET_HEADS, RET_DK, RET_DV), 0.5),
        'ffn1_w_gate': nrm(ks[5], (L, D_MODEL, D_FF), D_MODEL ** -0.5),
        'ffn1_w_up': nrm(ks[6], (L, D_MODEL, D_FF), D_MODEL ** -0.5),
        'ffn1_w_down': nrm(ks[7], (L, D_FF, D_MODEL), DN_BETA * D_FF ** -0.5),
        'ln1_g': 1.0 + nrm(ks[8], (L, D_MODEL), 0.02),
        'ln1_b': nrm(ks[9], (L, D_MODEL), 0.02),
        'w_in': nrm(ks[10], (L, D_MODEL, D_IN), D_MODEL ** -0.5),
        'conv_w': nrm(ks[11], (L, CONV_W, D_RNN), CONV_W ** -0.5),
        'conv_b': nrm(ks[12], (L, D_RNN), 0.02),
        'rg_w': nrm(ks[14], (L, LRU_BLOCKS, LRU_BLOCK, LRU_BLOCK), LRU_BLOCK ** -0.5),
        'rg_b': nrm(ks[15], (L, D_RNN), 0.02),
        'ig_w': nrm(ks[16], (L, LRU_BLOCKS, LRU_BLOCK, LRU_BLOCK), LRU_BLOCK ** -0.5),
        'ig_b': nrm(ks[17], (L, D_RNN), 0.02),
        'lru_lambda': lru_lambda,
        'ret_gn_g': 1.0 + nrm(ks[18], (L, D_RET_V), 0.02),
        'ret_gn_b': nrm(ks[19], (L, D_RET_V), 0.02),
        'w_a_proj': nrm(ks[20], (L, D_RNN, D_MODEL), D_RNN ** -0.5),
        'w_b_proj': nrm(ks[21], (L, D_RET_V, D_MODEL), D_RET_V ** -0.5),
        'w_o': nrm(ks[22], (L, D_MODEL, D_MODEL), DN_BETA * D_MODEL ** -0.5),
        'ln2_g': 1.0 + nrm(ks[23], (L, D_MODEL), 0.02),
        'ln2_b': nrm(ks[24], (L, D_MODEL), 0.02),
        'ffn2_w_gate': nrm(ks[25], (L, D_MODEL, D_FF), D_MODEL ** -0.5),
        'ffn2_w_up': nrm(ks[26], (L, D_MODEL, D_FF), D_MODEL ** -0.5),
        'ffn2_w_down': nrm(ks[27], (L, D_FF, D_MODEL), DN_BETA * D_FF ** -0.5),
        'ln3_g': 1.0 + nrm(ks[28], (L, D_MODEL), 0.02),
        'ln3_b': nrm(ks[29], (L, D_MODEL), 0.02),
    }


def reference(x_prompt, x_sample, state_conv, state_lru, state_ret,
              ffn1_w_gate, ffn1_w_up, ffn1_w_down, ln1_g, ln1_b,
              w_in, conv_w, conv_b, rg_w, rg_b, ig_w, ig_b, lru_lambda,
              ret_gn_g, ret_gn_b, w_a_proj, w_b_proj, w_o, ln2_g, ln2_b,
              ffn2_w_gate, ffn2_w_up, ffn2_w_down, ln3_g, ln3_b):

    def run(x, pos, conv0, h0, S0):
        convs, hs, Ss = [], [], []
        for l in range(DEPTH):
            x, c, h, S = trunk_layer(
                x, pos, conv0[l], h0[l], S0[l],
                ffn1_w_gate[l], ffn1_w_up[l], ffn1_w_down[l], ln1_g[l], ln1_b[l],
                w_in[l], conv_w[l], conv_b[l], rg_w[l], rg_b[l], ig_w[l], ig_b[l], lru_lambda[l],
                ret_gn_g[l], ret_gn_b[l], w_a_proj[l], w_b_proj[l], w_o[l], ln2_g[l], ln2_b[l],
                ffn2_w_gate[l], ffn2_w_up[l], ffn2_w_down[l], ln3_g[l], ln3_b[l])
            convs.append(c.astype(state_conv.dtype))
            hs.append(h.astype(state_lru.dtype))
            Ss.append(S.astype(state_ret.dtype))
        return x, jnp.stack(convs), jnp.stack(hs), jnp.stack(Ss)

    Bp, Tp, _ = x_prompt.shape
    pos_p = jnp.arange(Tp, dtype=jnp.int32)
    conv0_p = jnp.zeros((DEPTH, Bp, CONV_W - 1, D_RNN), state_conv.dtype)
    h0_p = jnp.zeros((DEPTH, Bp, D_RNN), state_lru.dtype)
    S0_p = jnp.zeros((DEPTH, Bp, RET_HEADS, RET_DK, RET_DV), state_ret.dtype)
    y_prompt, conv_prompt, lru_prompt, ret_prompt = run(x_prompt, pos_p, conv0_p, h0_p, S0_p)

    Ts = x_sample.shape[1]
    pos_s = PAST_LEN + jnp.arange(Ts, dtype=jnp.int32)
    y_sample, conv_sample, lru_sample, ret_sample = run(x_sample, pos_s, state_conv, state_lru, state_ret)

    return (y_prompt, y_sample, conv_prompt, lru_prompt, ret_prompt, conv_sample, lru_sample, ret_sample)
```

```python
import functools

import jax
import jax.numpy as jnp
from jax import lax
from jax.experimental import pallas as pl
from jax.experimental.pallas import tpu as pltpu

F32 = jnp.float32
BF16 = jnp.bfloat16

D_MODEL = 2048
BATCH = 4
SEQ = 2048
DEPTH = 1
DEC_BATCH = 32
DEC_SEQ = 64
PAST_LEN = 2048
CHUNK = 64
D_RNN = 1024
LRU_BLOCKS = 16
LRU_BLOCK = D_RNN // LRU_BLOCKS
CONV_W = 4
LRU_C = 8.0
RET_HEADS = 8
RET_DK = 128
RET_DV = 128
D_RET_K = RET_HEADS * RET_DK
D_RET_V = RET_HEADS * RET_DV
D_FF = 5632
DN_ALPHA = (2.0 * DEPTH) ** 0.25
LN_EPS = 1e-5
GN_EPS = 1e-5
ROPE_BASE = 10000.0
D_IN = 2 * D_RNN + 2 * D_RET_K + 2 * D_RET_V + 2 * D_MODEL

M_PROMPT = BATCH * SEQ
M_SAMPLE = DEC_BATCH * DEC_SEQ
M_TOTAL = M_PROMPT + M_SAMPLE

V7X_VMEM_BYTES = 64 * 1024 * 1024
VMEM_LIMIT = V7X_VMEM_BYTES - 8 * 1024 * 1024

FFN_TM = 512
FFN_TF = 512
PROJ_TM = 1024
PROJ_TN = 1024
MIX_R = 256
MERGE_TM = 256
CONV_PAD = 8


def _layer_norm_rows(y, g, b):
    mu = jnp.mean(y, axis=-1, keepdims=True)
    yc = y - mu
    var = jnp.mean(yc * yc, axis=-1, keepdims=True)
    return yc * lax.rsqrt(var + LN_EPS) * g + b


def _ffn_ln_kernel(x_ref, wg_ref, wu_ref, wd_ref, g_ref, b_ref, o_ref, acc_ref, xb_ref):
    f = pl.program_id(1)

    @pl.when(f == 0)
    def _():
        acc_ref[...] = jnp.zeros_like(acc_ref)
        xb_ref[...] = x_ref[...].astype(BF16)

    xb = xb_ref[...]
    gate = jnp.dot(xb, wg_ref[...], preferred_element_type=F32)
    up = jnp.dot(xb, wu_ref[...], preferred_element_type=F32)
    h = (gate * jax.nn.sigmoid(gate) * up).astype(BF16)
    acc_ref[...] += jnp.dot(h, wd_ref[...], preferred_element_type=F32)

    @pl.when(f == pl.num_programs(1) - 1)
    def _():
        y = DN_ALPHA * x_ref[...] + 0.5 * acc_ref[...]
        o_ref[...] = _layer_norm_rows(y, g_ref[...], b_ref[...])


def _ffn_ln(x, wg, wu, wd, ln_g, ln_b):
    m = x.shape[0]
    grid = (m // FFN_TM, D_FF // FFN_TF)
    return pl.pallas_call(
        _ffn_ln_kernel,
        out_shape=jax.ShapeDtypeStruct((m, D_MODEL), F32),
        grid=grid,
        in_specs=[
            pl.BlockSpec((FFN_TM, D_MODEL), lambda i, f: (i, 0)),
            pl.BlockSpec((D_MODEL, FFN_TF), lambda i, f: (0, f)),
            pl.BlockSpec((D_MODEL, FFN_TF), lambda i, f: (0, f)),
            pl.BlockSpec((FFN_TF, D_MODEL), lambda i, f: (f, 0)),
            pl.BlockSpec((1, D_MODEL), lambda i, f: (0, 0)),
            pl.BlockSpec((1, D_MODEL), lambda i, f: (0, 0)),
        ],
        out_specs=pl.BlockSpec((FFN_TM, D_MODEL), lambda i, f: (i, 0)),
        scratch_shapes=[pltpu.VMEM((FFN_TM, D_MODEL), F32), pltpu.VMEM((FFN_TM, D_MODEL), BF16)],
        compiler_params=pltpu.CompilerParams(
            dimension_semantics=("parallel", "arbitrary"), vmem_limit_bytes=VMEM_LIMIT),
        name="ffn_ln",
    )(x, wg, wu, wd, ln_g, ln_b)


def _in_proj_kernel(x_ref, w_ref, o_ref):
    o_ref[...] = jnp.dot(x_ref[...].astype(BF16), w_ref[...], preferred_element_type=F32)


def _in_proj(x1, w_in):
    m = x1.shape[0]
    return pl.pallas_call(
        _in_proj_kernel,
        out_shape=jax.ShapeDtypeStruct((m, D_IN), F32),
        grid=(m // PROJ_TM, D_IN // PROJ_TN),
        in_specs=[
            pl.BlockSpec((PROJ_TM, D_MODEL), lambda i, j: (i, 0)),
            pl.BlockSpec((D_MODEL, PROJ_TN), lambda i, j: (0, j)),
        ],
        out_specs=pl.BlockSpec((PROJ_TM, PROJ_TN), lambda i, j: (i, j)),
        compiler_params=pltpu.CompilerParams(
            dimension_semantics=("parallel", "parallel"), vmem_limit_bytes=VMEM_LIMIT),
        name="in_proj",
    )(x1, w_in)


def _mixer_kernel(*refs, is_prompt):
    n_chunks = MIX_R // CHUNK
    (xa_ref, ga_ref, q_ref, k_ref, v_ref, g_ref, cos_ref, sin_ref,
     dmask_ref, qdec_ref, kdec_ref, cdec_ref,
     convw_ref, convb_ref, wri_ref, bri_ref, lam_ref, gng_ref, gnb_ref) = refs[:19]
    refs = refs[19:]
    if is_prompt:
        conv0_ref = h0_ref = s0_ref = None
    else:
        conv0_ref, h0_ref, s0_ref = refs[:3]
        refs = refs[3:]
    ya_ref, yb_ref, convo_ref, ho_ref, so_ref, xpad_ref, xc_ref, a_ref, u_ref, hs_ref = refs

    if is_prompt:
        @pl.when(pl.program_id(1) == 0)
        def _():
            convo_ref[...] = jnp.zeros_like(convo_ref)
            ho_ref[...] = jnp.zeros_like(ho_ref)
            so_ref[...] = jnp.zeros_like(so_ref)

    n_seg = 1 if is_prompt else n_chunks
    seg_len = MIX_R // n_seg
    conv_b = convb_ref[...]
    w0, w1, w2, w3 = (convw_ref[j:j + 1, :] for j in range(CONV_W))
    lo = CONV_PAD - (CONV_W - 1)
    for sg in range(n_seg):
        rows = slice(sg * seg_len, (sg + 1) * seg_len)
        xpad_ref[sg, lo:CONV_PAD, :] = convo_ref[0] if is_prompt else conv0_ref[sg]
        xpad_ref[sg, CONV_PAD:CONV_PAD + seg_len, :] = xa_ref[rows, :]
        acc = conv_b + w0 * xpad_ref[sg, lo:lo + seg_len, :]
        acc = acc + w1 * xpad_ref[sg, lo + 1:lo + 1 + seg_len, :]
        acc = acc + w2 * xpad_ref[sg, lo + 2:lo + 2 + seg_len, :]
        acc = acc + w3 * xpad_ref[sg, lo + 3:lo + 3 + seg_len, :]
        xc_ref[rows, :] = acc
        convo_ref[sg] = xpad_ref[sg, lo + seg_len:CONV_PAD + seg_len, :]

    xc = xc_ref[...]
    ri = jnp.dot(xc.astype(BF16), wri_ref[...], preferred_element_type=F32) + bri_ref[...]
    r = jax.nn.sigmoid(ri[:, :D_RNN])
    i = jax.nn.sigmoid(ri[:, D_RNN:])
    log_a = -LRU_C * r * jax.nn.softplus(-lam_ref[...])
    a = jnp.exp(log_a)
    a_ref[...] = a
    u_ref[...] = jnp.sqrt(jnp.tanh(-log_a) * (1.0 + a * a)) * (i * xc)

    for sg in range(n_seg):
        h_init = ho_ref[0] if is_prompt else h0_ref[sg]

        def step(t, h, base=sg * seg_len):
            row = pl.ds(base + t, 1)
            h = a_ref[row, :] * h + u_ref[row, :]
            hs_ref[row, :] = h
            return h

        ho_ref[sg] = lax.fori_loop(0, seg_len, step, h_init, unroll=8)

    ya_ref[...] = (hs_ref[...] * jax.nn.gelu(ga_ref[...])).astype(BF16)

    k_scale = RET_DK ** -0.5

    def chunk_body(c, carry):
        rows = pl.ds(pl.multiple_of(c * CHUNK, CHUNK), CHUNK)
        if is_prompt:
            cos, sin = cos_ref[rows, :], sin_ref[rows, :]
        else:
            cos, sin = cos_ref[...], sin_ref[...]
        for hd in range(RET_HEADS):
            cols = slice(hd * RET_DK, (hd + 1) * RET_DK)
            qh, kh, vh = q_ref[rows, cols], k_ref[rows, cols], v_ref[rows, cols]
            qr = qh * cos + pltpu.roll(qh, RET_DK // 2, 1) * sin
            kr = (kh * cos + pltpu.roll(kh, RET_DK // 2, 1) * sin) * k_scale
            vb = vh.astype(BF16)
            s = lax.dot_general(qr.astype(BF16), kr.astype(BF16), (((1,), (1,)), ((), ())),
                                preferred_element_type=F32) * dmask_ref[hd]
            state = so_ref[0, hd] if is_prompt else s0_ref[c, hd]
            o = jnp.dot(s.astype(BF16), vb, preferred_element_type=F32)
            o = o + jnp.dot((qr * qdec_ref[hd]).astype(BF16), state.astype(BF16),
                            preferred_element_type=F32)
            kd_t = jnp.transpose(kr * kdec_ref[hd]).astype(BF16)
            new_state = cdec_ref[hd] * state + jnp.dot(kd_t, vb, preferred_element_type=F32)
            if is_prompt:
                so_ref[0, hd] = new_state
            else:
                so_ref[c, hd] = new_state
            mu = jnp.mean(o, axis=-1, keepdims=True)
            oc = o - mu
            var = jnp.mean(oc * oc, axis=-1, keepdims=True)
            on = oc * lax.rsqrt(var + GN_EPS) * gng_ref[:, cols] + gnb_ref[:, cols]
            gh = g_ref[rows, cols]
            yb_ref[rows, cols] = (gh * jax.nn.sigmoid(gh) * on).astype(BF16)
        return carry

    lax.fori_loop(0, n_chunks, chunk_body, 0)


def _mixer(z, tables, params, states, *, is_prompt):
    cos_t, sin_t, dmask, qdec, kdec, cdec = tables
    n_chunks = MIX_R // CHUNK
    if is_prompt:
        tiles_per_seq = SEQ // MIX_R
        grid = (BATCH, tiles_per_seq)
        n_seq = BATCH
        n_state = 1
        row_blk = lambda s, t: s * tiles_per_seq + t
        zspec = lambda col: pl.BlockSpec((MIX_R, D_RNN), lambda s, t, col=col: (row_blk(s, t), col))
        tab_spec = pl.BlockSpec((MIX_R, RET_DK), lambda s, t: (t, 0))
        state_idx = lambda s, t: s
        out_row = lambda s, t: (row_blk(s, t), 0)
        m_rows = M_PROMPT
        sem = ("parallel", "arbitrary")
    else:
        grid = (M_SAMPLE // MIX_R,)
        n_seq = DEC_BATCH
        n_state = n_chunks
        base_blk = M_PROMPT // MIX_R
        zspec = lambda col: pl.BlockSpec((MIX_R, D_RNN), lambda i, col=col: (base_blk + i, col))
        tab_spec = pl.BlockSpec((CHUNK, RET_DK), lambda i: (PAST_LEN // CHUNK, 0))
        state_idx = lambda i: i
        out_row = lambda i: (i, 0)
        m_rows = M_SAMPLE
        sem = ("parallel",)

    def const_spec(shape):
        nd = len(shape)
        return pl.BlockSpec(shape, lambda *_: (0,) * nd)

    def state_spec(shape):
        nd = len(shape)
        return pl.BlockSpec((n_state,) + shape, lambda *g: (state_idx(*g),) + (0,) * nd)

    in_specs = [zspec(c) for c in range(6)] + [tab_spec, tab_spec] + [
        const_spec(dmask.shape), const_spec(qdec.shape), const_spec(kdec.shape), const_spec(cdec.shape)
    ] + [const_spec(p.shape) for p in params]
    args = [z] * 6 + [cos_t, sin_t, dmask, qdec, kdec, cdec] + list(params)
    if not is_prompt:
        in_specs += [state_spec((CONV_W - 1, D_RNN)), state_spec((1, D_RNN)),
                     state_spec((RET_HEADS, RET_DK, RET_DV))]
        args += list(states)

    n_seg = 1 if is_prompt else n_chunks
    seg_len = MIX_R // n_seg
    out_shape = [
        jax.ShapeDtypeStruct((m_rows, D_RNN), BF16),
        jax.ShapeDtypeStruct((m_rows, D_RET_V), BF16),
        jax.ShapeDtypeStruct((n_seq, CONV_W - 1, D_RNN), F32),
        jax.ShapeDtypeStruct((n_seq, 1, D_RNN), F32),
        jax.ShapeDtypeStruct((n_seq, RET_HEADS, RET_DK, RET_DV), F32),
    ]
    out_specs = [
        pl.BlockSpec((MIX_R, D_RNN), out_row),
        pl.BlockSpec((MIX_R, D_RET_V), out_row),
        state_spec((CONV_W - 1, D_RNN)),
        state_spec((1, D_RNN)),
        state_spec((RET_HEADS, RET_DK, RET_DV)),
    ]
    scratch = [
        pltpu.VMEM((n_seg, CONV_PAD + seg_len, D_RNN), F32),
        pltpu.VMEM((MIX_R, D_RNN), F32),
        pltpu.VMEM((MIX_R, D_RNN), F32),
        pltpu.VMEM((MIX_R, D_RNN), F32),
        pltpu.VMEM((MIX_R, D_RNN), F32),
    ]
    return pl.pallas_call(
        functools.partial(_mixer_kernel, is_prompt=is_prompt),
        out_shape=out_shape,
        grid=grid,
        in_specs=in_specs,
        out_specs=out_specs,
        scratch_shapes=scratch,
        compiler_params=pltpu.CompilerParams(dimension_semantics=sem, vmem_limit_bytes=VMEM_LIMIT),
        name="mixer_prompt" if is_prompt else "mixer_sample",
    )(*args)


def _merge_ln_kernel(ya_ref, yb_ref, ga_ref, gb_ref, x_ref, wa_ref, wb_ref, wo_ref, g_ref, b_ref, o_ref):
    pa = jnp.dot(ya_ref[...], wa_ref[...], preferred_element_type=F32)
    pb = jnp.dot(yb_ref[...], wb_ref[...], preferred_element_type=F32)
    merged = jax.nn.sigmoid(ga_ref[...]) * pa + jax.nn.sigmoid(gb_ref[...]) * pb
    mo = jnp.dot(merged.astype(BF16), wo_ref[...], preferred_element_type=F32)
    o_ref[...] = _layer_norm_rows(DN_ALPHA * x_ref[...] + mo, g_ref[...], b_ref[...])


def _merge_ln(ya, yb, z, x1, wa, wb, wo, ln_g, ln_b):
    m = x1.shape[0]
    gate_a_blk = (2 * D_RNN + 2 * D_RET_K + 2 * D_RET_V) // D_MODEL
    row = lambda i: (i, 0)
    const = lambda i: (0, 0)
    return pl.pallas_call(
        _merge_ln_kernel,
        out_shape=jax.ShapeDtypeStruct((m, D_MODEL), F32),
        grid=(m // MERGE_TM,),
        in_specs=[
            pl.BlockSpec((MERGE_TM, D_RNN), row),
            pl.BlockSpec((MERGE_TM, D_RET_V), row),
            pl.BlockSpec((MERGE_TM, D_MODEL), lambda i: (i, gate_a_blk)),
            pl.BlockSpec((MERGE_TM, D_MODEL), lambda i: (i, gate_a_blk + 1)),
            pl.BlockSpec((MERGE_TM, D_MODEL), row),
            pl.BlockSpec((D_RNN, D_MODEL), const, pipeline_mode=pl.Buffered(1)),
            pl.BlockSpec((D_RET_V, D_MODEL), const, pipeline_mode=pl.Buffered(1)),
            pl.BlockSpec((D_MODEL, D_MODEL), const, pipeline_mode=pl.Buffered(1)),
            pl.BlockSpec((1, D_MODEL), const),
            pl.BlockSpec((1, D_MODEL), const),
        ],
        out_specs=pl.BlockSpec((MERGE_TM, D_MODEL), row),
        compiler_params=pltpu.CompilerParams(
            dimension_semantics=("parallel",), vmem_limit_bytes=VMEM_LIMIT),
        name="merge_ln",
    )(ya, yb, z, z, x1, wa, wb, wo, ln_g, ln_b)


def _rope_tables(pos):
    d = RET_DK
    inv_freq = ROPE_BASE ** (-jnp.arange(0, d, 2, dtype=F32) / d)
    ang = pos.astype(F32)[:, None] * inv_freq[None, :]
    cos, sin = jnp.cos(ang), jnp.sin(ang)
    return jnp.concatenate([cos, cos], axis=-1), jnp.concatenate([-sin, sin], axis=-1)


def _decay_tables():
    log_g = jnp.log1p(-jnp.exp2(-5.0 - jnp.arange(RET_HEADS, dtype=F32)))
    idx = jnp.arange(CHUNK, dtype=F32)
    diff = idx[:, None] - idx[None, :]
    dmask = jnp.where(diff >= 0, jnp.exp(log_g[:, None, None] * jnp.maximum(diff, 0.0)), 0.0)
    q_dec = jnp.exp(log_g[:, None] * (idx[None, :] + 1.0))
    k_dec = jnp.exp(log_g[:, None] * (CHUNK - 1.0 - idx[None, :]))
    chunk_dec = jnp.exp(log_g * CHUNK)
    lanes = (RET_HEADS, CHUNK, RET_DK)
    return (dmask, jnp.broadcast_to(q_dec[:, :, None], lanes), jnp.broadcast_to(k_dec[:, :, None], lanes),
            jnp.broadcast_to(chunk_dec[:, None, None], (RET_HEADS, 1, RET_DV)))


def _block_diag(w):
    eye = jnp.eye(LRU_BLOCKS, dtype=w.dtype)
    return jnp.einsum("nkj,nm->nkmj", w, eye).reshape(D_RNN, D_RNN)


def kernel(x_prompt, x_sample, state_conv, state_lru, state_ret, ffn1_w_gate, ffn1_w_up, ffn1_w_down, ln1_g, ln1_b, w_in, conv_w, conv_b, rg_w, rg_b, ig_w, ig_b, lru_lambda, ret_gn_g, ret_gn_b, w_a_proj, w_b_proj, w_o, ln2_g, ln2_b, ffn2_w_gate, ffn2_w_up, ffn2_w_down, ln3_g, ln3_b):
    assert DEPTH == 1
    bf = lambda w: w[0].astype(BF16)

    x = jnp.concatenate([x_prompt.reshape(M_PROMPT, D_MODEL), x_sample.reshape(M_SAMPLE, D_MODEL)], axis=0)
    x1 = _ffn_ln(x, bf(ffn1_w_gate), bf(ffn1_w_up), bf(ffn1_w_down), ln1_g, ln1_b)
    z = _in_proj(x1, bf(w_in))

    decay = _decay_tables()
    w_ri = jnp.concatenate([_block_diag(rg_w[0]), _block_diag(ig_w[0])], axis=1).astype(BF16)
    b_ri = jnp.concatenate([rg_b, ig_b], axis=1)
    params = (conv_w[0], conv_b, w_ri, b_ri, lru_lambda, ret_gn_g, ret_gn_b)

    cos_p, sin_p = _rope_tables(jnp.arange(PAST_LEN + DEC_SEQ, dtype=jnp.int32))
    tables = (cos_p, sin_p) + decay
    ya_p, yb_p, conv_p, lru_p, ret_p = _mixer(z, tables, params, None, is_prompt=True)
    states = (state_conv[0], state_lru[0][:, None, :], state_ret[0])
    ya_s, yb_s, conv_s, lru_s, ret_s = _mixer(z, tables, params, states, is_prompt=False)

    ya = jnp.concatenate([ya_p, ya_s], axis=0)
    yb = jnp.concatenate([yb_p, yb_s], axis=0)
    x2 = _merge_ln(ya, yb, z, x1, bf(w_a_proj), bf(w_b_proj), bf(w_o), ln2_g, ln2_b)
    y = _ffn_ln(x2, bf(ffn2_w_gate), bf(ffn2_w_up), bf(ffn2_w_down), ln3_g, ln3_b)

    y_prompt = y[:M_PROMPT].reshape(BATCH, SEQ, D_MODEL)
    y_sample = y[M_PROMPT:].reshape(DEC_BATCH, DEC_SEQ, D_MODEL)
    return (y_prompt, y_sample,
            conv_p[None], lru_p.reshape(1, BATCH, D_RNN), ret_p[None],
            conv_s[None], lru_s.reshape(1, DEC_BATCH, D_RNN), ret_s[None])
```

```python
import functools

import jax
import jax.numpy as jnp
from jax import lax
from jax.experimental import pallas as pl
from jax.experimental.pallas import tpu as pltpu

F32 = jnp.float32
BF16 = jnp.bfloat16

D_MODEL = 2048
BATCH = 4
SEQ = 2048
DEPTH = 1
DEC_BATCH = 32
DEC_SEQ = 64
PAST_LEN = 2048
CHUNK = 64
D_RNN = 1024
LRU_BLOCKS = 16
LRU_BLOCK = D_RNN // LRU_BLOCKS
CONV_W = 4
LRU_C = 8.0
RET_HEADS = 8
RET_DK = 128
RET_DV = 128
D_RET_K = RET_HEADS * RET_DK
D_RET_V = RET_HEADS * RET_DV
D_FF = 5632
DN_ALPHA = (2.0 * DEPTH) ** 0.25
LN_EPS = 1e-5
GN_EPS = 1e-5
ROPE_BASE = 10000.0
D_IN = 2 * D_RNN + 2 * D_RET_K + 2 * D_RET_V + 2 * D_MODEL

M_PROMPT = BATCH * SEQ
M_SAMPLE = DEC_BATCH * DEC_SEQ

V7X_VMEM_BYTES = 64 * 1024 * 1024
VMEM_LIMIT = V7X_VMEM_BYTES - 8 * 1024 * 1024
VMEM_LIMIT_BIG = V7X_VMEM_BYTES - 3 * 1024 * 1024

FFN_TM = 1024
FFN_SUB = 512
FFN_TF = 512
LN_ROWS = 128
PROJ_TM = 1024
PROJ_TN = 1024
MIX_R = 256
MERGE_TM = 256
CONV_PAD = 8


def _layer_norm_rows(y, g, b):
    mu = jnp.mean(y, axis=-1, keepdims=True)
    yc = y - mu
    var = jnp.mean(yc * yc, axis=-1, keepdims=True)
    return yc * lax.rsqrt(var + LN_EPS) * g + b


def _ffn_ln_kernel(x_ref, wg_ref, wu_ref, wd_ref, g_ref, b_ref, o_ref, xb_ref):
    f = pl.program_id(1)

    @pl.when(f == 0)
    def _():
        o_ref[...] = jnp.zeros_like(o_ref)
        xb_ref[...] = x_ref[...].astype(BF16)

    for r in range(FFN_TM // FFN_SUB):
        rows = slice(r * FFN_SUB, (r + 1) * FFN_SUB)
        xb = xb_ref[rows, :]
        gate = jnp.dot(xb, wg_ref[...], preferred_element_type=F32)
        up = jnp.dot(xb, wu_ref[...], preferred_element_type=F32)
        h = (gate * jax.nn.sigmoid(gate) * up).astype(BF16)
        o_ref[rows, :] += jnp.dot(h, wd_ref[...], preferred_element_type=F32)

    @pl.when(f == pl.num_programs(1) - 1)
    def _():
        def ln_rows(c, carry):
            rows = pl.ds(pl.multiple_of(c * LN_ROWS, LN_ROWS), LN_ROWS)
            y = DN_ALPHA * x_ref[rows, :] + 0.5 * o_ref[rows, :]
            o_ref[rows, :] = _layer_norm_rows(y, g_ref[...], b_ref[...])
            return carry

        lax.fori_loop(0, FFN_TM // LN_ROWS, ln_rows, 0)


def _ffn_ln(x, wg, wu, wd, ln_g, ln_b):
    m = x.shape[0]
    grid = (m // FFN_TM, D_FF // FFN_TF)
    return pl.pallas_call(
        _ffn_ln_kernel,
        out_shape=jax.ShapeDtypeStruct((m, D_MODEL), F32),
        grid=grid,
        in_specs=[
            pl.BlockSpec((FFN_TM, D_MODEL), lambda i, f: (i, 0)),
            pl.BlockSpec((D_MODEL, FFN_TF), lambda i, f: (0, f)),
            pl.BlockSpec((D_MODEL, FFN_TF), lambda i, f: (0, f)),
            pl.BlockSpec((FFN_TF, D_MODEL), lambda i, f: (f, 0)),
            pl.BlockSpec((1, D_MODEL), lambda i, f: (0, 0)),
            pl.BlockSpec((1, D_MODEL), lambda i, f: (0, 0)),
        ],
        out_specs=pl.BlockSpec((FFN_TM, D_MODEL), lambda i, f: (i, 0)),
        scratch_shapes=[pltpu.VMEM((FFN_TM, D_MODEL), BF16)],
        compiler_params=pltpu.CompilerParams(
            dimension_semantics=("parallel", "arbitrary"), vmem_limit_bytes=VMEM_LIMIT_BIG),
        name="ffn_ln",
    )(x, wg, wu, wd, ln_g, ln_b)


def _in_proj_kernel(x_ref, w_ref, o_ref, xb_ref):
    @pl.when(pl.program_id(1) == 0)
    def _():
        xb_ref[...] = x_ref[...].astype(BF16)

    o_ref[...] = jnp.dot(xb_ref[...], w_ref[...], preferred_element_type=F32)


def _in_proj(x1, w_in):
    m = x1.shape[0]
    return pl.pallas_call(
        _in_proj_kernel,
        out_shape=jax.ShapeDtypeStruct((m, D_IN), F32),
        grid=(m // PROJ_TM, D_IN // PROJ_TN),
        in_specs=[
            pl.BlockSpec((PROJ_TM, D_MODEL), lambda i, j: (i, 0)),
            pl.BlockSpec((D_MODEL, PROJ_TN), lambda i, j: (0, j)),
        ],
        out_specs=pl.BlockSpec((PROJ_TM, PROJ_TN), lambda i, j: (i, j)),
        scratch_shapes=[pltpu.VMEM((PROJ_TM, D_MODEL), BF16)],
        compiler_params=pltpu.CompilerParams(
            dimension_semantics=("parallel", "arbitrary"), vmem_limit_bytes=VMEM_LIMIT),
        name="in_proj",
    )(x1, w_in)


def _mixer_kernel(*refs, is_prompt):
    n_chunks = MIX_R // CHUNK
    (xa_ref, ga_ref, q_ref, k_ref, v_ref, g_ref, cos_ref, sin_ref,
     dmask_ref, qdec_ref, kdec_ref, cdec_ref,
     convw_ref, convb_ref, wri_ref, bri_ref, lam_ref, gng_ref, gnb_ref) = refs[:19]
    refs = refs[19:]
    if is_prompt:
        conv0_ref = h0_ref = s0_ref = None
    else:
        conv0_ref, h0_ref, s0_ref = refs[:3]
        refs = refs[3:]
    ya_ref, yb_ref, convo_ref, ho_ref, so_ref, xpad_ref, xc_ref, a_ref, u_ref, hs_ref = refs

    if is_prompt:
        @pl.when(pl.program_id(1) == 0)
        def _():
            convo_ref[...] = jnp.zeros_like(convo_ref)
            ho_ref[...] = jnp.zeros_like(ho_ref)
            so_ref[...] = jnp.zeros_like(so_ref)

    n_seg = 1 if is_prompt else n_chunks
    seg_len = MIX_R // n_seg
    conv_b = convb_ref[...]
    w0, w1, w2, w3 = (convw_ref[j:j + 1, :] for j in range(CONV_W))
    lo = CONV_PAD - (CONV_W - 1)
    for sg in range(n_seg):
        rows = slice(sg * seg_len, (sg + 1) * seg_len)
        xpad_ref[sg, lo:CONV_PAD, :] = convo_ref[0] if is_prompt else conv0_ref[sg]
        xpad_ref[sg, CONV_PAD:CONV_PAD + seg_len, :] = xa_ref[rows, :]
        acc = conv_b + w0 * xpad_ref[sg, lo:lo + seg_len, :]
        acc = acc + w1 * xpad_ref[sg, lo + 1:lo + 1 + seg_len, :]
        acc = acc + w2 * xpad_ref[sg, lo + 2:lo + 2 + seg_len, :]
        acc = acc + w3 * xpad_ref[sg, lo + 3:lo + 3 + seg_len, :]
        xc_ref[rows, :] = acc
        convo_ref[sg] = xpad_ref[sg, lo + seg_len:CONV_PAD + seg_len, :]

    xc = xc_ref[...]
    ri = jnp.dot(xc.astype(BF16), wri_ref[...], preferred_element_type=F32) + bri_ref[...]
    r = jax.nn.sigmoid(ri[:, :D_RNN])
    i = jax.nn.sigmoid(ri[:, D_RNN:])
    log_a = -LRU_C * r * jax.nn.softplus(-lam_ref[...])
    a = jnp.exp(log_a)
    a_ref[...] = a
    u_ref[...] = jnp.sqrt(jnp.tanh(-log_a) * (1.0 + a * a)) * (i * xc)

    for sg in range(n_seg):
        h_init = ho_ref[0] if is_prompt else h0_ref[sg]

        def step(t, h, base=sg * seg_len):
            row = pl.ds(base + t, 1)
            h = a_ref[row, :] * h + u_ref[row, :]
            hs_ref[row, :] = h
            return h

        ho_ref[sg] = lax.fori_loop(0, seg_len, step, h_init, unroll=8)

    ya_ref[...] = (hs_ref[...] * jax.nn.gelu(ga_ref[...])).astype(BF16)

    k_scale = RET_DK ** -0.5

    def chunk_body(c, carry):
        rows = pl.ds(pl.multiple_of(c * CHUNK, CHUNK), CHUNK)
        if is_prompt:
            cos, sin = cos_ref[rows, :], sin_ref[rows, :]
        else:
            cos, sin = cos_ref[...], sin_ref[...]
        for hd in range(RET_HEADS):
            cols = slice(hd * RET_DK, (hd + 1) * RET_DK)
            qh, kh, vh = q_ref[rows, cols], k_ref[rows, cols], v_ref[rows, cols]
            qr = qh * cos + pltpu.roll(qh, RET_DK // 2, 1) * sin
            kr = (kh * cos + pltpu.roll(kh, RET_DK // 2, 1) * sin) * k_scale
            vb = vh.astype(BF16)
            s = lax.dot_general(qr.astype(BF16), kr.astype(BF16), (((1,), (1,)), ((), ())),
                                preferred_element_type=F32) * dmask_ref[hd]
            state = so_ref[0, hd] if is_prompt else s0_ref[c, hd]
            o = jnp.dot(s.astype(BF16), vb, preferred_element_type=F32)
            o = o + jnp.dot((qr * qdec_ref[hd]).astype(BF16), state.astype(BF16),
                            preferred_element_type=F32)
            kd_t = jnp.transpose(kr * kdec_ref[hd]).astype(BF16)
            new_state = cdec_ref[hd] * state + jnp.dot(kd_t, vb, preferred_element_type=F32)
            if is_prompt:
                so_ref[0, hd] = new_state
            else:
                so_ref[c, hd] = new_state
            mu = jnp.mean(o, axis=-1, keepdims=True)
            oc = o - mu
            var = jnp.mean(oc * oc, axis=-1, keepdims=True)
            on = oc * lax.rsqrt(var + GN_EPS) * gng_ref[:, cols] + gnb_ref[:, cols]
            gh = g_ref[rows, cols]
            yb_ref[rows, cols] = (gh * jax.nn.sigmoid(gh) * on).astype(BF16)
        return carry

    lax.fori_loop(0, n_chunks, chunk_body, 0)


def _mixer(z, tables, params, states, *, is_prompt):
    cos_t, sin_t, dmask, qdec, kdec, cdec = tables
    n_chunks = MIX_R // CHUNK
    if is_prompt:
        tiles_per_seq = SEQ // MIX_R
        grid = (BATCH, tiles_per_seq)
        n_seq = BATCH
        n_state = 1
        row_blk = lambda s, t: s * tiles_per_seq + t
        zspec = lambda col: pl.BlockSpec((MIX_R, D_RNN), lambda s, t, col=col: (row_blk(s, t), col))
        tab_spec = pl.BlockSpec((MIX_R, RET_DK), lambda s, t: (t, 0))
        state_idx = lambda s, t: s
        out_row = lambda s, t: (row_blk(s, t), 0)
        m_rows = M_PROMPT
        sem = ("parallel", "arbitrary")
    else:
        grid = (M_SAMPLE // MIX_R,)
        n_seq = DEC_BATCH
        n_state = n_chunks
        zspec = lambda col: pl.BlockSpec((MIX_R, D_RNN), lambda i, col=col: (i, col))
        tab_spec = pl.BlockSpec((CHUNK, RET_DK), lambda i: (PAST_LEN // CHUNK, 0))
        state_idx = lambda i: i
        out_row = lambda i: (i, 0)
        m_rows = M_SAMPLE
        sem = ("parallel",)

    def const_spec(shape):
        nd = len(shape)
        return pl.BlockSpec(shape, lambda *_: (0,) * nd)

    def state_spec(shape):
        nd = len(shape)
        return pl.BlockSpec((n_state,) + shape, lambda *g: (state_idx(*g),) + (0,) * nd)

    in_specs = [zspec(c) for c in range(6)] + [tab_spec, tab_spec] + [
        const_spec(dmask.shape), const_spec(qdec.shape), const_spec(kdec.shape), const_spec(cdec.shape)
    ] + [const_spec(p.shape) for p in params]
    args = [z] * 6 + [cos_t, sin_t, dmask, qdec, kdec, cdec] + list(params)
    if not is_prompt:
        in_specs += [state_spec((CONV_W - 1, D_RNN)), state_spec((1, D_RNN)),
                     state_spec((RET_HEADS, RET_DK, RET_DV))]
        args += list(states)

    n_seg = 1 if is_prompt else n_chunks
    seg_len = MIX_R // n_seg
    out_shape = [
        jax.ShapeDtypeStruct((m_rows, D_RNN), BF16),
        jax.ShapeDtypeStruct((m_rows, D_RET_V), BF16),
        jax.ShapeDtypeStruct((n_seq, CONV_W - 1, D_RNN), F32),
        jax.ShapeDtypeStruct((n_seq, 1, D_RNN), F32),
        jax.ShapeDtypeStruct((n_seq, RET_HEADS, RET_DK, RET_DV), F32),
    ]
    out_specs = [
        pl.BlockSpec((MIX_R, D_RNN), out_row),
        pl.BlockSpec((MIX_R, D_RET_V), out_row),
        state_spec((CONV_W - 1, D_RNN)),
        state_spec((1, D_RNN)),
        state_spec((RET_HEADS, RET_DK, RET_DV)),
    ]
    scratch = [
        pltpu.VMEM((n_seg, CONV_PAD + seg_len, D_RNN), F32),
        pltpu.VMEM((MIX_R, D_RNN), F32),
        pltpu.VMEM((MIX_R, D_RNN), F32),
        pltpu.VMEM((MIX_R, D_RNN), F32),
        pltpu.VMEM((MIX_R, D_RNN), F32),
    ]
    return pl.pallas_call(
        functools.partial(_mixer_kernel, is_prompt=is_prompt),
        out_shape=out_shape,
        grid=grid,
        in_specs=in_specs,
        out_specs=out_specs,
        scratch_shapes=scratch,
        compiler_params=pltpu.CompilerParams(dimension_semantics=sem, vmem_limit_bytes=VMEM_LIMIT),
        name="mixer_prompt" if is_prompt else "mixer_sample",
    )(*args)


def _merge_ln_kernel(ya_ref, yb_ref, ga_ref, gb_ref, x_ref, wa_ref, wb_ref, wo_ref, g_ref, b_ref, o_ref):
    pa = jnp.dot(ya_ref[...], wa_ref[...], preferred_element_type=F32)
    pb = jnp.dot(yb_ref[...], wb_ref[...], preferred_element_type=F32)
    merged = jax.nn.sigmoid(ga_ref[...]) * pa + jax.nn.sigmoid(gb_ref[...]) * pb
    mo = jnp.dot(merged.astype(BF16), wo_ref[...], preferred_element_type=F32)
    o_ref[...] = _layer_norm_rows(DN_ALPHA * x_ref[...] + mo, g_ref[...], b_ref[...])


def _merge_ln(ya, yb, z, x1, wa, wb, wo, ln_g, ln_b):
    m = x1.shape[0]
    gate_a_blk = (2 * D_RNN + 2 * D_RET_K + 2 * D_RET_V) // D_MODEL
    row = lambda i: (i, 0)
    const = lambda i: (0, 0)
    return pl.pallas_call(
        _merge_ln_kernel,
        out_shape=jax.ShapeDtypeStruct((m, D_MODEL), F32),
        grid=(m // MERGE_TM,),
        in_specs=[
            pl.BlockSpec((MERGE_TM, D_RNN), row),
            pl.BlockSpec((MERGE_TM, D_RET_V), row),
            pl.BlockSpec((MERGE_TM, D_MODEL), lambda i: (i, gate_a_blk)),
            pl.BlockSpec((MERGE_TM, D_MODEL), lambda i: (i, gate_a_blk + 1)),
            pl.BlockSpec((MERGE_TM, D_MODEL), row),
            pl.BlockSpec((D_RNN, D_MODEL), const, pipeline_mode=pl.Buffered(1)),
            pl.BlockSpec((D_RET_V, D_MODEL), const, pipeline_mode=pl.Buffered(1)),
            pl.BlockSpec((D_MODEL, D_MODEL), const, pipeline_mode=pl.Buffered(1)),
            pl.BlockSpec((1, D_MODEL), const),
            pl.BlockSpec((1, D_MODEL), const),
        ],
        out_specs=pl.BlockSpec((MERGE_TM, D_MODEL), row),
        compiler_params=pltpu.CompilerParams(
            dimension_semantics=("parallel",), vmem_limit_bytes=VMEM_LIMIT),
        name="merge_ln",
    )(ya, yb, z, z, x1, wa, wb, wo, ln_g, ln_b)


def _rope_tables(pos):
    d = RET_DK
    inv_freq = ROPE_BASE ** (-jnp.arange(0, d, 2, dtype=F32) / d)
    ang = pos.astype(F32)[:, None] * inv_freq[None, :]
    cos, sin = jnp.cos(ang), jnp.sin(ang)
    return jnp.concatenate([cos, cos], axis=-1), jnp.concatenate([-sin, sin], axis=-1)


def _decay_tables():
    log_g = jnp.log1p(-jnp.exp2(-5.0 - jnp.arange(RET_HEADS, dtype=F32)))
    idx = jnp.arange(CHUNK, dtype=F32)
    diff = idx[:, None] - idx[None, :]
    dmask = jnp.where(diff >= 0, jnp.exp(log_g[:, None, None] * jnp.maximum(diff, 0.0)), 0.0)
    q_dec = jnp.exp(log_g[:, None] * (idx[None, :] + 1.0))
    k_dec = jnp.exp(log_g[:, None] * (CHUNK - 1.0 - idx[None, :]))
    chunk_dec = jnp.exp(log_g * CHUNK)
    lanes = (RET_HEADS, CHUNK, RET_DK)
    return (dmask, jnp.broadcast_to(q_dec[:, :, None], lanes), jnp.broadcast_to(k_dec[:, :, None], lanes),
            jnp.broadcast_to(chunk_dec[:, None, None], (RET_HEADS, 1, RET_DV)))


def _block_diag(w):
    eye = jnp.eye(LRU_BLOCKS, dtype=w.dtype)
    return jnp.einsum("nkj,nm->nkmj", w, eye).reshape(D_RNN, D_RNN)


def kernel(x_prompt, x_sample, state_conv, state_lru, state_ret, ffn1_w_gate, ffn1_w_up, ffn1_w_down, ln1_g, ln1_b, w_in, conv_w, conv_b, rg_w, rg_b, ig_w, ig_b, lru_lambda, ret_gn_g, ret_gn_b, w_a_proj, w_b_proj, w_o, ln2_g, ln2_b, ffn2_w_gate, ffn2_w_up, ffn2_w_down, ln3_g, ln3_b):
    assert DEPTH == 1
    bf = lambda w: w[0].astype(BF16)

    decay = _decay_tables()
    w_ri = jnp.concatenate([_block_diag(rg_w[0]), _block_diag(ig_w[0])], axis=1).astype(BF16)
    b_ri = jnp.concatenate([rg_b, ig_b], axis=1)
    params = (conv_w[0], conv_b, w_ri, b_ri, lru_lambda, ret_gn_g, ret_gn_b)
    tables = _rope_tables(jnp.arange(PAST_LEN + DEC_SEQ, dtype=jnp.int32)) + decay
    ffn1 = (bf(ffn1_w_gate), bf(ffn1_w_up), bf(ffn1_w_down), ln1_g, ln1_b)
    ffn2 = (bf(ffn2_w_gate), bf(ffn2_w_up), bf(ffn2_w_down), ln3_g, ln3_b)
    w_in_b = bf(w_in)
    merge_w = (bf(w_a_proj), bf(w_b_proj), bf(w_o), ln2_g, ln2_b)

    def run(x, states, is_prompt):
        x1 = _ffn_ln(x, *ffn1)
        z = _in_proj(x1, w_in_b)
        ya, yb, conv, lru, ret = _mixer(z, tables, params, states, is_prompt=is_prompt)
        x2 = _merge_ln(ya, yb, z, x1, *merge_w)
        return _ffn_ln(x2, *ffn2), conv, lru, ret

    y_p, conv_p, lru_p, ret_p = run(x_prompt.reshape(M_PROMPT, D_MODEL), None, True)
    states = (state_conv[0], state_lru[0][:, None, :], state_ret[0])
    y_s, conv_s, lru_s, ret_s = run(x_sample.reshape(M_SAMPLE, D_MODEL), states, False)

    return (y_p.reshape(BATCH, SEQ, D_MODEL), y_s.reshape(DEC_BATCH, DEC_SEQ, D_MODEL),
            conv_p[None], lru_p.reshape(1, BATCH, D_RNN), ret_p[None],
            conv_s[None], lru_s.reshape(1, DEC_BATCH, D_RNN), ret_s[None])
```

```python
import functools

import jax
import jax.numpy as jnp
from jax import lax
from jax.experimental import pallas as pl
from jax.experimental.pallas import tpu as pltpu

F32 = jnp.float32
BF16 = jnp.bfloat16

D_MODEL = 2048
BATCH = 4
SEQ = 2048
DEPTH = 1
DEC_BATCH = 32
DEC_SEQ = 64
PAST_LEN = 2048
CHUNK = 64
D_RNN = 1024
LRU_BLOCKS = 16
LRU_BLOCK = D_RNN // LRU_BLOCKS
CONV_W = 4
LRU_C = 8.0
RET_HEADS = 8
RET_DK = 128
RET_DV = 128
D_RET_K = RET_HEADS * RET_DK
D_RET_V = RET_HEADS * RET_DV
D_FF = 5632
DN_ALPHA = (2.0 * DEPTH) ** 0.25
LN_EPS = 1e-5
GN_EPS = 1e-5
ROPE_BASE = 10000.0
D_IN = 2 * D_RNN + 2 * D_RET_K + 2 * D_RET_V + 2 * D_MODEL

M_PROMPT = BATCH * SEQ
M_SAMPLE = DEC_BATCH * DEC_SEQ

V7X_VMEM_BYTES = 64 * 1024 * 1024
VMEM_LIMIT = V7X_VMEM_BYTES - 8 * 1024 * 1024
VMEM_LIMIT_BIG = V7X_VMEM_BYTES - 3 * 1024 * 1024

FFN_TM = 1024
FFN_SUB = 512
FFN_TF = 512
FFN_TF_EXPORT = 256
LN_ROWS = 128
PROJ_TM = 1024
PROJ_TN = 1024
PROJ_TN_EXPORT = 512
MIX_R = 256
MERGE_TM = 256
CONV_PAD = 8


def _layer_norm_rows(y, g, b):
    mu = jnp.mean(y, axis=-1, keepdims=True)
    yc = y - mu
    var = jnp.mean(yc * yc, axis=-1, keepdims=True)
    return yc * lax.rsqrt(var + LN_EPS) * g + b


def _ffn_ln_kernel(x_ref, wg_ref, wu_ref, wd_ref, g_ref, b_ref, o_ref, *rest, export):
    i, f = pl.program_id(0), pl.program_id(1)
    xb_ref = rest[-1]

    @pl.when(f == 0)
    def _():
        o_ref[...] = jnp.zeros_like(o_ref)
        xb_ref[...] = x_ref[...].astype(BF16)

    if export:
        wg, wu, wd = (w[...].astype(BF16) for w in (wg_ref, wu_ref, wd_ref))

        @pl.when(i == 0)
        def _():
            for dst, w in zip(rest[:3], (wg, wu, wd)):
                dst[...] = w
    else:
        wg, wu, wd = wg_ref[...], wu_ref[...], wd_ref[...]

    for r in range(FFN_TM // FFN_SUB):
        rows = slice(r * FFN_SUB, (r + 1) * FFN_SUB)
        xb = xb_ref[rows, :]
        gate = jnp.dot(xb, wg, preferred_element_type=F32)
        up = jnp.dot(xb, wu, preferred_element_type=F32)
        h = (gate * jax.nn.sigmoid(gate) * up).astype(BF16)
        o_ref[rows, :] += jnp.dot(h, wd, preferred_element_type=F32)

    @pl.when(f == pl.num_programs(1) - 1)
    def _():
        def ln_rows(c, carry):
            rows = pl.ds(pl.multiple_of(c * LN_ROWS, LN_ROWS), LN_ROWS)
            y = DN_ALPHA * x_ref[rows, :] + 0.5 * o_ref[rows, :]
            o_ref[rows, :] = _layer_norm_rows(y, g_ref[...], b_ref[...])
            return carry

        lax.fori_loop(0, FFN_TM // LN_ROWS, ln_rows, 0)


def _ffn_ln(x, wg, wu, wd, ln_g, ln_b, *, export=False):
    m = x.shape[0]
    tf = FFN_TF_EXPORT if export else FFN_TF
    nf = D_FF // tf
    grid = (m // FFN_TM, nf)
    out_shape = [jax.ShapeDtypeStruct((m, D_MODEL), F32)]
    out_specs = [pl.BlockSpec((FFN_TM, D_MODEL), lambda i, f: (i, 0))]
    x_mode = {}
    if export:
        pin = lambda i, f: jnp.where(i == 0, f, nf - 1)
        out_shape += [jax.ShapeDtypeStruct(w.shape, BF16) for w in (wg, wu, wd)]
        out_specs += [
            pl.BlockSpec((D_MODEL, tf), lambda i, f: (0, pin(i, f))),
            pl.BlockSpec((D_MODEL, tf), lambda i, f: (0, pin(i, f))),
            pl.BlockSpec((tf, D_MODEL), lambda i, f: (pin(i, f), 0)),
        ]
        x_mode = dict(pipeline_mode=pl.Buffered(1))
    out = pl.pallas_call(
        functools.partial(_ffn_ln_kernel, export=export),
        out_shape=out_shape,
        grid=grid,
        in_specs=[
            pl.BlockSpec((FFN_TM, D_MODEL), lambda i, f: (i, 0), **x_mode),
            pl.BlockSpec((D_MODEL, tf), lambda i, f: (0, f)),
            pl.BlockSpec((D_MODEL, tf), lambda i, f: (0, f)),
            pl.BlockSpec((tf, D_MODEL), lambda i, f: (f, 0)),
            pl.BlockSpec((1, D_MODEL), lambda i, f: (0, 0)),
            pl.BlockSpec((1, D_MODEL), lambda i, f: (0, 0)),
        ],
        out_specs=out_specs,
        scratch_shapes=[pltpu.VMEM((FFN_TM, D_MODEL), BF16)],
        compiler_params=pltpu.CompilerParams(
            dimension_semantics=("arbitrary" if export else "parallel", "arbitrary"),
            vmem_limit_bytes=VMEM_LIMIT_BIG),
        name="ffn_ln_export" if export else "ffn_ln",
    )(x, wg, wu, wd, ln_g, ln_b)
    return out if export else out[0]


def _in_proj_kernel(x_ref, w_ref, o_ref, *rest, export):
    xb_ref = rest[-1]

    @pl.when(pl.program_id(1) == 0)
    def _():
        xb_ref[...] = x_ref[...].astype(BF16)

    if export:
        w = w_ref[...].astype(BF16)
        rest[0][...] = w
    else:
        w = w_ref[...]
    o_ref[...] = jnp.dot(xb_ref[...], w, preferred_element_type=F32)


def _in_proj(x1, w_in, *, export=False):
    m = x1.shape[0]
    tm, tn = (m, PROJ_TN_EXPORT) if export else (PROJ_TM, PROJ_TN)
    assert not export or m == tm
    out_shape = [jax.ShapeDtypeStruct((m, D_IN), F32)]
    out_specs = [pl.BlockSpec((tm, tn), lambda i, j: (i, j))]
    x_mode = {}
    if export:
        out_shape.append(jax.ShapeDtypeStruct(w_in.shape, BF16))
        out_specs.append(pl.BlockSpec((D_MODEL, tn), lambda i, j: (0, j)))
        x_mode = dict(pipeline_mode=pl.Buffered(1))
    out = pl.pallas_call(
        functools.partial(_in_proj_kernel, export=export),
        out_shape=out_shape,
        grid=(m // tm, D_IN // tn),
        in_specs=[
            pl.BlockSpec((tm, D_MODEL), lambda i, j: (i, 0), **x_mode),
            pl.BlockSpec((D_MODEL, tn), lambda i, j: (0, j)),
        ],
        out_specs=out_specs,
        scratch_shapes=[pltpu.VMEM((tm, D_MODEL), BF16)],
        compiler_params=pltpu.CompilerParams(
            dimension_semantics=("parallel", "arbitrary"), vmem_limit_bytes=VMEM_LIMIT),
        name="in_proj_export" if export else "in_proj",
    )(x1, w_in)
    return out if export else out[0]


def _mixer_kernel(*refs, is_prompt, n_cast):
    n_chunks = MIX_R // CHUNK
    (xa_ref, ga_ref, q_ref, k_ref, v_ref, g_ref, cos_ref, sin_ref,
     dmask_ref, qdec_ref, kdec_ref, cdec_ref,
     convw_ref, convb_ref, wri_ref, bri_ref, lam_ref, gng_ref, gnb_ref) = refs[:19]
    refs = refs[19:]
    if is_prompt:
        conv0_ref = h0_ref = s0_ref = None
    else:
        conv0_ref, h0_ref, s0_ref = refs[:3]
        refs = refs[3:]
    cast_src, refs = refs[:n_cast], refs[n_cast:]
    ya_ref, yb_ref, convo_ref, ho_ref, so_ref = refs[:5]
    cast_dst, refs = refs[5:5 + n_cast], refs[5 + n_cast:]
    xpad_ref, xc_ref, a_ref, u_ref, hs_ref = refs

    for src, dst in zip(cast_src, cast_dst):
        dst[...] = src[...].astype(BF16)

    if is_prompt:
        @pl.when(pl.program_id(1) == 0)
        def _():
            convo_ref[...] = jnp.zeros_like(convo_ref)
            ho_ref[...] = jnp.zeros_like(ho_ref)
            so_ref[...] = jnp.zeros_like(so_ref)

    n_seg = 1 if is_prompt else n_chunks
    seg_len = MIX_R // n_seg
    conv_b = convb_ref[...]
    w0, w1, w2, w3 = (convw_ref[j:j + 1, :] for j in range(CONV_W))
    lo = CONV_PAD - (CONV_W - 1)
    for sg in range(n_seg):
        rows = slice(sg * seg_len, (sg + 1) * seg_len)
        xpad_ref[sg, lo:CONV_PAD, :] = convo_ref[0] if is_prompt else conv0_ref[sg]
        xpad_ref[sg, CONV_PAD:CONV_PAD + seg_len, :] = xa_ref[rows, :]
        acc = conv_b + w0 * xpad_ref[sg, lo:lo + seg_len, :]
        acc = acc + w1 * xpad_ref[sg, lo + 1:lo + 1 + seg_len, :]
        acc = acc + w2 * xpad_ref[sg, lo + 2:lo + 2 + seg_len, :]
        acc = acc + w3 * xpad_ref[sg, lo + 3:lo + 3 + seg_len, :]
        xc_ref[rows, :] = acc
        convo_ref[sg] = xpad_ref[sg, lo + seg_len:CONV_PAD + seg_len, :]

    xc = xc_ref[...]
    ri = jnp.dot(xc.astype(BF16), wri_ref[...], preferred_element_type=F32) + bri_ref[...]
    r = jax.nn.sigmoid(ri[:, :D_RNN])
    i = jax.nn.sigmoid(ri[:, D_RNN:])
    log_a = -LRU_C * r * jax.nn.softplus(-lam_ref[...])
    a = jnp.exp(log_a)
    a_ref[...] = a
    w = jnp.tanh(-log_a) * (1.0 + a * a)
    u_ref[...] = jnp.where(w > 0.0, w * lax.rsqrt(w), 0.0) * (i * xc)

    for sg in range(n_seg):
        h_init = ho_ref[0] if is_prompt else h0_ref[sg]

        def step(t, h, base=sg * seg_len):
            row = pl.ds(base + t, 1)
            h = a_ref[row, :] * h + u_ref[row, :]
            hs_ref[row, :] = h
            return h

        ho_ref[sg] = lax.fori_loop(0, seg_len, step, h_init, unroll=8)

    ya_ref[...] = (hs_ref[...] * jax.nn.gelu(ga_ref[...])).astype(BF16)

    k_scale = RET_DK ** -0.5

    blk = MIX_R if is_prompt else CHUNK

    def ret_block(c, carry):
        rows = pl.ds(pl.multiple_of(c * blk, blk), blk)
        if is_prompt:
            cos, sin = cos_ref[rows, :], sin_ref[rows, :]
        else:
            cos, sin = cos_ref[...], sin_ref[...]
        for hd in range(RET_HEADS):
            cols = slice(hd * RET_DK, (hd + 1) * RET_DK)
            qh, kh, vh = q_ref[rows, cols], k_ref[rows, cols], v_ref[rows, cols]
            qr = qh * cos + pltpu.roll(qh, RET_DK // 2, 1) * sin
            kr = (kh * cos + pltpu.roll(kh, RET_DK // 2, 1) * sin) * k_scale
            vb = vh.astype(BF16)
            s = lax.dot_general(qr.astype(BF16), kr.astype(BF16), (((1,), (1,)), ((), ())),
                                preferred_element_type=F32) * dmask_ref[hd]
            state = so_ref[0, hd] if is_prompt else s0_ref[c, hd]
            o = jnp.dot(s.astype(BF16), vb, preferred_element_type=F32)
            o = o + jnp.dot((qr * qdec_ref[hd]).astype(BF16), state.astype(BF16),
                            preferred_element_type=F32)
            kd = (kr * kdec_ref[hd]).astype(BF16)
            new_state = cdec_ref[hd] * state + lax.dot_general(
                kd, vb, (((0,), (0,)), ((), ())), preferred_element_type=F32)
            if is_prompt:
                so_ref[0, hd] = new_state
            else:
                so_ref[c, hd] = new_state
            mu = jnp.mean(o, axis=-1, keepdims=True)
            oc = o - mu
            var = jnp.mean(oc * oc, axis=-1, keepdims=True)
            on = oc * lax.rsqrt(var + GN_EPS) * gng_ref[:, cols] + gnb_ref[:, cols]
            gh = g_ref[rows, cols]
            yb_ref[rows, cols] = (gh * jax.nn.sigmoid(gh) * on).astype(BF16)
        return carry

    lax.fori_loop(0, MIX_R // blk, ret_block, 0)


def _mixer(z, tables, params, states, *, is_prompt, cast_srcs=()):
    cos_t, sin_t, dmask, qdec, kdec, cdec = tables
    n_chunks = MIX_R // CHUNK
    if is_prompt:
        tiles_per_seq = SEQ // MIX_R
        grid = (BATCH, tiles_per_seq)
        n_seq = BATCH
        n_state = 1
        row_blk = lambda s, t: s * tiles_per_seq + t
        zspec = lambda col: pl.BlockSpec((MIX_R, D_RNN), lambda s, t, col=col: (row_blk(s, t), col))
        tab_spec = pl.BlockSpec((MIX_R, RET_DK), lambda s, t: (t, 0))
        state_idx = lambda s, t: s
        out_row = lambda s, t: (row_blk(s, t), 0)
        m_rows = M_PROMPT
        sem = ("parallel", "arbitrary")
    else:
        grid = (M_SAMPLE // MIX_R,)
        n_seq = DEC_BATCH
        n_state = n_chunks
        zspec = lambda col: pl.BlockSpec((MIX_R, D_RNN), lambda i, col=col: (i, col))
        tab_spec = pl.BlockSpec((CHUNK, RET_DK), lambda i: (PAST_LEN // CHUNK, 0))
        state_idx = lambda i: i
        out_row = lambda i: (i, 0)
        m_rows = M_SAMPLE
        sem = ("parallel",)

    def const_spec(shape):
        nd = len(shape)
        return pl.BlockSpec(shape, lambda *_: (0,) * nd)

    def state_spec(shape):
        nd = len(shape)
        return pl.BlockSpec((n_state,) + shape, lambda *g: (state_idx(*g),) + (0,) * nd)

    in_specs = [zspec(c) for c in range(6)] + [tab_spec, tab_spec] + [
        const_spec(dmask.shape), const_spec(qdec.shape), const_spec(kdec.shape), const_spec(cdec.shape)
    ] + [const_spec(p.shape) for p in params]
    args = [z] * 6 + [cos_t, sin_t, dmask, qdec, kdec, cdec] + list(params)
    if not is_prompt:
        in_specs += [state_spec((CONV_W - 1, D_RNN)), state_spec((1, D_RNN)),
                     state_spec((RET_HEADS, RET_DK, RET_DV))]
        args += list(states)

    n_steps = 1
    for g in grid:
        n_steps *= g
    step_idx = (lambda s, t: (s * grid[1] + t, 0)) if is_prompt else (lambda i: (i, 0))
    cast_specs = [pl.BlockSpec((w.shape[0] // n_steps, w.shape[1]), step_idx) for w in cast_srcs]
    in_specs += cast_specs
    args += list(cast_srcs)

    n_seg = 1 if is_prompt else n_chunks
    seg_len = MIX_R // n_seg
    out_shape = [
        jax.ShapeDtypeStruct((m_rows, D_RNN), BF16),
        jax.ShapeDtypeStruct((m_rows, D_RET_V), BF16),
        jax.ShapeDtypeStruct((n_seq, CONV_W - 1, D_RNN), F32),
        jax.ShapeDtypeStruct((n_seq, 1, D_RNN), F32),
        jax.ShapeDtypeStruct((n_seq, RET_HEADS, RET_DK, RET_DV), F32),
    ]
    out_specs = [
        pl.BlockSpec((MIX_R, D_RNN), out_row),
        pl.BlockSpec((MIX_R, D_RET_V), out_row),
        state_spec((CONV_W - 1, D_RNN)),
        state_spec((1, D_RNN)),
        state_spec((RET_HEADS, RET_DK, RET_DV)),
    ] + cast_specs
    out_shape += [jax.ShapeDtypeStruct(w.shape, BF16) for w in cast_srcs]
    scratch = [
        pltpu.VMEM((n_seg, CONV_PAD + seg_len, D_RNN), F32),
        pltpu.VMEM((MIX_R, D_RNN), F32),
        pltpu.VMEM((MIX_R, D_RNN), F32),
        pltpu.VMEM((MIX_R, D_RNN), F32),
        pltpu.VMEM((MIX_R, D_RNN), F32),
    ]
    return pl.pallas_call(
        functools.partial(_mixer_kernel, is_prompt=is_prompt, n_cast=len(cast_srcs)),
        out_shape=out_shape,
        grid=grid,
        in_specs=in_specs,
        out_specs=out_specs,
        scratch_shapes=scratch,
        compiler_params=pltpu.CompilerParams(dimension_semantics=sem, vmem_limit_bytes=VMEM_LIMIT),
        name="mixer_prompt" if is_prompt else "mixer_sample",
    )(*args)


def _merge_ln_kernel(ya_ref, yb_ref, ga_ref, gb_ref, x_ref, wa_ref, wb_ref, wo_ref, g_ref, b_ref, o_ref):
    pa = jnp.dot(ya_ref[...], wa_ref[...], preferred_element_type=F32)
    pb = jnp.dot(yb_ref[...], wb_ref[...], preferred_element_type=F32)
    merged = jax.nn.sigmoid(ga_ref[...]) * pa + jax.nn.sigmoid(gb_ref[...]) * pb
    mo = jnp.dot(merged.astype(BF16), wo_ref[...], preferred_element_type=F32)
    o_ref[...] = _layer_norm_rows(DN_ALPHA * x_ref[...] + mo, g_ref[...], b_ref[...])


def _merge_ln(ya, yb, z, x1, wa, wb, wo, ln_g, ln_b):
    m = x1.shape[0]
    gate_a_blk = (2 * D_RNN + 2 * D_RET_K + 2 * D_RET_V) // D_MODEL
    row = lambda i: (i, 0)
    const = lambda i: (0, 0)
    return pl.pallas_call(
        _merge_ln_kernel,
        out_shape=jax.ShapeDtypeStruct((m, D_MODEL), F32),
        grid=(m // MERGE_TM,),
        in_specs=[
            pl.BlockSpec((MERGE_TM, D_RNN), row),
            pl.BlockSpec((MERGE_TM, D_RET_V), row),
            pl.BlockSpec((MERGE_TM, D_MODEL), lambda i: (i, gate_a_blk)),
            pl.BlockSpec((MERGE_TM, D_MODEL), lambda i: (i, gate_a_blk + 1)),
            pl.BlockSpec((MERGE_TM, D_MODEL), row),
            pl.BlockSpec((D_RNN, D_MODEL), const, pipeline_mode=pl.Buffered(1)),
            pl.BlockSpec((D_RET_V, D_MODEL), const, pipeline_mode=pl.Buffered(1)),
            pl.BlockSpec((D_MODEL, D_MODEL), const, pipeline_mode=pl.Buffered(1)),
            pl.BlockSpec((1, D_MODEL), const),
            pl.BlockSpec((1, D_MODEL), const),
        ],
        out_specs=pl.BlockSpec((MERGE_TM, D_MODEL), row),
        compiler_params=pltpu.CompilerParams(
            dimension_semantics=("parallel",), vmem_limit_bytes=VMEM_LIMIT),
        name="merge_ln",
    )(ya, yb, z, z, x1, wa, wb, wo, ln_g, ln_b)


def _rope_tables(pos):
    d = RET_DK
    inv_freq = ROPE_BASE ** (-jnp.arange(0, d, 2, dtype=F32) / d)
    ang = pos.astype(F32)[:, None] * inv_freq[None, :]
    cos, sin = jnp.cos(ang), jnp.sin(ang)
    return jnp.concatenate([cos, cos], axis=-1), jnp.concatenate([-sin, sin], axis=-1)


def _decay_tables(blk):
    log_g = jnp.log1p(-jnp.exp2(-5.0 - jnp.arange(RET_HEADS, dtype=F32)))
    idx = jnp.arange(blk, dtype=F32)
    diff = idx[:, None] - idx[None, :]
    dmask = jnp.where(diff >= 0, jnp.exp(log_g[:, None, None] * jnp.maximum(diff, 0.0)), 0.0)
    q_dec = jnp.exp(log_g[:, None] * (idx[None, :] + 1.0))
    k_dec = jnp.exp(log_g[:, None] * (blk - 1.0 - idx[None, :]))
    chunk_dec = jnp.exp(log_g * blk)
    lanes = (RET_HEADS, blk, RET_DK)
    return (dmask, jnp.broadcast_to(q_dec[:, :, None], lanes), jnp.broadcast_to(k_dec[:, :, None], lanes),
            jnp.broadcast_to(chunk_dec[:, None, None], (RET_HEADS, 1, RET_DV)))


def _block_diag(w):
    eye = jnp.eye(LRU_BLOCKS, dtype=w.dtype)
    return jnp.einsum("nkj,nm->nkmj", w, eye).reshape(D_RNN, D_RNN)


def kernel(x_prompt, x_sample, state_conv, state_lru, state_ret, ffn1_w_gate, ffn1_w_up, ffn1_w_down, ln1_g, ln1_b, w_in, conv_w, conv_b, rg_w, rg_b, ig_w, ig_b, lru_lambda, ret_gn_g, ret_gn_b, w_a_proj, w_b_proj, w_o, ln2_g, ln2_b, ffn2_w_gate, ffn2_w_up, ffn2_w_down, ln3_g, ln3_b):
    assert DEPTH == 1
    w_ri = jnp.concatenate([_block_diag(rg_w[0]), _block_diag(ig_w[0])], axis=1).astype(BF16)
    b_ri = jnp.concatenate([rg_b, ig_b], axis=1)
    params = (conv_w[0], conv_b, w_ri, b_ri, lru_lambda, ret_gn_g, ret_gn_b)
    rope = _rope_tables(jnp.arange(PAST_LEN + DEC_SEQ, dtype=jnp.int32))
    x_p = x_prompt.reshape(M_PROMPT, D_MODEL)
    x_s = x_sample.reshape(M_SAMPLE, D_MODEL)
    states = (state_conv[0], state_lru[0][:, None, :], state_ret[0])

    x1_s, *ffn1_w = _ffn_ln(x_s, ffn1_w_gate[0], ffn1_w_up[0], ffn1_w_down[0], ln1_g, ln1_b, export=True)
    x1_p = _ffn_ln(x_p, *ffn1_w, ln1_g, ln1_b)
    z_s, w_in_b = _in_proj(x1_s, w_in[0], export=True)
    z_p = _in_proj(x1_p, w_in_b)

    later_w = (ffn2_w_gate[0], ffn2_w_up[0], ffn2_w_down[0], w_a_proj[0], w_b_proj[0], w_o[0])
    ya_p, yb_p, conv_p, lru_p, ret_p, *later_b = _mixer(
        z_p, rope + _decay_tables(MIX_R), params, None, is_prompt=True, cast_srcs=later_w)
    ya_s, yb_s, conv_s, lru_s, ret_s = _mixer(
        z_s, rope + _decay_tables(CHUNK), params, states, is_prompt=False)
    ffn2_w, merge_w = later_b[:3], later_b[3:]

    def tail(ya, yb, z, x1):
        x2 = _merge_ln(ya, yb, z, x1, *merge_w, ln2_g, ln2_b)
        return _ffn_ln(x2, *ffn2_w, ln3_g, ln3_b)

    y_p = tail(ya_p, yb_p, z_p, x1_p)
    y_s = tail(ya_s, yb_s, z_s, x1_s)

    return (y_p.reshape(BATCH, SEQ, D_MODEL), y_s.reshape(DEC_BATCH, DEC_SEQ, D_MODEL),
            conv_p[None], lru_p.reshape(1, BATCH, D_RNN), ret_p[None],
            conv_s[None], lru_s.reshape(1, DEC_BATCH, D_RNN), ret_s[None])
```

```python
import functools

import jax
import jax.numpy as jnp
from jax import lax
from jax.experimental import pallas as pl
from jax.experimental.pallas import tpu as pltpu

F32 = jnp.float32
BF16 = jnp.bfloat16

D_MODEL = 2048
BATCH = 4
SEQ = 2048
DEPTH = 1
DEC_BATCH = 32
DEC_SEQ = 64
PAST_LEN = 2048
CHUNK = 64
D_RNN = 1024
LRU_BLOCKS = 16
LRU_BLOCK = D_RNN // LRU_BLOCKS
CONV_W = 4
LRU_C = 8.0
RET_HEADS = 8
RET_DK = 128
RET_DV = 128
D_RET_K = RET_HEADS * RET_DK
D_RET_V = RET_HEADS * RET_DV
D_FF = 5632
DN_ALPHA = (2.0 * DEPTH) ** 0.25
LN_EPS = 1e-5
GN_EPS = 1e-5
ROPE_BASE = 10000.0
D_IN = 2 * D_RNN + 2 * D_RET_K + 2 * D_RET_V + 2 * D_MODEL

M_PROMPT = BATCH * SEQ
M_SAMPLE = DEC_BATCH * DEC_SEQ

V7X_VMEM_BYTES = 64 * 1024 * 1024
VMEM_LIMIT = V7X_VMEM_BYTES - 8 * 1024 * 1024
VMEM_LIMIT_BIG = V7X_VMEM_BYTES - 3 * 1024 * 1024

FFN_TM = 1024
FFN_SUB = 512
FFN_TF = 512
FFN_TF_EXPORT = 256
LN_ROWS = 128
PROJ_TM = 1024
PROJ_TN = 1024
PROJ_TN_EXPORT = 512
MIX_R = 256
MERGE_TM = 256
MERGE_SUB = 128
CONV_PAD = 8


def _layer_norm_rows(y, g, b):
    mu = jnp.mean(y, axis=-1, keepdims=True)
    yc = y - mu
    var = jnp.mean(yc * yc, axis=-1, keepdims=True)
    return yc * lax.rsqrt(var + LN_EPS) * g + b


def _ffn_ln_kernel(x_ref, wg_ref, wu_ref, wd_ref, g_ref, b_ref, o_ref, *rest, export):
    i, f = pl.program_id(0), pl.program_id(1)
    xb_ref = rest[-1]

    @pl.when(f == 0)
    def _():
        x = x_ref[...]
        o_ref[...] = (2.0 * DN_ALPHA) * x
        xb_ref[...] = x.astype(BF16)

    if export:
        wg, wu, wd = (w[...].astype(BF16) for w in (wg_ref, wu_ref, wd_ref))

        @pl.when(i == 0)
        def _():
            for dst, w in zip(rest[:3], (wg, wu, wd)):
                dst[...] = w
    else:
        wg, wu, wd = wg_ref[...], wu_ref[...], wd_ref[...]

    for r in range(FFN_TM // FFN_SUB):
        rows = slice(r * FFN_SUB, (r + 1) * FFN_SUB)
        xb = xb_ref[rows, :]
        gate = jnp.dot(xb, wg, preferred_element_type=F32)
        up = jnp.dot(xb, wu, preferred_element_type=F32)
        h = (gate * jax.nn.sigmoid(gate) * up).astype(BF16)
        o_ref[rows, :] += jnp.dot(h, wd, preferred_element_type=F32)

    @pl.when(f == pl.num_programs(1) - 1)
    def _():
        def ln_rows(c, carry):
            rows = pl.ds(pl.multiple_of(c * LN_ROWS, LN_ROWS), LN_ROWS)
            acc = o_ref[rows, :]
            mu = jnp.mean(acc, axis=-1, keepdims=True)
            ac = acc - mu
            var = jnp.mean(ac * ac, axis=-1, keepdims=True)
            scale = 0.5 * lax.rsqrt(0.25 * var + LN_EPS)
            o_ref[rows, :] = ac * scale * g_ref[...] + b_ref[...]
            return carry

        lax.fori_loop(0, FFN_TM // LN_ROWS, ln_rows, 0)


def _ffn_ln(x, wg, wu, wd, ln_g, ln_b, *, export=False):
    m = x.shape[0]
    tf = FFN_TF_EXPORT if export else FFN_TF
    nf = D_FF // tf
    grid = (m // FFN_TM, nf)
    out_shape = [jax.ShapeDtypeStruct((m, D_MODEL), F32)]
    out_specs = [pl.BlockSpec((FFN_TM, D_MODEL), lambda i, f: (i, 0))]
    x_mode = {}
    if export:
        pin = lambda i, f: jnp.where(i == 0, f, nf - 1)
        out_shape += [jax.ShapeDtypeStruct(w.shape, BF16) for w in (wg, wu, wd)]
        out_specs += [
            pl.BlockSpec((D_MODEL, tf), lambda i, f: (0, pin(i, f))),
            pl.BlockSpec((D_MODEL, tf), lambda i, f: (0, pin(i, f))),
            pl.BlockSpec((tf, D_MODEL), lambda i, f: (pin(i, f), 0)),
        ]
        x_mode = dict(pipeline_mode=pl.Buffered(1))
    out = pl.pallas_call(
        functools.partial(_ffn_ln_kernel, export=export),
        out_shape=out_shape,
        grid=grid,
        in_specs=[
            pl.BlockSpec((FFN_TM, D_MODEL), lambda i, f: (i, 0), **x_mode),
            pl.BlockSpec((D_MODEL, tf), lambda i, f: (0, f)),
            pl.BlockSpec((D_MODEL, tf), lambda i, f: (0, f)),
            pl.BlockSpec((tf, D_MODEL), lambda i, f: (f, 0)),
            pl.BlockSpec((1, D_MODEL), lambda i, f: (0, 0)),
            pl.BlockSpec((1, D_MODEL), lambda i, f: (0, 0)),
        ],
        out_specs=out_specs,
        scratch_shapes=[pltpu.VMEM((FFN_TM, D_MODEL), BF16)],
        compiler_params=pltpu.CompilerParams(
            dimension_semantics=("arbitrary" if export else "parallel", "arbitrary"),
            vmem_limit_bytes=VMEM_LIMIT_BIG),
        name="ffn_ln_export" if export else "ffn_ln",
    )(x, wg, wu, wd, ln_g, ln_b)
    return out if export else out[0]


def _in_proj_kernel(x_ref, w_ref, o_ref, *rest, export):
    xb_ref = rest[-1]

    @pl.when(pl.program_id(1) == 0)
    def _():
        xb_ref[...] = x_ref[...].astype(BF16)

    if export:
        w = w_ref[...].astype(BF16)
        rest[0][...] = w
    else:
        w = w_ref[...]
    o_ref[...] = jnp.dot(xb_ref[...], w, preferred_element_type=F32)


def _in_proj(x1, w_in, *, export=False):
    m = x1.shape[0]
    tm, tn = (m, PROJ_TN_EXPORT) if export else (PROJ_TM, PROJ_TN)
    assert not export or m == tm
    out_shape = [jax.ShapeDtypeStruct((m, D_IN), F32)]
    out_specs = [pl.BlockSpec((tm, tn), lambda i, j: (i, j))]
    x_mode = {}
    if export:
        out_shape.append(jax.ShapeDtypeStruct(w_in.shape, BF16))
        out_specs.append(pl.BlockSpec((D_MODEL, tn), lambda i, j: (0, j)))
        x_mode = dict(pipeline_mode=pl.Buffered(1))
    out = pl.pallas_call(
        functools.partial(_in_proj_kernel, export=export),
        out_shape=out_shape,
        grid=(m // tm, D_IN // tn),
        in_specs=[
            pl.BlockSpec((tm, D_MODEL), lambda i, j: (i, 0), **x_mode),
            pl.BlockSpec((D_MODEL, tn), lambda i, j: (0, j)),
        ],
        out_specs=out_specs,
        scratch_shapes=[pltpu.VMEM((tm, D_MODEL), BF16)],
        compiler_params=pltpu.CompilerParams(
            dimension_semantics=("parallel", "arbitrary"), vmem_limit_bytes=VMEM_LIMIT),
        name="in_proj_export" if export else "in_proj",
    )(x1, w_in)
    return out if export else out[0]


def _mixer_kernel(*refs, is_prompt, n_cast):
    n_chunks = MIX_R // CHUNK
    (xa_ref, ga_ref, q_ref, k_ref, v_ref, g_ref, cos_ref, sin_ref,
     dmask_ref, qdec_ref, kdec_ref, cdec_ref,
     convw_ref, convb_ref, wri_ref, bri_ref, lam_ref, gng_ref, gnb_ref) = refs[:19]
    refs = refs[19:]
    if is_prompt:
        conv0_ref = h0_ref = s0_ref = None
    else:
        conv0_ref, h0_ref, s0_ref = refs[:3]
        refs = refs[3:]
    cast_src, refs = refs[:n_cast], refs[n_cast:]
    ya_ref, yb_ref, convo_ref, ho_ref, so_ref = refs[:5]
    cast_dst, refs = refs[5:5 + n_cast], refs[5 + n_cast:]
    (xpad_ref, xc_ref, a_ref, u_ref, hs_ref,
     qb_ref, kb_ref, qd_ref, kd_ref, vb_ref, sc_ref, oh_ref) = refs

    for src, dst in zip(cast_src, cast_dst):
        dst[...] = src[...].astype(BF16)

    if is_prompt:
        @pl.when(pl.program_id(1) == 0)
        def _():
            convo_ref[...] = jnp.zeros_like(convo_ref)
            ho_ref[...] = jnp.zeros_like(ho_ref)
            so_ref[...] = jnp.zeros_like(so_ref)

    n_seg = 1 if is_prompt else n_chunks
    seg_len = MIX_R // n_seg
    conv_b = convb_ref[...]
    w0, w1, w2, w3 = (convw_ref[j:j + 1, :] for j in range(CONV_W))
    lo = CONV_PAD - (CONV_W - 1)
    for sg in range(n_seg):
        rows = slice(sg * seg_len, (sg + 1) * seg_len)
        xpad_ref[sg, lo:CONV_PAD, :] = convo_ref[0] if is_prompt else conv0_ref[sg]
        xpad_ref[sg, CONV_PAD:CONV_PAD + seg_len, :] = xa_ref[rows, :]
        acc = conv_b + w0 * xpad_ref[sg, lo:lo + seg_len, :]
        acc = acc + w1 * xpad_ref[sg, lo + 1:lo + 1 + seg_len, :]
        acc = acc + w2 * xpad_ref[sg, lo + 2:lo + 2 + seg_len, :]
        acc = acc + w3 * xpad_ref[sg, lo + 3:lo + 3 + seg_len, :]
        xc_ref[rows, :] = acc
        convo_ref[sg] = xpad_ref[sg, lo + seg_len:CONV_PAD + seg_len, :]

    xc = xc_ref[...]
    ri = jnp.dot(xc.astype(BF16), wri_ref[...], preferred_element_type=F32) + bri_ref[...]
    r = jax.nn.sigmoid(ri[:, :D_RNN])
    i = jax.nn.sigmoid(ri[:, D_RNN:])
    log_a = -LRU_C * r * jax.nn.softplus(-lam_ref[...])
    a = jnp.exp(log_a)
    a_ref[...] = a
    w = jnp.tanh(-log_a) * (1.0 + a * a)
    u_ref[...] = jnp.where(w > 0.0, w * lax.rsqrt(w), 0.0) * (i * xc)

    for sg in range(n_seg):
        h_init = ho_ref[0] if is_prompt else h0_ref[sg]

        def step(t, h, base=sg * seg_len):
            row = pl.ds(base + t, 1)
            h = a_ref[row, :] * h + u_ref[row, :]
            hs_ref[row, :] = h
            return h

        ho_ref[sg] = lax.fori_loop(0, seg_len, step, h_init, unroll=8)

    ya_ref[...] = (hs_ref[...] * jax.nn.gelu(ga_ref[...])).astype(BF16)

    k_scale = RET_DK ** -0.5

    blk = MIX_R if is_prompt else CHUNK
    head_cols = [slice(hd * RET_DK, (hd + 1) * RET_DK) for hd in range(RET_HEADS)]
    blk_rows = [slice(c * blk, (c + 1) * blk) for c in range(MIX_R // blk)]
    cos, sin = cos_ref[...], sin_ref[...]

    for hd, cols in enumerate(head_cols):
        qh, kh = q_ref[:, cols], k_ref[:, cols]
        qr = qh * cos + pltpu.roll(qh, RET_DK // 2, 1) * sin
        kr = (kh * cos + pltpu.roll(kh, RET_DK // 2, 1) * sin) * k_scale
        qb_ref[:, cols] = qr.astype(BF16)
        kb_ref[:, cols] = kr.astype(BF16)
        qd_ref[:, cols] = (qr * qdec_ref[hd]).astype(BF16)
        kd_ref[:, cols] = (kr * kdec_ref[hd]).astype(BF16)
        vb_ref[:, cols] = v_ref[:, cols].astype(BF16)

    for rows in blk_rows:
        for hd, cols in enumerate(head_cols):
            s = lax.dot_general(qb_ref[rows, cols], kb_ref[rows, cols], (((1,), (1,)), ((), ())),
                                preferred_element_type=F32) * dmask_ref[hd]
            sc_ref[hd, rows, :] = s.astype(BF16)

    for c, rows in enumerate(blk_rows):
        for hd, cols in enumerate(head_cols):
            state = so_ref[0, hd] if is_prompt else s0_ref[c, hd]
            vb = vb_ref[rows, cols]
            o = jnp.dot(sc_ref[hd, rows, :], vb, preferred_element_type=F32)
            o = o + jnp.dot(qd_ref[rows, cols], state.astype(BF16), preferred_element_type=F32)
            oh_ref[rows, cols] = o
            new_state = cdec_ref[hd] * state + lax.dot_general(
                kd_ref[rows, cols], vb, (((0,), (0,)), ((), ())), preferred_element_type=F32)
            if is_prompt:
                so_ref[0, hd] = new_state
            else:
                so_ref[c, hd] = new_state

    for cols in head_cols:
        o = oh_ref[:, cols]
        mu = jnp.mean(o, axis=-1, keepdims=True)
        oc = o - mu
        var = jnp.mean(oc * oc, axis=-1, keepdims=True)
        on = oc * lax.rsqrt(var + GN_EPS) * gng_ref[:, cols] + gnb_ref[:, cols]
        gh = g_ref[:, cols]
        yb_ref[:, cols] = (gh * jax.nn.sigmoid(gh) * on).astype(BF16)


def _mixer(z, tables, params, states, *, is_prompt, cast_srcs=()):
    cos_t, sin_t, dmask, qdec, kdec, cdec = tables
    n_chunks = MIX_R // CHUNK
    if is_prompt:
        tiles_per_seq = SEQ // MIX_R
        grid = (BATCH, tiles_per_seq)
        n_seq = BATCH
        n_state = 1
        row_blk = lambda s, t: s * tiles_per_seq + t
        zspec = lambda col: pl.BlockSpec((MIX_R, D_RNN), lambda s, t, col=col: (row_blk(s, t), col))
        tab_spec = pl.BlockSpec((MIX_R, RET_DK), lambda s, t: (t, 0))
        state_idx = lambda s, t: s
        out_row = lambda s, t: (row_blk(s, t), 0)
        m_rows = M_PROMPT
        sem = ("parallel", "arbitrary")
    else:
        grid = (M_SAMPLE // MIX_R,)
        n_seq = DEC_BATCH
        n_state = n_chunks
        zspec = lambda col: pl.BlockSpec((MIX_R, D_RNN), lambda i, col=col: (i, col))
        tab_spec = pl.BlockSpec((MIX_R, RET_DK), lambda i: (0, 0))
        state_idx = lambda i: i
        out_row = lambda i: (i, 0)
        m_rows = M_SAMPLE
        sem = ("parallel",)

    def const_spec(shape):
        nd = len(shape)
        return pl.BlockSpec(shape, lambda *_: (0,) * nd)

    def state_spec(shape):
        nd = len(shape)
        return pl.BlockSpec((n_state,) + shape, lambda *g: (state_idx(*g),) + (0,) * nd)

    in_specs = [zspec(c) for c in range(6)] + [tab_spec, tab_spec] + [
        const_spec(dmask.shape), const_spec(qdec.shape), const_spec(kdec.shape), const_spec(cdec.shape)
    ] + [const_spec(p.shape) for p in params]
    args = [z] * 6 + [cos_t, sin_t, dmask, qdec, kdec, cdec] + list(params)
    if not is_prompt:
        in_specs += [state_spec((CONV_W - 1, D_RNN)), state_spec((1, D_RNN)),
                     state_spec((RET_HEADS, RET_DK, RET_DV))]
        args += list(states)

    n_steps = 1
    for g in grid:
        n_steps *= g
    step_idx = (lambda s, t: (s * grid[1] + t, 0)) if is_prompt else (lambda i: (i, 0))
    cast_specs = [pl.BlockSpec((w.shape[0] // n_steps, w.shape[1]), step_idx) for w in cast_srcs]
    in_specs += cast_specs
    args += list(cast_srcs)

    n_seg = 1 if is_prompt else n_chunks
    seg_len = MIX_R // n_seg
    out_shape = [
        jax.ShapeDtypeStruct((m_rows, D_RNN), BF16),
        jax.ShapeDtypeStruct((m_rows, D_RET_V), BF16),
        jax.ShapeDtypeStruct((n_seq, CONV_W - 1, D_RNN), F32),
        jax.ShapeDtypeStruct((n_seq, 1, D_RNN), F32),
        jax.ShapeDtypeStruct((n_seq, RET_HEADS, RET_DK, RET_DV), F32),
    ]
    out_specs = [
        pl.BlockSpec((MIX_R, D_RNN), out_row),
        pl.BlockSpec((MIX_R, D_RET_V), out_row),
        state_spec((CONV_W - 1, D_RNN)),
        state_spec((1, D_RNN)),
        state_spec((RET_HEADS, RET_DK, RET_DV)),
    ] + cast_specs
    out_shape += [jax.ShapeDtypeStruct(w.shape, BF16) for w in cast_srcs]
    scratch = [
        pltpu.VMEM((n_seg, CONV_PAD + seg_len, D_RNN), F32),
        pltpu.VMEM((MIX_R, D_RNN), F32),
        pltpu.VMEM((MIX_R, D_RNN), F32),
        pltpu.VMEM((MIX_R, D_RNN), F32),
        pltpu.VMEM((MIX_R, D_RNN), F32),
    ] + [pltpu.VMEM((MIX_R, D_RET_K), BF16)] * 5 + [
        pltpu.VMEM((RET_HEADS, MIX_R, MIX_R if is_prompt else CHUNK), BF16),
        pltpu.VMEM((MIX_R, D_RET_V), F32),
    ]
    return pl.pallas_call(
        functools.partial(_mixer_kernel, is_prompt=is_prompt, n_cast=len(cast_srcs)),
        out_shape=out_shape,
        grid=grid,
        in_specs=in_specs,
        out_specs=out_specs,
        scratch_shapes=scratch,
        compiler_params=pltpu.CompilerParams(dimension_semantics=sem, vmem_limit_bytes=VMEM_LIMIT),
        name="mixer_prompt" if is_prompt else "mixer_sample",
    )(*args)


def _merge_ln_kernel(ya_ref, yb_ref, ga_ref, gb_ref, x_ref, wa_ref, wb_ref, wo_ref, g_ref, b_ref, o_ref):
    for r in range(MERGE_TM // MERGE_SUB):
        rows = slice(r * MERGE_SUB, (r + 1) * MERGE_SUB)
        pa = jnp.dot(ya_ref[rows, :], wa_ref[...], preferred_element_type=F32)
        pb = jnp.dot(yb_ref[rows, :], wb_ref[...], preferred_element_type=F32)
        merged = jax.nn.sigmoid(ga_ref[rows, :]) * pa + jax.nn.sigmoid(gb_ref[rows, :]) * pb
        mo = jnp.dot(merged.astype(BF16), wo_ref[...], preferred_element_type=F32)
        o_ref[rows, :] = _layer_norm_rows(DN_ALPHA * x_ref[rows, :] + mo, g_ref[...], b_ref[...])


def _merge_ln(ya, yb, z, x1, wa, wb, wo, ln_g, ln_b):
    m = x1.shape[0]
    gate_a_blk = (2 * D_RNN + 2 * D_RET_K + 2 * D_RET_V) // D_MODEL
    row = lambda i: (i, 0)
    const = lambda i: (0, 0)
    return pl.pallas_call(
        _merge_ln_kernel,
        out_shape=jax.ShapeDtypeStruct((m, D_MODEL), F32),
        grid=(m // MERGE_TM,),
        in_specs=[
            pl.BlockSpec((MERGE_TM, D_RNN), row),
            pl.BlockSpec((MERGE_TM, D_RET_V), row),
            pl.BlockSpec((MERGE_TM, D_MODEL), lambda i: (i, gate_a_blk)),
            pl.BlockSpec((MERGE_TM, D_MODEL), lambda i: (i, gate_a_blk + 1)),
            pl.BlockSpec((MERGE_TM, D_MODEL), row),
            pl.BlockSpec((D_RNN, D_MODEL), const, pipeline_mode=pl.Buffered(1)),
            pl.BlockSpec((D_RET_V, D_MODEL), const, pipeline_mode=pl.Buffered(1)),
            pl.BlockSpec((D_MODEL, D_MODEL), const, pipeline_mode=pl.Buffered(1)),
            pl.BlockSpec((1, D_MODEL), const),
            pl.BlockSpec((1, D_MODEL), const),
        ],
        out_specs=pl.BlockSpec((MERGE_TM, D_MODEL), row),
        compiler_params=pltpu.CompilerParams(
            dimension_semantics=("parallel",), vmem_limit_bytes=VMEM_LIMIT),
        name="merge_ln",
    )(ya, yb, z, z, x1, wa, wb, wo, ln_g, ln_b)


def _rope_tables(pos):
    d = RET_DK
    inv_freq = ROPE_BASE ** (-jnp.arange(0, d, 2, dtype=F32) / d)
    ang = pos.astype(F32)[:, None] * inv_freq[None, :]
    cos, sin = jnp.cos(ang), jnp.sin(ang)
    return jnp.concatenate([cos, cos], axis=-1), jnp.concatenate([-sin, sin], axis=-1)


def _decay_tables(blk):
    log_g = jnp.log1p(-jnp.exp2(-5.0 - jnp.arange(RET_HEADS, dtype=F32)))
    idx = jnp.arange(blk, dtype=F32)
    diff = idx[:, None] - idx[None, :]
    dmask = jnp.where(diff >= 0, jnp.exp(log_g[:, None, None] * jnp.maximum(diff, 0.0)), 0.0)
    q_dec = jnp.exp(log_g[:, None] * (idx[None, :] + 1.0))
    k_dec = jnp.exp(log_g[:, None] * (blk - 1.0 - idx[None, :]))
    chunk_dec = jnp.exp(log_g * blk)
    lanes = (RET_HEADS, blk, RET_DK)
    return (dmask, jnp.broadcast_to(q_dec[:, :, None], lanes), jnp.broadcast_to(k_dec[:, :, None], lanes),
            jnp.broadcast_to(chunk_dec[:, None, None], (RET_HEADS, 1, RET_DV)))


def _block_diag(w):
    eye = jnp.eye(LRU_BLOCKS, dtype=w.dtype)
    return jnp.einsum("nkj,nm->nkmj", w, eye).reshape(D_RNN, D_RNN)


def kernel(x_prompt, x_sample, state_conv, state_lru, state_ret, ffn1_w_gate, ffn1_w_up, ffn1_w_down, ln1_g, ln1_b, w_in, conv_w, conv_b, rg_w, rg_b, ig_w, ig_b, lru_lambda, ret_gn_g, ret_gn_b, w_a_proj, w_b_proj, w_o, ln2_g, ln2_b, ffn2_w_gate, ffn2_w_up, ffn2_w_down, ln3_g, ln3_b):
    assert DEPTH == 1
    w_ri = jnp.concatenate([_block_diag(rg_w[0]), _block_diag(ig_w[0])], axis=1).astype(BF16)
    b_ri = jnp.concatenate([rg_b, ig_b], axis=1)
    params = (conv_w[0], conv_b, w_ri, b_ri, lru_lambda, ret_gn_g, ret_gn_b)
    rope = _rope_tables(jnp.arange(PAST_LEN + DEC_SEQ, dtype=jnp.int32))
    x_p = x_prompt.reshape(M_PROMPT, D_MODEL)
    x_s = x_sample.reshape(M_SAMPLE, D_MODEL)
    states = (state_conv[0], state_lru[0][:, None, :], state_ret[0])

    x1_s, *ffn1_w = _ffn_ln(x_s, ffn1_w_gate[0], ffn1_w_up[0], ffn1_w_down[0], ln1_g, ln1_b, export=True)
    x1_p = _ffn_ln(x_p, *ffn1_w, ln1_g, ln1_b)
    z_s, w_in_b = _in_proj(x1_s, w_in[0], export=True)
    z_p = _in_proj(x1_p, w_in_b)

    later_w = (ffn2_w_gate[0], ffn2_w_up[0], ffn2_w_down[0], w_a_proj[0], w_b_proj[0], w_o[0])
    ya_p, yb_p, conv_p, lru_p, ret_p, *later_b = _mixer(
        z_p, rope + _decay_tables(MIX_R), params, None, is_prompt=True, cast_srcs=later_w)
    reps = MIX_R // CHUNK
    dmask_s, qdec_s, kdec_s, cdec_s = _decay_tables(CHUNK)
    tables_s = tuple(jnp.tile(t[PAST_LEN:], (reps, 1)) for t in rope) + (
        dmask_s, jnp.tile(qdec_s, (1, reps, 1)), jnp.tile(kdec_s, (1, reps, 1)), cdec_s)
    ya_s, yb_s, conv_s, lru_s, ret_s = _mixer(z_s, tables_s, params, states, is_prompt=False)
    ffn2_w, merge_w = later_b[:3], later_b[3:]

    def tail(ya, yb, z, x1):
        x2 = _merge_ln(ya, yb, z, x1, *merge_w, ln2_g, ln2_b)
        return _ffn_ln(x2, *ffn2_w, ln3_g, ln3_b)

    y_p = tail(ya_p, yb_p, z_p, x1_p)
    y_s = tail(ya_s, yb_s, z_s, x1_s)

    return (y_p.reshape(BATCH, SEQ, D_MODEL), y_s.reshape(DEC_BATCH, DEC_SEQ, D_MODEL),
            conv_p[None], lru_p.reshape(1, BATCH, D_RNN), ret_p[None],
            conv_s[None], lru_s.reshape(1, DEC_BATCH, D_RNN), ret_s[None])
```

```python
import functools

import jax
import jax.numpy as jnp
from jax import lax
from jax.experimental import pallas as pl
from jax.experimental.pallas import tpu as pltpu

F32 = jnp.float32
BF16 = jnp.bfloat16

D_MODEL = 2048
BATCH = 4
SEQ = 2048
DEPTH = 1
DEC_BATCH = 32
DEC_SEQ = 64
PAST_LEN = 2048
CHUNK = 64
D_RNN = 1024
LRU_BLOCKS = 16
LRU_BLOCK = D_RNN // LRU_BLOCKS
CONV_W = 4
LRU_C = 8.0
RET_HEADS = 8
RET_DK = 128
RET_DV = 128
D_RET_K = RET_HEADS * RET_DK
D_RET_V = RET_HEADS * RET_DV
D_FF = 5632
DN_ALPHA = (2.0 * DEPTH) ** 0.25
LN_EPS = 1e-5
GN_EPS = 1e-5
ROPE_BASE = 10000.0
D_IN = 2 * D_RNN + 2 * D_RET_K + 2 * D_RET_V + 2 * D_MODEL

M_PROMPT = BATCH * SEQ
M_SAMPLE = DEC_BATCH * DEC_SEQ

V7X_VMEM_BYTES = 64 * 1024 * 1024
VMEM_LIMIT = V7X_VMEM_BYTES - 8 * 1024 * 1024
VMEM_LIMIT_BIG = V7X_VMEM_BYTES - 3 * 1024 * 1024

FFN_TM = 1024
FFN_SUB = 512
FFN_SUB_LAST = 256
FFN_TF = 512
FFN_TF_EXPORT = 256
LN_ROWS = 128
PROJ_TM = 1024
PROJ_TN = 1024
PROJ_TN_EXPORT = 512
MIX_R = 256
MERGE_TM = 256
MERGE_SUB = 128
CONV_PAD = 8


def _layer_norm_rows(y, g, b):
    mu = jnp.mean(y, axis=-1, keepdims=True)
    yc = y - mu
    var = jnp.mean(yc * yc, axis=-1, keepdims=True)
    return yc * lax.rsqrt(var + LN_EPS) * g + b


def _ffn_ln_kernel(x_ref, wg_ref, wu_ref, wd_ref, g_ref, b_ref, o_ref, *rest, export):
    i, f = pl.program_id(0), pl.program_id(1)
    last = pl.num_programs(1) - 1
    xb_ref = rest[-1]

    def sub_tiles(size):
        return [slice(r * size, (r + 1) * size) for r in range(FFN_TM // size)]

    def weights():
        if not export:
            return wg_ref[...], wu_ref[...], wd_ref[...]
        ws = tuple(w[...].astype(BF16) for w in (wg_ref, wu_ref, wd_ref))

        @pl.when(i == 0)
        def _():
            for dst, w in zip(rest[:3], ws):
                dst[...] = w
        return ws

    def seed(rows):
        x = x_ref[rows, :]
        o_ref[rows, :] = (2.0 * DN_ALPHA) * x
        xb_ref[rows, :] = x.astype(BF16)

    def accumulate(rows, ws):
        wg, wu, wd = ws
        xb = xb_ref[rows, :]
        gate = jnp.dot(xb, wg, preferred_element_type=F32)
        up = jnp.dot(xb, wu, preferred_element_type=F32)
        h = (gate * jax.nn.sigmoid(gate) * up).astype(BF16)
        o_ref[rows, :] += jnp.dot(h, wd, preferred_element_type=F32)

    def finish(rows):
        for c in range(rows.start, rows.stop, LN_ROWS):
            chunk = slice(c, c + LN_ROWS)
            acc = o_ref[chunk, :]
            mu = jnp.mean(acc, axis=-1, keepdims=True)
            ac = acc - mu
            var = jnp.mean(ac * ac, axis=-1, keepdims=True)
            scale = 0.5 * lax.rsqrt(0.25 * var + LN_EPS)
            o_ref[chunk, :] = ac * scale * g_ref[...] + b_ref[...]

    @pl.when(f == 0)
    def _():
        ws = weights()
        for rows in sub_tiles(FFN_SUB):
            seed(rows)
            accumulate(rows, ws)

    @pl.when(jnp.logical_and(f > 0, f < last))
    def _():
        ws = weights()
        for rows in sub_tiles(FFN_SUB):
            accumulate(rows, ws)

    @pl.when(f == last)
    def _():
        ws = weights()
        for rows in sub_tiles(FFN_SUB_LAST):
            accumulate(rows, ws)
            finish(rows)


def _ffn_ln(x, wg, wu, wd, ln_g, ln_b, *, export=False):
    m = x.shape[0]
    tf = FFN_TF_EXPORT if export else FFN_TF
    nf = D_FF // tf
    grid = (m // FFN_TM, nf)
    out_shape = [jax.ShapeDtypeStruct((m, D_MODEL), F32)]
    out_specs = [pl.BlockSpec((FFN_TM, D_MODEL), lambda i, f: (i, 0))]
    x_mode = {}
    if export:
        pin = lambda i, f: jnp.where(i == 0, f, nf - 1)
        out_shape += [jax.ShapeDtypeStruct(w.shape, BF16) for w in (wg, wu, wd)]
        out_specs += [
            pl.BlockSpec((D_MODEL, tf), lambda i, f: (0, pin(i, f))),
            pl.BlockSpec((D_MODEL, tf), lambda i, f: (0, pin(i, f))),
            pl.BlockSpec((tf, D_MODEL), lambda i, f: (pin(i, f), 0)),
        ]
        x_mode = dict(pipeline_mode=pl.Buffered(1))
    out = pl.pallas_call(
        functools.partial(_ffn_ln_kernel, export=export),
        out_shape=out_shape,
        grid=grid,
        in_specs=[
            pl.BlockSpec((FFN_TM, D_MODEL), lambda i, f: (i, 0), **x_mode),
            pl.BlockSpec((D_MODEL, tf), lambda i, f: (0, f)),
            pl.BlockSpec((D_MODEL, tf), lambda i, f: (0, f)),
            pl.BlockSpec((tf, D_MODEL), lambda i, f: (f, 0)),
            pl.BlockSpec((1, D_MODEL), lambda i, f: (0, 0)),
            pl.BlockSpec((1, D_MODEL), lambda i, f: (0, 0)),
        ],
        out_specs=out_specs,
        scratch_shapes=[pltpu.VMEM((FFN_TM, D_MODEL), BF16)],
        compiler_params=pltpu.CompilerParams(
            dimension_semantics=("arbitrary" if export else "parallel", "arbitrary"),
            vmem_limit_bytes=VMEM_LIMIT_BIG),
        name="ffn_ln_export" if export else "ffn_ln",
    )(x, wg, wu, wd, ln_g, ln_b)
    return out if export else out[0]


def _in_proj_kernel(x_ref, w_ref, o_ref, *rest, export):
    xb_ref = rest[-1]

    @pl.when(pl.program_id(1) == 0)
    def _():
        xb_ref[...] = x_ref[...].astype(BF16)

    if export:
        w = w_ref[...].astype(BF16)
        rest[0][...] = w
    else:
        w = w_ref[...]
    o_ref[...] = jnp.dot(xb_ref[...], w, preferred_element_type=F32)


def _in_proj(x1, w_in, *, export=False):
    m = x1.shape[0]
    tm, tn = (m, PROJ_TN_EXPORT) if export else (PROJ_TM, PROJ_TN)
    assert not export or m == tm
    out_shape = [jax.ShapeDtypeStruct((m, D_IN), F32)]
    out_specs = [pl.BlockSpec((tm, tn), lambda i, j: (i, j))]
    x_mode = {}
    if export:
        out_shape.append(jax.ShapeDtypeStruct(w_in.shape, BF16))
        out_specs.append(pl.BlockSpec((D_MODEL, tn), lambda i, j: (0, j)))
        x_mode = dict(pipeline_mode=pl.Buffered(1))
    out = pl.pallas_call(
        functools.partial(_in_proj_kernel, export=export),
        out_shape=out_shape,
        grid=(m // tm, D_IN // tn),
        in_specs=[
            pl.BlockSpec((tm, D_MODEL), lambda i, j: (i, 0), **x_mode),
            pl.BlockSpec((D_MODEL, tn), lambda i, j: (0, j)),
        ],
        out_specs=out_specs,
        scratch_shapes=[pltpu.VMEM((tm, D_MODEL), BF16)],
        compiler_params=pltpu.CompilerParams(
            dimension_semantics=("parallel", "arbitrary"), vmem_limit_bytes=VMEM_LIMIT),
        name="in_proj_export" if export else "in_proj",
    )(x1, w_in)
    return out if export else out[0]


def _mixer_kernel(*refs, is_prompt, n_cast):
    n_chunks = MIX_R // CHUNK
    (xa_ref, ga_ref, q_ref, k_ref, v_ref, g_ref, cos_ref, sin_ref,
     dmask_ref, qdec_ref, kdec_ref, cdec_ref,
     convw_ref, convb_ref, wri_ref, bri_ref, lam_ref, gng_ref, gnb_ref) = refs[:19]
    refs = refs[19:]
    if is_prompt:
        conv0_ref = h0_ref = s0_ref = None
    else:
        conv0_ref, h0_ref, s0_ref = refs[:3]
        refs = refs[3:]
    cast_src, refs = refs[:n_cast], refs[n_cast:]
    ya_ref, yb_ref, convo_ref, ho_ref, so_ref = refs[:5]
    cast_dst, refs = refs[5:5 + n_cast], refs[5 + n_cast:]
    (xpad_ref, xc_ref, a_ref, u_ref, hs_ref,
     qb_ref, kb_ref, qd_ref, kd_ref, vb_ref, sc_ref, oh_ref) = refs

    for src, dst in zip(cast_src, cast_dst):
        dst[...] = src[...].astype(BF16)

    if is_prompt:
        @pl.when(pl.program_id(1) == 0)
        def _():
            convo_ref[...] = jnp.zeros_like(convo_ref)
            ho_ref[...] = jnp.zeros_like(ho_ref)
            so_ref[...] = jnp.zeros_like(so_ref)

    n_seg = 1 if is_prompt else n_chunks
    seg_len = MIX_R // n_seg
    conv_b = convb_ref[...]
    w0, w1, w2, w3 = (convw_ref[j:j + 1, :] for j in range(CONV_W))
    lo = CONV_PAD - (CONV_W - 1)
    for sg in range(n_seg):
        rows = slice(sg * seg_len, (sg + 1) * seg_len)
        xpad_ref[sg, lo:CONV_PAD, :] = convo_ref[0] if is_prompt else conv0_ref[sg]
        xpad_ref[sg, CONV_PAD:CONV_PAD + seg_len, :] = xa_ref[rows, :]
        acc = conv_b + w0 * xpad_ref[sg, lo:lo + seg_len, :]
        acc = acc + w1 * xpad_ref[sg, lo + 1:lo + 1 + seg_len, :]
        acc = acc + w2 * xpad_ref[sg, lo + 2:lo + 2 + seg_len, :]
        acc = acc + w3 * xpad_ref[sg, lo + 3:lo + 3 + seg_len, :]
        xc_ref[rows, :] = acc
        convo_ref[sg] = xpad_ref[sg, lo + seg_len:CONV_PAD + seg_len, :]

    xc = xc_ref[...]
    ri = jnp.dot(xc.astype(BF16), wri_ref[...], preferred_element_type=F32) + bri_ref[...]
    r = jax.nn.sigmoid(ri[:, :D_RNN])
    i = jax.nn.sigmoid(ri[:, D_RNN:])
    log_a = -LRU_C * r * jax.nn.softplus(-lam_ref[...])
    a = jnp.exp(log_a)
    a_ref[...] = a
    w = jnp.tanh(-log_a) * (1.0 + a * a)
    u_ref[...] = jnp.where(w > 0.0, w * lax.rsqrt(w), 0.0) * (i * xc)

    for sg in range(n_seg):
        h_init = ho_ref[0] if is_prompt else h0_ref[sg]

        def step(t, h, base=sg * seg_len):
            row = pl.ds(base + t, 1)
            h = a_ref[row, :] * h + u_ref[row, :]
            hs_ref[row, :] = h
            return h

        ho_ref[sg] = lax.fori_loop(0, seg_len, step, h_init, unroll=8)

    ya_ref[...] = (hs_ref[...] * jax.nn.gelu(ga_ref[...])).astype(BF16)

    k_scale = RET_DK ** -0.5

    blk = MIX_R if is_prompt else CHUNK
    head_cols = [slice(hd * RET_DK, (hd + 1) * RET_DK) for hd in range(RET_HEADS)]
    blk_rows = [slice(c * blk, (c + 1) * blk) for c in range(MIX_R // blk)]
    cos, sin = cos_ref[...], sin_ref[...]

    for hd, cols in enumerate(head_cols):
        qh, kh = q_ref[:, cols], k_ref[:, cols]
        qr = qh * cos + pltpu.roll(qh, RET_DK // 2, 1) * sin
        kr = (kh * cos + pltpu.roll(kh, RET_DK // 2, 1) * sin) * k_scale
        qb_ref[:, cols] = qr.astype(BF16)
        kb_ref[:, cols] = kr.astype(BF16)
        qd_ref[:, cols] = (qr * qdec_ref[hd]).astype(BF16)
        kd_ref[:, cols] = (kr * kdec_ref[hd]).astype(BF16)
        vb_ref[:, cols] = v_ref[:, cols].astype(BF16)

    for rows in blk_rows:
        for hd, cols in enumerate(head_cols):
            s = lax.dot_general(qb_ref[rows, cols], kb_ref[rows, cols], (((1,), (1,)), ((), ())),
                                preferred_element_type=F32) * dmask_ref[hd]
            sc_ref[hd, rows, :] = s.astype(BF16)

    for c, rows in enumerate(blk_rows):
        for hd, cols in enumerate(head_cols):
            state = so_ref[0, hd] if is_prompt else s0_ref[c, hd]
            vb = vb_ref[rows, cols]
            o = jnp.dot(sc_ref[hd, rows, :], vb, preferred_element_type=F32)
            o = o + jnp.dot(qd_ref[rows, cols], state.astype(BF16), preferred_element_type=F32)
            oh_ref[rows, cols] = o
            new_state = cdec_ref[hd] * state + lax.dot_general(
                kd_ref[rows, cols], vb, (((0,), (0,)), ((), ())), preferred_element_type=F32)
            if is_prompt:
                so_ref[0, hd] = new_state
            else:
                so_ref[c, hd] = new_state

    for cols in head_cols:
        o = oh_ref[:, cols]
        mu = jnp.mean(o, axis=-1, keepdims=True)
        oc = o - mu
        var = jnp.mean(oc * oc, axis=-1, keepdims=True)
        on = oc * lax.rsqrt(var + GN_EPS) * gng_ref[:, cols] + gnb_ref[:, cols]
        gh = g_ref[:, cols]
        yb_ref[:, cols] = (gh * jax.nn.sigmoid(gh) * on).astype(BF16)


def _mixer(z, tables, params, states, *, is_prompt, cast_srcs=()):
    cos_t, sin_t, dmask, qdec, kdec, cdec = tables
    n_chunks = MIX_R // CHUNK
    if is_prompt:
        tiles_per_seq = SEQ // MIX_R
        grid = (BATCH, tiles_per_seq)
        n_seq = BATCH
        n_state = 1
        row_blk = lambda s, t: s * tiles_per_seq + t
        zspec = lambda col: pl.BlockSpec((MIX_R, D_RNN), lambda s, t, col=col: (row_blk(s, t), col))
        tab_spec = pl.BlockSpec((MIX_R, RET_DK), lambda s, t: (t, 0))
        state_idx = lambda s, t: s
        out_row = lambda s, t: (row_blk(s, t), 0)
        m_rows = M_PROMPT
        sem = ("parallel", "arbitrary")
    else:
        grid = (M_SAMPLE // MIX_R,)
        n_seq = DEC_BATCH
        n_state = n_chunks
        zspec = lambda col: pl.BlockSpec((MIX_R, D_RNN), lambda i, col=col: (i, col))
        tab_spec = pl.BlockSpec((MIX_R, RET_DK), lambda i: (0, 0))
        state_idx = lambda i: i
        out_row = lambda i: (i, 0)
        m_rows = M_SAMPLE
        sem = ("parallel",)

    def const_spec(shape):
        nd = len(shape)
        return pl.BlockSpec(shape, lambda *_: (0,) * nd)

    def state_spec(shape):
        nd = len(shape)
        return pl.BlockSpec((n_state,) + shape, lambda *g: (state_idx(*g),) + (0,) * nd)

    in_specs = [zspec(c) for c in range(6)] + [tab_spec, tab_spec] + [
        const_spec(dmask.shape), const_spec(qdec.shape), const_spec(kdec.shape), const_spec(cdec.shape)
    ] + [const_spec(p.shape) for p in params]
    args = [z] * 6 + [cos_t, sin_t, dmask, qdec, kdec, cdec] + list(params)
    if not is_prompt:
        in_specs += [state_spec((CONV_W - 1, D_RNN)), state_spec((1, D_RNN)),
                     state_spec((RET_HEADS, RET_DK, RET_DV))]
        args += list(states)

    n_steps = 1
    for g in grid:
        n_steps *= g
    step_idx = (lambda s, t: (s * grid[1] + t, 0)) if is_prompt else (lambda i: (i, 0))
    cast_specs = [pl.BlockSpec((w.shape[0] // n_steps, w.shape[1]), step_idx) for w in cast_srcs]
    in_specs += cast_specs
    args += list(cast_srcs)

    n_seg = 1 if is_prompt else n_chunks
    seg_len = MIX_R // n_seg
    out_shape = [
        jax.ShapeDtypeStruct((m_rows, D_RNN), BF16),
        jax.ShapeDtypeStruct((m_rows, D_RET_V), BF16),
        jax.ShapeDtypeStruct((n_seq, CONV_W - 1, D_RNN), F32),
        jax.ShapeDtypeStruct((n_seq, 1, D_RNN), F32),
        jax.ShapeDtypeStruct((n_seq, RET_HEADS, RET_DK, RET_DV), F32),
    ]
    out_specs = [
        pl.BlockSpec((MIX_R, D_RNN), out_row),
        pl.BlockSpec((MIX_R, D_RET_V), out_row),
        state_spec((CONV_W - 1, D_RNN)),
        state_spec((1, D_RNN)),
        state_spec((RET_HEADS, RET_DK, RET_DV)),
    ] + cast_specs
    out_shape += [jax.ShapeDtypeStruct(w.shape, BF16) for w in cast_srcs]
    scratch = [
        pltpu.VMEM((n_seg, CONV_PAD + seg_len, D_RNN), F32),
        pltpu.VMEM((MIX_R, D_RNN), F32),
        pltpu.VMEM((MIX_R, D_RNN), F32),
        pltpu.VMEM((MIX_R, D_RNN), F32),
        pltpu.VMEM((MIX_R, D_RNN), F32),
    ] + [pltpu.VMEM((MIX_R, D_RET_K), BF16)] * 5 + [
        pltpu.VMEM((RET_HEADS, MIX_R, MIX_R if is_prompt else CHUNK), BF16),
        pltpu.VMEM((MIX_R, D_RET_V), F32),
    ]
    return pl.pallas_call(
        functools.partial(_mixer_kernel, is_prompt=is_prompt, n_cast=len(cast_srcs)),
        out_shape=out_shape,
        grid=grid,
        in_specs=in_specs,
        out_specs=out_specs,
        scratch_shapes=scratch,
        compiler_params=pltpu.CompilerParams(dimension_semantics=sem, vmem_limit_bytes=VMEM_LIMIT),
        name="mixer_prompt" if is_prompt else "mixer_sample",
    )(*args)


def _merge_ln_kernel(ya_ref, yb_ref, ga_ref, gb_ref, x_ref, wa_ref, wb_ref, wo_ref, g_ref, b_ref, o_ref):
    for r in range(MERGE_TM // MERGE_SUB):
        rows = slice(r * MERGE_SUB, (r + 1) * MERGE_SUB)
        pa = jnp.dot(ya_ref[rows, :], wa_ref[...], preferred_element_type=F32)
        pb = jnp.dot(yb_ref[rows, :], wb_ref[...], preferred_element_type=F32)
        merged = jax.nn.sigmoid(ga_ref[rows, :]) * pa + jax.nn.sigmoid(gb_ref[rows, :]) * pb
        mo = jnp.dot(merged.astype(BF16), wo_ref[...], preferred_element_type=F32)
        o_ref[rows, :] = _layer_norm_rows(DN_ALPHA * x_ref[rows, :] + mo, g_ref[...], b_ref[...])


def _merge_ln(ya, yb, z, x1, wa, wb, wo, ln_g, ln_b):
    m = x1.shape[0]
    gate_a_blk = (2 * D_RNN + 2 * D_RET_K + 2 * D_RET_V) // D_MODEL
    row = lambda i: (i, 0)
    const = lambda i: (0, 0)
    return pl.pallas_call(
        _merge_ln_kernel,
        out_shape=jax.ShapeDtypeStruct((m, D_MODEL), F32),
        grid=(m // MERGE_TM,),
        in_specs=[
            pl.BlockSpec((MERGE_TM, D_RNN), row),
            pl.BlockSpec((MERGE_TM, D_RET_V), row),
            pl.BlockSpec((MERGE_TM, D_MODEL), lambda i: (i, gate_a_blk)),
            pl.BlockSpec((MERGE_TM, D_MODEL), lambda i: (i, gate_a_blk + 1)),
            pl.BlockSpec((MERGE_TM, D_MODEL), row),
            pl.BlockSpec((D_RNN, D_MODEL), const, pipeline_mode=pl.Buffered(1)),
            pl.BlockSpec((D_RET_V, D_MODEL), const, pipeline_mode=pl.Buffered(1)),
            pl.BlockSpec((D_MODEL, D_MODEL), const, pipeline_mode=pl.Buffered(1)),
            pl.BlockSpec((1, D_MODEL), const),
            pl.BlockSpec((1, D_MODEL), const),
        ],
        out_specs=pl.BlockSpec((MERGE_TM, D_MODEL), row),
        compiler_params=pltpu.CompilerParams(
            dimension_semantics=("parallel",), vmem_limit_bytes=VMEM_LIMIT),
        name="merge_ln",
    )(ya, yb, z, z, x1, wa, wb, wo, ln_g, ln_b)


def _rope_tables(pos):
    d = RET_DK
    inv_freq = ROPE_BASE ** (-jnp.arange(0, d, 2, dtype=F32) / d)
    ang = pos.astype(F32)[:, None] * inv_freq[None, :]
    cos, sin = jnp.cos(ang), jnp.sin(ang)
    return jnp.concatenate([cos, cos], axis=-1), jnp.concatenate([-sin, sin], axis=-1)


def _decay_tables(blk):
    log_g = jnp.log1p(-jnp.exp2(-5.0 - jnp.arange(RET_HEADS, dtype=F32)))
    idx = jnp.arange(blk, dtype=F32)
    diff = idx[:, None] - idx[None, :]
    dmask = jnp.where(diff >= 0, jnp.exp(log_g[:, None, None] * jnp.maximum(diff, 0.0)), 0.0)
    q_dec = jnp.exp(log_g[:, None] * (idx[None, :] + 1.0))
    k_dec = jnp.exp(log_g[:, None] * (blk - 1.0 - idx[None, :]))
    chunk_dec = jnp.exp(log_g * blk)
    lanes = (RET_HEADS, blk, RET_DK)
    return (dmask, jnp.broadcast_to(q_dec[:, :, None], lanes), jnp.broadcast_to(k_dec[:, :, None], lanes),
            jnp.broadcast_to(chunk_dec[:, None, None], (RET_HEADS, 1, RET_DV)))


def _block_diag(w):
    eye = jnp.eye(LRU_BLOCKS, dtype=w.dtype)
    return jnp.einsum("nkj,nm->nkmj", w, eye).reshape(D_RNN, D_RNN)


def kernel(x_prompt, x_sample, state_conv, state_lru, state_ret, ffn1_w_gate, ffn1_w_up, ffn1_w_down, ln1_g, ln1_b, w_in, conv_w, conv_b, rg_w, rg_b, ig_w, ig_b, lru_lambda, ret_gn_g, ret_gn_b, w_a_proj, w_b_proj, w_o, ln2_g, ln2_b, ffn2_w_gate, ffn2_w_up, ffn2_w_down, ln3_g, ln3_b):
    assert DEPTH == 1
    w_ri = jnp.concatenate([_block_diag(rg_w[0]), _block_diag(ig_w[0])], axis=1).astype(BF16)
    b_ri = jnp.concatenate([rg_b, ig_b], axis=1)
    params = (conv_w[0], conv_b, w_ri, b_ri, lru_lambda, ret_gn_g, ret_gn_b)
    rope = _rope_tables(jnp.arange(PAST_LEN + DEC_SEQ, dtype=jnp.int32))
    x_p = x_prompt.reshape(M_PROMPT, D_MODEL)
    x_s = x_sample.reshape(M_SAMPLE, D_MODEL)
    states = (state_conv[0], state_lru[0][:, None, :], state_ret[0])

    x1_s, *ffn1_w = _ffn_ln(x_s, ffn1_w_gate[0], ffn1_w_up[0], ffn1_w_down[0], ln1_g, ln1_b, export=True)
    x1_p = _ffn_ln(x_p, *ffn1_w, ln1_g, ln1_b)
    z_s, w_in_b = _in_proj(x1_s, w_in[0], export=True)
    z_p = _in_proj(x1_p, w_in_b)

    later_w = (ffn2_w_gate[0], ffn2_w_up[0], ffn2_w_down[0], w_a_proj[0], w_b_proj[0], w_o[0])
    ya_p, yb_p, conv_p, lru_p, ret_p, *later_b = _mixer(
        z_p, rope + _decay_tables(MIX_R), params, None, is_prompt=True, cast_srcs=later_w)
    reps = MIX_R // CHUNK
    dmask_s, qdec_s, kdec_s, cdec_s = _decay_tables(CHUNK)
    tables_s = tuple(jnp.tile(t[PAST_LEN:], (reps, 1)) for t in rope) + (
        dmask_s, jnp.tile(qdec_s, (1, reps, 1)), jnp.tile(kdec_s, (1, reps, 1)), cdec_s)
    ya_s, yb_s, conv_s, lru_s, ret_s = _mixer(z_s, tables_s, params, states, is_prompt=False)
    ffn2_w, merge_w = later_b[:3], later_b[3:]

    def tail(ya, yb, z, x1):
        x2 = _merge_ln(ya, yb, z, x1, *merge_w, ln2_g, ln2_b)
        return _ffn_ln(x2, *ffn2_w, ln3_g, ln3_b)

    y_p = tail(ya_p, yb_p, z_p, x1_p)
    y_s = tail(ya_s, yb_s, z_s, x1_s)

    return (y_p.reshape(BATCH, SEQ, D_MODEL), y_s.reshape(DEC_BATCH, DEC_SEQ, D_MODEL),
            conv_p[None], lru_p.reshape(1, BATCH, D_RNN), ret_p[None],
            conv_s[None], lru_s.reshape(1, DEC_BATCH, D_RNN), ret_s[None])
```

```python
import functools

import jax
import jax.numpy as jnp
from jax import lax
from jax.experimental import pallas as pl
from jax.experimental.pallas import tpu as pltpu

F32 = jnp.float32
BF16 = jnp.bfloat16

D_MODEL = 2048
BATCH = 4
SEQ = 2048
DEPTH = 1
DEC_BATCH = 32
DEC_SEQ = 64
PAST_LEN = 2048
CHUNK = 64
D_RNN = 1024
LRU_BLOCKS = 16
LRU_BLOCK = D_RNN // LRU_BLOCKS
CONV_W = 4
LRU_C = 8.0
RET_HEADS = 8
RET_DK = 128
RET_DV = 128
D_RET_K = RET_HEADS * RET_DK
D_RET_V = RET_HEADS * RET_DV
D_FF = 5632
DN_ALPHA = (2.0 * DEPTH) ** 0.25
LN_EPS = 1e-5
GN_EPS = 1e-5
ROPE_BASE = 10000.0
D_IN = 2 * D_RNN + 2 * D_RET_K + 2 * D_RET_V + 2 * D_MODEL

M_PROMPT = BATCH * SEQ
M_SAMPLE = DEC_BATCH * DEC_SEQ

V7X_VMEM_BYTES = 64 * 1024 * 1024
VMEM_LIMIT = V7X_VMEM_BYTES - 8 * 1024 * 1024
VMEM_LIMIT_BIG = V7X_VMEM_BYTES - 3 * 1024 * 1024

FFN_TM = 1024
FFN_SUB = 512
FFN_SUB_LAST = 256
FFN_TF = 512
FFN_TF_EXPORT = 256
LN_ROWS = 128
PROJ_TM = 1024
PROJ_TN = 1024
PROJ_TN_EXPORT = 512
MIX_R = 256
MERGE_TM = 256
MERGE_SUB = 128
CONV_PAD = 8


def _layer_norm_rows(y, g, b):
    mu = jnp.mean(y, axis=-1, keepdims=True)
    yc = y - mu
    var = jnp.mean(yc * yc, axis=-1, keepdims=True)
    return yc * lax.rsqrt(var + LN_EPS) * g + b


def _ffn_ln_kernel(x_ref, wg_ref, wu_ref, wd_ref, g_ref, b_ref, o_ref, *rest, export):
    f = pl.program_id(1)
    last = pl.num_programs(1) - 1
    xb_ref = rest[-1]

    def sub_tiles(size):
        return [slice(r * size, (r + 1) * size) for r in range(FFN_TM // size)]

    def weights():
        if not export:
            return wg_ref[...], wu_ref[...], wd_ref[...]
        ws = tuple(w[...].astype(BF16) for w in (wg_ref, wu_ref, wd_ref))
        for dst, w in zip(rest[:3], ws):
            dst[...] = w
        return ws

    def seed(rows):
        x = x_ref[rows, :]
        o_ref[rows, :] = (2.0 * DN_ALPHA) * x
        xb_ref[rows, :] = x.astype(BF16)

    def accumulate(rows, ws):
        wg, wu, wd = ws
        xb = xb_ref[rows, :]
        gate = jnp.dot(xb, wg, preferred_element_type=F32)
        up = jnp.dot(xb, wu, preferred_element_type=F32)
        h = (gate * jax.nn.sigmoid(gate) * up).astype(BF16)
        o_ref[rows, :] += jnp.dot(h, wd, preferred_element_type=F32)

    def finish(rows):
        for c in range(rows.start, rows.stop, LN_ROWS):
            chunk = slice(c, c + LN_ROWS)
            acc = o_ref[chunk, :]
            mu = jnp.mean(acc, axis=-1, keepdims=True)
            ac = acc - mu
            var = jnp.mean(ac * ac, axis=-1, keepdims=True)
            scale = 0.5 * lax.rsqrt(0.25 * var + LN_EPS)
            o_ref[chunk, :] = ac * scale * g_ref[...] + b_ref[...]

    @pl.when(f == 0)
    def _():
        ws = weights()
        for rows in sub_tiles(FFN_SUB):
            seed(rows)
            accumulate(rows, ws)

    @pl.when(jnp.logical_and(f > 0, f < last))
    def _():
        ws = weights()
        for rows in sub_tiles(FFN_SUB):
            accumulate(rows, ws)

    @pl.when(f == last)
    def _():
        ws = weights()
        for rows in sub_tiles(FFN_SUB_LAST):
            accumulate(rows, ws)
            finish(rows)


def _ffn_ln(x, wg, wu, wd, ln_g, ln_b, *, export=False):
    m = x.shape[0]
    tf = FFN_TF_EXPORT if export else FFN_TF
    nf = D_FF // tf
    grid = (m // FFN_TM, nf)
    out_shape = [jax.ShapeDtypeStruct((m, D_MODEL), F32)]
    out_specs = [pl.BlockSpec((FFN_TM, D_MODEL), lambda i, f: (i, 0))]
    x_mode = {}
    if export:
        out_shape += [jax.ShapeDtypeStruct(w.shape, BF16) for w in (wg, wu, wd)]
        out_specs += [
            pl.BlockSpec((D_MODEL, tf), lambda i, f: (0, f)),
            pl.BlockSpec((D_MODEL, tf), lambda i, f: (0, f)),
            pl.BlockSpec((tf, D_MODEL), lambda i, f: (f, 0)),
        ]
        x_mode = dict(pipeline_mode=pl.Buffered(1))
    out = pl.pallas_call(
        functools.partial(_ffn_ln_kernel, export=export),
        out_shape=out_shape,
        grid=grid,
        in_specs=[
            pl.BlockSpec((FFN_TM, D_MODEL), lambda i, f: (i, 0), **x_mode),
            pl.BlockSpec((D_MODEL, tf), lambda i, f: (0, f)),
            pl.BlockSpec((D_MODEL, tf), lambda i, f: (0, f)),
            pl.BlockSpec((tf, D_MODEL), lambda i, f: (f, 0)),
            pl.BlockSpec((1, D_MODEL), lambda i, f: (0, 0)),
            pl.BlockSpec((1, D_MODEL), lambda i, f: (0, 0)),
        ],
        out_specs=out_specs,
        scratch_shapes=[pltpu.VMEM((FFN_TM, D_MODEL), BF16)],
        compiler_params=pltpu.CompilerParams(
            dimension_semantics=("arbitrary" if export else "parallel", "arbitrary"),
            vmem_limit_bytes=VMEM_LIMIT_BIG),
        name="ffn_ln_export" if export else "ffn_ln",
    )(x, wg, wu, wd, ln_g, ln_b)
    return out if export else out[0]


def _in_proj_kernel(x_ref, w_ref, o_ref, *rest, export):
    xb_ref = rest[-1]

    @pl.when(pl.program_id(1) == 0)
    def _():
        xb_ref[...] = x_ref[...].astype(BF16)

    if export:
        w = w_ref[...].astype(BF16)
        rest[0][...] = w
    else:
        w = w_ref[...]
    o_ref[...] = jnp.dot(xb_ref[...], w, preferred_element_type=F32)


def _in_proj(x1, w_in, *, export=False):
    m = x1.shape[0]
    tm, tn = (m, PROJ_TN_EXPORT) if export else (PROJ_TM, PROJ_TN)
    assert not export or m == tm
    out_shape = [jax.ShapeDtypeStruct((m, D_IN), F32)]
    out_specs = [pl.BlockSpec((tm, tn), lambda i, j: (i, j))]
    x_mode = {}
    if export:
        out_shape.append(jax.ShapeDtypeStruct(w_in.shape, BF16))
        out_specs.append(pl.BlockSpec((D_MODEL, tn), lambda i, j: (0, j)))
        x_mode = dict(pipeline_mode=pl.Buffered(1))
    out = pl.pallas_call(
        functools.partial(_in_proj_kernel, export=export),
        out_shape=out_shape,
        grid=(m // tm, D_IN // tn),
        in_specs=[
            pl.BlockSpec((tm, D_MODEL), lambda i, j: (i, 0), **x_mode),
            pl.BlockSpec((D_MODEL, tn), lambda i, j: (0, j)),
        ],
        out_specs=out_specs,
        scratch_shapes=[pltpu.VMEM((tm, D_MODEL), BF16)],
        compiler_params=pltpu.CompilerParams(
            dimension_semantics=("parallel", "arbitrary"), vmem_limit_bytes=VMEM_LIMIT),
        name="in_proj_export" if export else "in_proj",
    )(x1, w_in)
    return out if export else out[0]


def _mixer_kernel(*refs, is_prompt, n_cast):
    n_chunks = MIX_R // CHUNK
    (xa_ref, ga_ref, q_ref, k_ref, v_ref, g_ref, cos_ref, sin_ref,
     dmask_ref, qdec_ref, kdec_ref, cdec_ref,
     convw_ref, convb_ref, wri_ref, bri_ref, lam_ref, gng_ref, gnb_ref) = refs[:19]
    refs = refs[19:]
    if is_prompt:
        conv0_ref = h0_ref = s0_ref = None
    else:
        conv0_ref, h0_ref, s0_ref = refs[:3]
        refs = refs[3:]
    cast_src, refs = refs[:n_cast], refs[n_cast:]
    ya_ref, yb_ref, convo_ref, ho_ref, so_ref = refs[:5]
    cast_dst, refs = refs[5:5 + n_cast], refs[5 + n_cast:]
    (xpad_ref, xc_ref, a_ref, u_ref, hs_ref,
     qb_ref, kb_ref, qd_ref, kd_ref, vb_ref, sc_ref, oh_ref) = refs

    for src, dst in zip(cast_src, cast_dst):
        dst[...] = src[...].astype(BF16)

    if is_prompt:
        @pl.when(pl.program_id(1) == 0)
        def _():
            convo_ref[...] = jnp.zeros_like(convo_ref)
            ho_ref[...] = jnp.zeros_like(ho_ref)
            so_ref[...] = jnp.zeros_like(so_ref)

    n_seg = 1 if is_prompt else n_chunks
    seg_len = MIX_R // n_seg
    conv_b = convb_ref[...]
    w0, w1, w2, w3 = (convw_ref[j:j + 1, :] for j in range(CONV_W))
    lo = CONV_PAD - (CONV_W - 1)
    for sg in range(n_seg):
        rows = slice(sg * seg_len, (sg + 1) * seg_len)
        xpad_ref[sg, lo:CONV_PAD, :] = convo_ref[0] if is_prompt else conv0_ref[sg]
        xpad_ref[sg, CONV_PAD:CONV_PAD + seg_len, :] = xa_ref[rows, :]
        acc = conv_b + w0 * xpad_ref[sg, lo:lo + seg_len, :]
        acc = acc + w1 * xpad_ref[sg, lo + 1:lo + 1 + seg_len, :]
        acc = acc + w2 * xpad_ref[sg, lo + 2:lo + 2 + seg_len, :]
        acc = acc + w3 * xpad_ref[sg, lo + 3:lo + 3 + seg_len, :]
        xc_ref[rows, :] = acc
        convo_ref[sg] = xpad_ref[sg, lo + seg_len:CONV_PAD + seg_len, :]

    xc = xc_ref[...]
    ri = jnp.dot(xc.astype(BF16), wri_ref[...], preferred_element_type=F32) + bri_ref[...]
    r = jax.nn.sigmoid(ri[:, :D_RNN])
    i = jax.nn.sigmoid(ri[:, D_RNN:])
    log_a = -LRU_C * r * jax.nn.softplus(-lam_ref[...])
    a = jnp.exp(log_a)
    a_ref[...] = a
    w = jnp.tanh(-log_a) * (1.0 + a * a)
    u_ref[...] = jnp.where(w > 0.0, w * lax.rsqrt(w), 0.0) * (i * xc)

    for sg in range(n_seg):
        h_init = ho_ref[0] if is_prompt else h0_ref[sg]

        def step(t, h, base=sg * seg_len):
            row = pl.ds(base + t, 1)
            h = a_ref[row, :] * h + u_ref[row, :]
            hs_ref[row, :] = h
            return h

        ho_ref[sg] = lax.fori_loop(0, seg_len, step, h_init, unroll=8)

    ya_ref[...] = (hs_ref[...] * jax.nn.gelu(ga_ref[...])).astype(BF16)

    k_scale = RET_DK ** -0.5

    blk = MIX_R if is_prompt else CHUNK
    head_cols = [slice(hd * RET_DK, (hd + 1) * RET_DK) for hd in range(RET_HEADS)]
    blk_rows = [slice(c * blk, (c + 1) * blk) for c in range(MIX_R // blk)]
    cos, sin = cos_ref[...], sin_ref[...]

    for hd, cols in enumerate(head_cols):
        qh, kh = q_ref[:, cols], k_ref[:, cols]
        qr = qh * cos + pltpu.roll(qh, RET_DK // 2, 1) * sin
        kr = (kh * cos + pltpu.roll(kh, RET_DK // 2, 1) * sin) * k_scale
        qb_ref[:, cols] = qr.astype(BF16)
        kb_ref[:, cols] = kr.astype(BF16)
        qd_ref[:, cols] = (qr * qdec_ref[hd]).astype(BF16)
        kd_ref[:, cols] = (kr * kdec_ref[hd]).astype(BF16)
        vb_ref[:, cols] = v_ref[:, cols].astype(BF16)

    for rows in blk_rows:
        for hd, cols in enumerate(head_cols):
            s = lax.dot_general(qb_ref[rows, cols], kb_ref[rows, cols], (((1,), (1,)), ((), ())),
                                preferred_element_type=F32) * dmask_ref[hd]
            sc_ref[hd, rows, :] = s.astype(BF16)

    for c, rows in enumerate(blk_rows):
        for hd, cols in enumerate(head_cols):
            state = so_ref[0, hd] if is_prompt else s0_ref[c, hd]
            vb = vb_ref[rows, cols]
            o = jnp.dot(sc_ref[hd, rows, :], vb, preferred_element_type=F32)
            o = o + jnp.dot(qd_ref[rows, cols], state.astype(BF16), preferred_element_type=F32)
            oh_ref[rows, cols] = o
            new_state = cdec_ref[hd] * state + lax.dot_general(
                kd_ref[rows, cols], vb, (((0,), (0,)), ((), ())), preferred_element_type=F32)
            if is_prompt:
                so_ref[0, hd] = new_state
            else:
                so_ref[c, hd] = new_state

    for cols in head_cols:
        o = oh_ref[:, cols]
        mu = jnp.mean(o, axis=-1, keepdims=True)
        oc = o - mu
        var = jnp.mean(oc * oc, axis=-1, keepdims=True)
        on = oc * lax.rsqrt(var + GN_EPS) * gng_ref[:, cols] + gnb_ref[:, cols]
        gh = g_ref[:, cols]
        yb_ref[:, cols] = (gh * jax.nn.sigmoid(gh) * on).astype(BF16)


def _mixer(z, tables, params, states, *, is_prompt, cast_srcs=()):
    cos_t, sin_t, dmask, qdec, kdec, cdec = tables
    n_chunks = MIX_R // CHUNK
    if is_prompt:
        tiles_per_seq = SEQ // MIX_R
        grid = (BATCH, tiles_per_seq)
        n_seq = BATCH
        n_state = 1
        row_blk = lambda s, t: s * tiles_per_seq + t
        zspec = lambda col: pl.BlockSpec((MIX_R, D_RNN), lambda s, t, col=col: (row_blk(s, t), col))
        tab_spec = pl.BlockSpec((MIX_R, RET_DK), lambda s, t: (t, 0))
        state_idx = lambda s, t: s
        out_row = lambda s, t: (row_blk(s, t), 0)
        m_rows = M_PROMPT
        sem = ("parallel", "arbitrary")
    else:
        grid = (M_SAMPLE // MIX_R,)
        n_seq = DEC_BATCH
        n_state = n_chunks
        zspec = lambda col: pl.BlockSpec((MIX_R, D_RNN), lambda i, col=col: (i, col))
        tab_spec = pl.BlockSpec((MIX_R, RET_DK), lambda i: (0, 0))
        state_idx = lambda i: i
        out_row = lambda i: (i, 0)
        m_rows = M_SAMPLE
        sem = ("parallel",)

    def const_spec(shape):
        nd = len(shape)
        return pl.BlockSpec(shape, lambda *_: (0,) * nd)

    def state_spec(shape):
        nd = len(shape)
        return pl.BlockSpec((n_state,) + shape, lambda *g: (state_idx(*g),) + (0,) * nd)

    in_specs = [zspec(c) for c in range(6)] + [tab_spec, tab_spec] + [
        const_spec(dmask.shape), const_spec(qdec.shape), const_spec(kdec.shape), const_spec(cdec.shape)
    ] + [const_spec(p.shape) for p in params]
    args = [z] * 6 + [cos_t, sin_t, dmask, qdec, kdec, cdec] + list(params)
    if not is_prompt:
        in_specs += [state_spec((CONV_W - 1, D_RNN)), state_spec((1, D_RNN)),
                     state_spec((RET_HEADS, RET_DK, RET_DV))]
        args += list(states)

    n_steps = 1
    for g in grid:
        n_steps *= g
    step_idx = (lambda s, t: (s * grid[1] + t, 0)) if is_prompt else (lambda i: (i, 0))
    cast_specs = [pl.BlockSpec((w.shape[0] // n_steps, w.shape[1]), step_idx) for w in cast_srcs]
    in_specs += cast_specs
    args += list(cast_srcs)

    n_seg = 1 if is_prompt else n_chunks
    seg_len = MIX_R // n_seg
    out_shape = [
        jax.ShapeDtypeStruct((m_rows, D_RNN), BF16),
        jax.ShapeDtypeStruct((m_rows, D_RET_V), BF16),
        jax.ShapeDtypeStruct((n_seq, CONV_W - 1, D_RNN), F32),
        jax.ShapeDtypeStruct((n_seq, 1, D_RNN), F32),
        jax.ShapeDtypeStruct((n_seq, RET_HEADS, RET_DK, RET_DV), F32),
    ]
    out_specs = [
        pl.BlockSpec((MIX_R, D_RNN), out_row),
        pl.BlockSpec((MIX_R, D_RET_V), out_row),
        state_spec((CONV_W - 1, D_RNN)),
        state_spec((1, D_RNN)),
        state_spec((RET_HEADS, RET_DK, RET_DV)),
    ] + cast_specs
    out_shape += [jax.ShapeDtypeStruct(w.shape, BF16) for w in cast_srcs]
    scratch = [
        pltpu.VMEM((n_seg, CONV_PAD + seg_len, D_RNN), F32),
        pltpu.VMEM((MIX_R, D_RNN), F32),
        pltpu.VMEM((MIX_R, D_RNN), F32),
        pltpu.VMEM((MIX_R, D_RNN), F32),
        pltpu.VMEM((MIX_R, D_RNN), F32),
    ] + [pltpu.VMEM((MIX_R, D_RET_K), BF16)] * 5 + [
        pltpu.VMEM((RET_HEADS, MIX_R, MIX_R if is_prompt else CHUNK), BF16),
        pltpu.VMEM((MIX_R, D_RET_V), F32),
    ]
    return pl.pallas_call(
        functools.partial(_mixer_kernel, is_prompt=is_prompt, n_cast=len(cast_srcs)),
        out_shape=out_shape,
        grid=grid,
        in_specs=in_specs,
        out_specs=out_specs,
        scratch_shapes=scratch,
        compiler_params=pltpu.CompilerParams(dimension_semantics=sem, vmem_limit_bytes=VMEM_LIMIT),
        name="mixer_prompt" if is_prompt else "mixer_sample",
    )(*args)


def _merge_ln_kernel(ya_ref, yb_ref, ga_ref, gb_ref, x_ref, wa_ref, wb_ref, wo_ref, g_ref, b_ref, o_ref):
    for r in range(MERGE_TM // MERGE_SUB):
        rows = slice(r * MERGE_SUB, (r + 1) * MERGE_SUB)
        pa = jnp.dot(ya_ref[rows, :], wa_ref[...], preferred_element_type=F32)
        pb = jnp.dot(yb_ref[rows, :], wb_ref[...], preferred_element_type=F32)
        merged = jax.nn.sigmoid(ga_ref[rows, :]) * pa + jax.nn.sigmoid(gb_ref[rows, :]) * pb
        mo = jnp.dot(merged.astype(BF16), wo_ref[...], preferred_element_type=F32)
        o_ref[rows, :] = _layer_norm_rows(DN_ALPHA * x_ref[rows, :] + mo, g_ref[...], b_ref[...])


def _merge_ln(ya, yb, z, x1, wa, wb, wo, ln_g, ln_b):
    m = x1.shape[0]
    gate_a_blk = (2 * D_RNN + 2 * D_RET_K + 2 * D_RET_V) // D_MODEL
    row = lambda i: (i, 0)
    const = lambda i: (0, 0)
    return pl.pallas_call(
        _merge_ln_kernel,
        out_shape=jax.ShapeDtypeStruct((m, D_MODEL), F32),
        grid=(m // MERGE_TM,),
        in_specs=[
            pl.BlockSpec((MERGE_TM, D_RNN), row),
            pl.BlockSpec((MERGE_TM, D_RET_V), row),
            pl.BlockSpec((MERGE_TM, D_MODEL), lambda i: (i, gate_a_blk)),
            pl.BlockSpec((MERGE_TM, D_MODEL), lambda i: (i, gate_a_blk + 1)),
            pl.BlockSpec((MERGE_TM, D_MODEL), row),
            pl.BlockSpec((D_RNN, D_MODEL), const, pipeline_mode=pl.Buffered(1)),
            pl.BlockSpec((D_RET_V, D_MODEL), const, pipeline_mode=pl.Buffered(1)),
            pl.BlockSpec((D_MODEL, D_MODEL), const, pipeline_mode=pl.Buffered(1)),
            pl.BlockSpec((1, D_MODEL), const),
            pl.BlockSpec((1, D_MODEL), const),
        ],
        out_specs=pl.BlockSpec((MERGE_TM, D_MODEL), row),
        compiler_params=pltpu.CompilerParams(
            dimension_semantics=("parallel",), vmem_limit_bytes=VMEM_LIMIT),
        name="merge_ln",
    )(ya, yb, z, z, x1, wa, wb, wo, ln_g, ln_b)


def _rope_tables(pos):
    d = RET_DK
    inv_freq = ROPE_BASE ** (-jnp.arange(0, d, 2, dtype=F32) / d)
    ang = pos.astype(F32)[:, None] * inv_freq[None, :]
    cos, sin = jnp.cos(ang), jnp.sin(ang)
    return jnp.concatenate([cos, cos], axis=-1), jnp.concatenate([-sin, sin], axis=-1)


def _decay_tables(blk):
    log_g = jnp.log1p(-jnp.exp2(-5.0 - jnp.arange(RET_HEADS, dtype=F32)))
    idx = jnp.arange(blk, dtype=F32)
    diff = idx[:, None] - idx[None, :]
    dmask = jnp.where(diff >= 0, jnp.exp(log_g[:, None, None] * jnp.maximum(diff, 0.0)), 0.0)
    q_dec = jnp.exp(log_g[:, None] * (idx[None, :] + 1.0))
    k_dec = jnp.exp(log_g[:, None] * (blk - 1.0 - idx[None, :]))
    chunk_dec = jnp.exp(log_g * blk)
    lanes = (RET_HEADS, blk, RET_DK)
    return (dmask, jnp.broadcast_to(q_dec[:, :, None], lanes), jnp.broadcast_to(k_dec[:, :, None], lanes),
            jnp.broadcast_to(chunk_dec[:, None, None], (RET_HEADS, 1, RET_DV)))


def _block_diag(w):
    eye = jnp.eye(LRU_BLOCKS, dtype=w.dtype)
    return jnp.einsum("nkj,nm->nkmj", w, eye).reshape(D_RNN, D_RNN)


def kernel(x_prompt, x_sample, state_conv, state_lru, state_ret, ffn1_w_gate, ffn1_w_up, ffn1_w_down, ln1_g, ln1_b, w_in, conv_w, conv_b, rg_w, rg_b, ig_w, ig_b, lru_lambda, ret_gn_g, ret_gn_b, w_a_proj, w_b_proj, w_o, ln2_g, ln2_b, ffn2_w_gate, ffn2_w_up, ffn2_w_down, ln3_g, ln3_b):
    assert DEPTH == 1
    w_ri = jnp.concatenate([_block_diag(rg_w[0]), _block_diag(ig_w[0])], axis=1).astype(BF16)
    b_ri = jnp.concatenate([rg_b, ig_b], axis=1)
    params = (conv_w[0], conv_b, w_ri, b_ri, lru_lambda, ret_gn_g, ret_gn_b)
    rope = _rope_tables(jnp.arange(PAST_LEN + DEC_SEQ, dtype=jnp.int32))
    x_p = x_prompt.reshape(M_PROMPT, D_MODEL)
    x_s = x_sample.reshape(M_SAMPLE, D_MODEL)
    states = (state_conv[0], state_lru[0][:, None, :], state_ret[0])

    x1_s, *ffn1_w = _ffn_ln(x_s, ffn1_w_gate[0], ffn1_w_up[0], ffn1_w_down[0], ln1_g, ln1_b, export=True)
    x1_p = _ffn_ln(x_p, *ffn1_w, ln1_g, ln1_b)
    z_s, w_in_b = _in_proj(x1_s, w_in[0], export=True)
    z_p = _in_proj(x1_p, w_in_b)

    later_w = (ffn2_w_gate[0], ffn2_w_up[0], ffn2_w_down[0], w_a_proj[0], w_b_proj[0], w_o[0])
    ya_p, yb_p, conv_p, lru_p, ret_p, *later_b = _mixer(
        z_p, rope + _decay_tables(MIX_R), params, None, is_prompt=True, cast_srcs=later_w)
    reps = MIX_R // CHUNK
    dmask_s, qdec_s, kdec_s, cdec_s = _decay_tables(CHUNK)
    tables_s = tuple(jnp.tile(t[PAST_LEN:], (reps, 1)) for t in rope) + (
        dmask_s, jnp.tile(qdec_s, (1, reps, 1)), jnp.tile(kdec_s, (1, reps, 1)), cdec_s)
    ya_s, yb_s, conv_s, lru_s, ret_s = _mixer(z_s, tables_s, params, states, is_prompt=False)
    ffn2_w, merge_w = later_b[:3], later_b[3:]

    def tail(ya, yb, z, x1):
        x2 = _merge_ln(ya, yb, z, x1, *merge_w, ln2_g, ln2_b)
        return _ffn_ln(x2, *ffn2_w, ln3_g, ln3_b)

    y_p = tail(ya_p, yb_p, z_p, x1_p)
    y_s = tail(ya_s, yb_s, z_s, x1_s)

    return (y_p.reshape(BATCH, SEQ, D_MODEL), y_s.reshape(DEC_BATCH, DEC_SEQ, D_MODEL),
            conv_p[None], lru_p.reshape(1, BATCH, D_RNN), ret_p[None],
            conv_s[None], lru_s.reshape(1, DEC_BATCH, D_RNN), ret_s[None])
```

```python
import functools

import jax
import jax.numpy as jnp
from jax import lax
from jax.experimental import pallas as pl
from jax.experimental.pallas import tpu as pltpu

F32 = jnp.float32
BF16 = jnp.bfloat16

D_MODEL = 2048
BATCH = 4
SEQ = 2048
DEPTH = 1
DEC_BATCH = 32
DEC_SEQ = 64
PAST_LEN = 2048
CHUNK = 64
D_RNN = 1024
LRU_BLOCKS = 16
LRU_BLOCK = D_RNN // LRU_BLOCKS
CONV_W = 4
LRU_C = 8.0
RET_HEADS = 8
RET_DK = 128
RET_DV = 128
D_RET_K = RET_HEADS * RET_DK
D_RET_V = RET_HEADS * RET_DV
D_FF = 5632
DN_ALPHA = (2.0 * DEPTH) ** 0.25
LN_EPS = 1e-5
GN_EPS = 1e-5
ROPE_BASE = 10000.0
D_IN = 2 * D_RNN + 2 * D_RET_K + 2 * D_RET_V + 2 * D_MODEL

M_PROMPT = BATCH * SEQ
M_SAMPLE = DEC_BATCH * DEC_SEQ

V7X_VMEM_BYTES = 64 * 1024 * 1024
VMEM_LIMIT = V7X_VMEM_BYTES - 8 * 1024 * 1024
VMEM_LIMIT_BIG = V7X_VMEM_BYTES - 3 * 1024 * 1024

FFN_TM = 1024
FFN_SUB = 512
FFN_SUB_LAST = 256
FFN_TF = 512
FFN_TF_EXPORT = 256
LN_ROWS = 128
PROJ_TM = 1024
PROJ_TN = 1024
PROJ_TN_EXPORT = 512
MIX_R = 256
MERGE_TM = 256
MERGE_SUB = 128
CONV_PAD = 8


def _layer_norm_rows(y, g, b):
    mu = jnp.mean(y, axis=-1, keepdims=True)
    yc = y - mu
    var = jnp.mean(yc * yc, axis=-1, keepdims=True)
    return yc * lax.rsqrt(var + LN_EPS) * g + b


def _ffn_ln_kernel(x_ref, wg_ref, wu_ref, wd_ref, g_ref, b_ref, o_ref, *rest, export):
    f = pl.program_id(1)
    last = pl.num_programs(1) - 1
    xb_ref = rest[-1]

    def sub_tiles(size):
        return [slice(r * size, (r + 1) * size) for r in range(FFN_TM // size)]

    def weights():
        if not export:
            return wg_ref[...], wu_ref[...], wd_ref[...]
        ws = tuple(w[...].astype(BF16) for w in (wg_ref, wu_ref, wd_ref))
        for dst, w in zip(rest[:3], ws):
            dst[...] = w
        return ws

    def seed(rows):
        x = x_ref[rows, :]
        o_ref[rows, :] = (2.0 * DN_ALPHA) * x
        xb_ref[rows, :] = x.astype(BF16)

    def accumulate(rows, ws):
        wg, wu, wd = ws
        xb = xb_ref[rows, :]
        gate = jnp.dot(xb, wg, preferred_element_type=F32)
        up = jnp.dot(xb, wu, preferred_element_type=F32)
        h = (gate * jax.nn.sigmoid(gate) * up).astype(BF16)
        o_ref[rows, :] += jnp.dot(h, wd, preferred_element_type=F32)

    def finish(rows):
        for c in range(rows.start, rows.stop, LN_ROWS):
            chunk = slice(c, c + LN_ROWS)
            acc = o_ref[chunk, :]
            mu = jnp.mean(acc, axis=-1, keepdims=True)
            ac = acc - mu
            var = jnp.mean(ac * ac, axis=-1, keepdims=True)
            scale = 0.5 * lax.rsqrt(0.25 * var + LN_EPS)
            o_ref[chunk, :] = ac * scale * g_ref[...] + b_ref[...]

    @pl.when(f == 0)
    def _():
        ws = weights()
        for rows in sub_tiles(FFN_SUB):
            seed(rows)
            accumulate(rows, ws)

    @pl.when(jnp.logical_and(f > 0, f < last))
    def _():
        ws = weights()
        for rows in sub_tiles(FFN_SUB):
            accumulate(rows, ws)

    @pl.when(f == last)
    def _():
        ws = weights()
        for rows in sub_tiles(FFN_SUB_LAST):
            accumulate(rows, ws)
            finish(rows)


def _ffn_ln(x, wg, wu, wd, ln_g, ln_b, *, export=False):
    m = x.shape[0]
    n_tiles = m // FFN_TM
    tf = FFN_TF_EXPORT if export else FFN_TF
    grid = (n_tiles, D_FF // tf)
    out_shape = [jax.ShapeDtypeStruct((m, D_MODEL), F32)]
    out_specs = [pl.BlockSpec((FFN_TM, D_MODEL), lambda i, f: (i, 0))]
    x_mode = {}
    if export:
        out_shape += [jax.ShapeDtypeStruct((n_tiles,) + w.shape, BF16) for w in (wg, wu, wd)]
        out_specs += [
            pl.BlockSpec((None, D_MODEL, tf), lambda i, f: (i, 0, f)),
            pl.BlockSpec((None, D_MODEL, tf), lambda i, f: (i, 0, f)),
            pl.BlockSpec((None, tf, D_MODEL), lambda i, f: (i, f, 0)),
        ]
        x_mode = dict(pipeline_mode=pl.Buffered(1))

    def w_spec(w, block, idx):
        if w.ndim == 2:
            return pl.BlockSpec(block, idx)
        return pl.BlockSpec((None,) + block, lambda i, f: (0,) + idx(i, f))

    out = pl.pallas_call(
        functools.partial(_ffn_ln_kernel, export=export),
        out_shape=out_shape,
        grid=grid,
        in_specs=[
            pl.BlockSpec((FFN_TM, D_MODEL), lambda i, f: (i, 0), **x_mode),
            w_spec(wg, (D_MODEL, tf), lambda i, f: (0, f)),
            w_spec(wu, (D_MODEL, tf), lambda i, f: (0, f)),
            w_spec(wd, (tf, D_MODEL), lambda i, f: (f, 0)),
            pl.BlockSpec((1, D_MODEL), lambda i, f: (0, 0)),
            pl.BlockSpec((1, D_MODEL), lambda i, f: (0, 0)),
        ],
        out_specs=out_specs,
        scratch_shapes=[pltpu.VMEM((FFN_TM, D_MODEL), BF16)],
        compiler_params=pltpu.CompilerParams(
            dimension_semantics=("parallel", "arbitrary"), vmem_limit_bytes=VMEM_LIMIT_BIG),
        name="ffn_ln_export" if export else "ffn_ln",
    )(x, wg, wu, wd, ln_g, ln_b)
    return out if export else out[0]


def _in_proj_kernel(x_ref, w_ref, o_ref, *rest, export):
    xb_ref = rest[-1]

    @pl.when(pl.program_id(1) == 0)
    def _():
        xb_ref[...] = x_ref[...].astype(BF16)

    if export:
        w = w_ref[...].astype(BF16)
        rest[0][...] = w
    else:
        w = w_ref[...]
    o_ref[...] = jnp.dot(xb_ref[...], w, preferred_element_type=F32)


def _in_proj(x1, w_in, *, export=False):
    m = x1.shape[0]
    tm, tn = (m, PROJ_TN_EXPORT) if export else (PROJ_TM, PROJ_TN)
    assert not export or m == tm
    out_shape = [jax.ShapeDtypeStruct((m, D_IN), F32)]
    out_specs = [pl.BlockSpec((tm, tn), lambda i, j: (i, j))]
    x_mode = {}
    if export:
        out_shape.append(jax.ShapeDtypeStruct(w_in.shape, BF16))
        out_specs.append(pl.BlockSpec((D_MODEL, tn), lambda i, j: (0, j)))
        x_mode = dict(pipeline_mode=pl.Buffered(1))
    out = pl.pallas_call(
        functools.partial(_in_proj_kernel, export=export),
        out_shape=out_shape,
        grid=(m // tm, D_IN // tn),
        in_specs=[
            pl.BlockSpec((tm, D_MODEL), lambda i, j: (i, 0), **x_mode),
            pl.BlockSpec((D_MODEL, tn), lambda i, j: (0, j)),
        ],
        out_specs=out_specs,
        scratch_shapes=[pltpu.VMEM((tm, D_MODEL), BF16)],
        compiler_params=pltpu.CompilerParams(
            dimension_semantics=("parallel", "arbitrary"), vmem_limit_bytes=VMEM_LIMIT),
        name="in_proj_export" if export else "in_proj",
    )(x1, w_in)
    return out if export else out[0]


def _mixer_kernel(*refs, is_prompt, n_cast):
    n_chunks = MIX_R // CHUNK
    (xa_ref, ga_ref, q_ref, k_ref, v_ref, g_ref, cos_ref, sin_ref,
     dmask_ref, qdec_ref, kdec_ref, cdec_ref,
     convw_ref, convb_ref, wri_ref, bri_ref, lam_ref, gng_ref, gnb_ref) = refs[:19]
    refs = refs[19:]
    if is_prompt:
        conv0_ref = h0_ref = s0_ref = None
    else:
        conv0_ref, h0_ref, s0_ref = refs[:3]
        refs = refs[3:]
    cast_src, refs = refs[:n_cast], refs[n_cast:]
    ya_ref, yb_ref, convo_ref, ho_ref, so_ref = refs[:5]
    cast_dst, refs = refs[5:5 + n_cast], refs[5 + n_cast:]
    (xpad_ref, xc_ref, a_ref, u_ref, hs_ref,
     qb_ref, kb_ref, qd_ref, kd_ref, vb_ref, sc_ref, oh_ref) = refs

    for src, dst in zip(cast_src, cast_dst):
        dst[...] = src[...].astype(BF16)

    if is_prompt:
        @pl.when(pl.program_id(1) == 0)
        def _():
            convo_ref[...] = jnp.zeros_like(convo_ref)
            ho_ref[...] = jnp.zeros_like(ho_ref)
            so_ref[...] = jnp.zeros_like(so_ref)

    n_seg = 1 if is_prompt else n_chunks
    seg_len = MIX_R // n_seg
    conv_b = convb_ref[...]
    w0, w1, w2, w3 = (convw_ref[j:j + 1, :] for j in range(CONV_W))
    lo = CONV_PAD - (CONV_W - 1)
    for sg in range(n_seg):
        rows = slice(sg * seg_len, (sg + 1) * seg_len)
        xpad_ref[sg, lo:CONV_PAD, :] = convo_ref[0] if is_prompt else conv0_ref[sg]
        xpad_ref[sg, CONV_PAD:CONV_PAD + seg_len, :] = xa_ref[rows, :]
        acc = conv_b + w0 * xpad_ref[sg, lo:lo + seg_len, :]
        acc = acc + w1 * xpad_ref[sg, lo + 1:lo + 1 + seg_len, :]
        acc = acc + w2 * xpad_ref[sg, lo + 2:lo + 2 + seg_len, :]
        acc = acc + w3 * xpad_ref[sg, lo + 3:lo + 3 + seg_len, :]
        xc_ref[rows, :] = acc
        convo_ref[sg] = xpad_ref[sg, lo + seg_len:CONV_PAD + seg_len, :]

    xc = xc_ref[...]
    ri = jnp.dot(xc.astype(BF16), wri_ref[...], preferred_element_type=F32) + bri_ref[...]
    r = jax.nn.sigmoid(ri[:, :D_RNN])
    i = jax.nn.sigmoid(ri[:, D_RNN:])
    log_a = -LRU_C * r * jax.nn.softplus(-lam_ref[...])
    a = jnp.exp(log_a)
    a_ref[...] = a
    w = jnp.tanh(-log_a) * (1.0 + a * a)
    u_ref[...] = jnp.where(w > 0.0, w * lax.rsqrt(w), 0.0) * (i * xc)

    for sg in range(n_seg):
        h_init = ho_ref[0] if is_prompt else h0_ref[sg]

        def step(t, h, base=sg * seg_len):
            row = pl.ds(base + t, 1)
            h = a_ref[row, :] * h + u_ref[row, :]
            hs_ref[row, :] = h
            return h

        ho_ref[sg] = lax.fori_loop(0, seg_len, step, h_init, unroll=8)

    ya_ref[...] = (hs_ref[...] * jax.nn.gelu(ga_ref[...])).astype(BF16)

    k_scale = RET_DK ** -0.5

    blk = MIX_R if is_prompt else CHUNK
    head_cols = [slice(hd * RET_DK, (hd + 1) * RET_DK) for hd in range(RET_HEADS)]
    blk_rows = [slice(c * blk, (c + 1) * blk) for c in range(MIX_R // blk)]
    cos, sin = cos_ref[...], sin_ref[...]

    for hd, cols in enumerate(head_cols):
        qh, kh = q_ref[:, cols], k_ref[:, cols]
        qr = qh * cos + pltpu.roll(qh, RET_DK // 2, 1) * sin
        kr = (kh * cos + pltpu.roll(kh, RET_DK // 2, 1) * sin) * k_scale
        qb_ref[:, cols] = qr.astype(BF16)
        kb_ref[:, cols] = kr.astype(BF16)
        qd_ref[:, cols] = (qr * qdec_ref[hd]).astype(BF16)
        kd_ref[:, cols] = (kr * kdec_ref[hd]).astype(BF16)
        vb_ref[:, cols] = v_ref[:, cols].astype(BF16)

    for rows in blk_rows:
        for hd, cols in enumerate(head_cols):
            s = lax.dot_general(qb_ref[rows, cols], kb_ref[rows, cols], (((1,), (1,)), ((), ())),
                                preferred_element_type=F32) * dmask_ref[hd]
            sc_ref[hd, rows, :] = s.astype(BF16)

    for c, rows in enumerate(blk_rows):
        for hd, cols in enumerate(head_cols):
            state = so_ref[0, hd] if is_prompt else s0_ref[c, hd]
            vb = vb_ref[rows, cols]
            o = jnp.dot(sc_ref[hd, rows, :], vb, preferred_element_type=F32)
            o = o + jnp.dot(qd_ref[rows, cols], state.astype(BF16), preferred_element_type=F32)
            oh_ref[rows, cols] = o
            new_state = cdec_ref[hd] * state + lax.dot_general(
                kd_ref[rows, cols], vb, (((0,), (0,)), ((), ())), preferred_element_type=F32)
            if is_prompt:
                so_ref[0, hd] = new_state
            else:
                so_ref[c, hd] = new_state

    for cols in head_cols:
        o = oh_ref[:, cols]
        mu = jnp.mean(o, axis=-1, keepdims=True)
        oc = o - mu
        var = jnp.mean(oc * oc, axis=-1, keepdims=True)
        on = oc * lax.rsqrt(var + GN_EPS) * gng_ref[:, cols] + gnb_ref[:, cols]
        gh = g_ref[:, cols]
        yb_ref[:, cols] = (gh * jax.nn.sigmoid(gh) * on).astype(BF16)


def _mixer(z, tables, params, states, *, is_prompt, cast_srcs=()):
    cos_t, sin_t, dmask, qdec, kdec, cdec = tables
    n_chunks = MIX_R // CHUNK
    if is_prompt:
        tiles_per_seq = SEQ // MIX_R
        grid = (BATCH, tiles_per_seq)
        n_seq = BATCH
        n_state = 1
        row_blk = lambda s, t: s * tiles_per_seq + t
        zspec = lambda col: pl.BlockSpec((MIX_R, D_RNN), lambda s, t, col=col: (row_blk(s, t), col))
        tab_spec = pl.BlockSpec((MIX_R, RET_DK), lambda s, t: (t, 0))
        state_idx = lambda s, t: s
        out_row = lambda s, t: (row_blk(s, t), 0)
        m_rows = M_PROMPT
        sem = ("parallel", "arbitrary")
    else:
        grid = (M_SAMPLE // MIX_R,)
        n_seq = DEC_BATCH
        n_state = n_chunks
        zspec = lambda col: pl.BlockSpec((MIX_R, D_RNN), lambda i, col=col: (i, col))
        tab_spec = pl.BlockSpec((MIX_R, RET_DK), lambda i: (0, 0))
        state_idx = lambda i: i
        out_row = lambda i: (i, 0)
        m_rows = M_SAMPLE
        sem = ("parallel",)

    def const_spec(shape):
        nd = len(shape)
        return pl.BlockSpec(shape, lambda *_: (0,) * nd)

    def state_spec(shape):
        nd = len(shape)
        return pl.BlockSpec((n_state,) + shape, lambda *g: (state_idx(*g),) + (0,) * nd)

    in_specs = [zspec(c) for c in range(6)] + [tab_spec, tab_spec] + [
        const_spec(dmask.shape), const_spec(qdec.shape), const_spec(kdec.shape), const_spec(cdec.shape)
    ] + [const_spec(p.shape) for p in params]
    args = [z] * 6 + [cos_t, sin_t, dmask, qdec, kdec, cdec] + list(params)
    if not is_prompt:
        in_specs += [state_spec((CONV_W - 1, D_RNN)), state_spec((1, D_RNN)),
                     state_spec((RET_HEADS, RET_DK, RET_DV))]
        args += list(states)

    n_steps = 1
    for g in grid:
        n_steps *= g
    step_idx = (lambda s, t: (s * grid[1] + t, 0)) if is_prompt else (lambda i: (i, 0))
    cast_specs = [pl.BlockSpec((w.shape[0] // n_steps, w.shape[1]), step_idx) for w in cast_srcs]
    in_specs += cast_specs
    args += list(cast_srcs)

    n_seg = 1 if is_prompt else n_chunks
    seg_len = MIX_R // n_seg
    out_shape = [
        jax.ShapeDtypeStruct((m_rows, D_RNN), BF16),
        jax.ShapeDtypeStruct((m_rows, D_RET_V), BF16),
        jax.ShapeDtypeStruct((n_seq, CONV_W - 1, D_RNN), F32),
        jax.ShapeDtypeStruct((n_seq, 1, D_RNN), F32),
        jax.ShapeDtypeStruct((n_seq, RET_HEADS, RET_DK, RET_DV), F32),
    ]
    out_specs = [
        pl.BlockSpec((MIX_R, D_RNN), out_row),
        pl.BlockSpec((MIX_R, D_RET_V), out_row),
        state_spec((CONV_W - 1, D_RNN)),
        state_spec((1, D_RNN)),
        state_spec((RET_HEADS, RET_DK, RET_DV)),
    ] + cast_specs
    out_shape += [jax.ShapeDtypeStruct(w.shape, BF16) for w in cast_srcs]
    scratch = [
        pltpu.VMEM((n_seg, CONV_PAD + seg_len, D_RNN), F32),
        pltpu.VMEM((MIX_R, D_RNN), F32),
        pltpu.VMEM((MIX_R, D_RNN), F32),
        pltpu.VMEM((MIX_R, D_RNN), F32),
        pltpu.VMEM((MIX_R, D_RNN), F32),
    ] + [pltpu.VMEM((MIX_R, D_RET_K), BF16)] * 5 + [
        pltpu.VMEM((RET_HEADS, MIX_R, MIX_R if is_prompt else CHUNK), BF16),
        pltpu.VMEM((MIX_R, D_RET_V), F32),
    ]
    return pl.pallas_call(
        functools.partial(_mixer_kernel, is_prompt=is_prompt, n_cast=len(cast_srcs)),
        out_shape=out_shape,
        grid=grid,
        in_specs=in_specs,
        out_specs=out_specs,
        scratch_shapes=scratch,
        compiler_params=pltpu.CompilerParams(dimension_semantics=sem, vmem_limit_bytes=VMEM_LIMIT),
        name="mixer_prompt" if is_prompt else "mixer_sample",
    )(*args)


def _merge_ln_kernel(ya_ref, yb_ref, ga_ref, gb_ref, x_ref, wa_ref, wb_ref, wo_ref, g_ref, b_ref, o_ref):
    for r in range(MERGE_TM // MERGE_SUB):
        rows = slice(r * MERGE_SUB, (r + 1) * MERGE_SUB)
        pa = jnp.dot(ya_ref[rows, :], wa_ref[...], preferred_element_type=F32)
        pb = jnp.dot(yb_ref[rows, :], wb_ref[...], preferred_element_type=F32)
        merged = jax.nn.sigmoid(ga_ref[rows, :]) * pa + jax.nn.sigmoid(gb_ref[rows, :]) * pb
        mo = jnp.dot(merged.astype(BF16), wo_ref[...], preferred_element_type=F32)
        o_ref[rows, :] = _layer_norm_rows(DN_ALPHA * x_ref[rows, :] + mo, g_ref[...], b_ref[...])


def _merge_ln(ya, yb, z, x1, wa, wb, wo, ln_g, ln_b):
    m = x1.shape[0]
    gate_a_blk = (2 * D_RNN + 2 * D_RET_K + 2 * D_RET_V) // D_MODEL
    row = lambda i: (i, 0)
    const = lambda i: (0, 0)
    return pl.pallas_call(
        _merge_ln_kernel,
        out_shape=jax.ShapeDtypeStruct((m, D_MODEL), F32),
        grid=(m // MERGE_TM,),
        in_specs=[
            pl.BlockSpec((MERGE_TM, D_RNN), row),
            pl.BlockSpec((MERGE_TM, D_RET_V), row),
            pl.BlockSpec((MERGE_TM, D_MODEL), lambda i: (i, gate_a_blk)),
            pl.BlockSpec((MERGE_TM, D_MODEL), lambda i: (i, gate_a_blk + 1)),
            pl.BlockSpec((MERGE_TM, D_MODEL), row),
            pl.BlockSpec((D_RNN, D_MODEL), const, pipeline_mode=pl.Buffered(1)),
            pl.BlockSpec((D_RET_V, D_MODEL), const, pipeline_mode=pl.Buffered(1)),
            pl.BlockSpec((D_MODEL, D_MODEL), const, pipeline_mode=pl.Buffered(1)),
            pl.BlockSpec((1, D_MODEL), const),
            pl.BlockSpec((1, D_MODEL), const),
        ],
        out_specs=pl.BlockSpec((MERGE_TM, D_MODEL), row),
        compiler_params=pltpu.CompilerParams(
            dimension_semantics=("parallel",), vmem_limit_bytes=VMEM_LIMIT),
        name="merge_ln",
    )(ya, yb, z, z, x1, wa, wb, wo, ln_g, ln_b)


def _rope_tables(pos):
    d = RET_DK
    inv_freq = ROPE_BASE ** (-jnp.arange(0, d, 2, dtype=F32) / d)
    ang = pos.astype(F32)[:, None] * inv_freq[None, :]
    cos, sin = jnp.cos(ang), jnp.sin(ang)
    return jnp.concatenate([cos, cos], axis=-1), jnp.concatenate([-sin, sin], axis=-1)


def _decay_tables(blk):
    log_g = jnp.log1p(-jnp.exp2(-5.0 - jnp.arange(RET_HEADS, dtype=F32)))
    idx = jnp.arange(blk, dtype=F32)
    diff = idx[:, None] - idx[None, :]
    dmask = jnp.where(diff >= 0, jnp.exp(log_g[:, None, None] * jnp.maximum(diff, 0.0)), 0.0)
    q_dec = jnp.exp(log_g[:, None] * (idx[None, :] + 1.0))
    k_dec = jnp.exp(log_g[:, None] * (blk - 1.0 - idx[None, :]))
    chunk_dec = jnp.exp(log_g * blk)
    lanes = (RET_HEADS, blk, RET_DK)
    return (dmask, jnp.broadcast_to(q_dec[:, :, None], lanes), jnp.broadcast_to(k_dec[:, :, None], lanes),
            jnp.broadcast_to(chunk_dec[:, None, None], (RET_HEADS, 1, RET_DV)))


def _block_diag(w):
    eye = jnp.eye(LRU_BLOCKS, dtype=w.dtype)
    return jnp.einsum("nkj,nm->nkmj", w, eye).reshape(D_RNN, D_RNN)


def kernel(x_prompt, x_sample, state_conv, state_lru, state_ret, ffn1_w_gate, ffn1_w_up, ffn1_w_down, ln1_g, ln1_b, w_in, conv_w, conv_b, rg_w, rg_b, ig_w, ig_b, lru_lambda, ret_gn_g, ret_gn_b, w_a_proj, w_b_proj, w_o, ln2_g, ln2_b, ffn2_w_gate, ffn2_w_up, ffn2_w_down, ln3_g, ln3_b):
    assert DEPTH == 1
    w_ri = jnp.concatenate([_block_diag(rg_w[0]), _block_diag(ig_w[0])], axis=1).astype(BF16)
    b_ri = jnp.concatenate([rg_b, ig_b], axis=1)
    params = (conv_w[0], conv_b, w_ri, b_ri, lru_lambda, ret_gn_g, ret_gn_b)
    rope = _rope_tables(jnp.arange(PAST_LEN + DEC_SEQ, dtype=jnp.int32))
    x_p = x_prompt.reshape(M_PROMPT, D_MODEL)
    x_s = x_sample.reshape(M_SAMPLE, D_MODEL)
    states = (state_conv[0], state_lru[0][:, None, :], state_ret[0])

    x1_s, *ffn1_w = _ffn_ln(x_s, ffn1_w_gate[0], ffn1_w_up[0], ffn1_w_down[0], ln1_g, ln1_b, export=True)
    x1_p = _ffn_ln(x_p, *ffn1_w, ln1_g, ln1_b)
    z_s, w_in_b = _in_proj(x1_s, w_in[0], export=True)
    z_p = _in_proj(x1_p, w_in_b)

    later_w = (ffn2_w_gate[0], ffn2_w_up[0], ffn2_w_down[0], w_a_proj[0], w_b_proj[0], w_o[0])
    ya_p, yb_p, conv_p, lru_p, ret_p, *later_b = _mixer(
        z_p, rope + _decay_tables(MIX_R), params, None, is_prompt=True, cast_srcs=later_w)
    reps = MIX_R // CHUNK
    dmask_s, qdec_s, kdec_s, cdec_s = _decay_tables(CHUNK)
    tables_s = tuple(jnp.tile(t[PAST_LEN:], (reps, 1)) for t in rope) + (
        dmask_s, jnp.tile(qdec_s, (1, reps, 1)), jnp.tile(kdec_s, (1, reps, 1)), cdec_s)
    ya_s, yb_s, conv_s, lru_s, ret_s = _mixer(z_s, tables_s, params, states, is_prompt=False)
    ffn2_w, merge_w = later_b[:3], later_b[3:]

    def tail(ya, yb, z, x1):
        x2 = _merge_ln(ya, yb, z, x1, *merge_w, ln2_g, ln2_b)
        return _ffn_ln(x2, *ffn2_w, ln3_g, ln3_b)

    y_p = tail(ya_p, yb_p, z_p, x1_p)
    y_s = tail(ya_s, yb_s, z_s, x1_s)

    return (y_p.reshape(BATCH, SEQ, D_MODEL), y_s.reshape(DEC_BATCH, DEC_SEQ, D_MODEL),
            conv_p[None], lru_p.reshape(1, BATCH, D_RNN), ret_p[None],
            conv_s[None], lru_s.reshape(1, DEC_BATCH, D_RNN), ret_s[None])
```

```python
import functools

import jax
import jax.numpy as jnp
from jax import lax
from jax.experimental import pallas as pl
from jax.experimental.pallas import tpu as pltpu

F32 = jnp.float32
BF16 = jnp.bfloat16

D_MODEL = 2048
BATCH = 4
SEQ = 2048
DEPTH = 1
DEC_BATCH = 32
DEC_SEQ = 64
PAST_LEN = 2048
CHUNK = 64
D_RNN = 1024
LRU_BLOCKS = 16
LRU_BLOCK = D_RNN // LRU_BLOCKS
CONV_W = 4
LRU_C = 8.0
RET_HEADS = 8
RET_DK = 128
RET_DV = 128
D_RET_K = RET_HEADS * RET_DK
D_RET_V = RET_HEADS * RET_DV
D_FF = 5632
DN_ALPHA = (2.0 * DEPTH) ** 0.25
LN_EPS = 1e-5
GN_EPS = 1e-5
ROPE_BASE = 10000.0
D_IN = 2 * D_RNN + 2 * D_RET_K + 2 * D_RET_V + 2 * D_MODEL

M_PROMPT = BATCH * SEQ
M_SAMPLE = DEC_BATCH * DEC_SEQ

V7X_VMEM_BYTES = 64 * 1024 * 1024
VMEM_LIMIT = V7X_VMEM_BYTES - 8 * 1024 * 1024
VMEM_LIMIT_BIG = V7X_VMEM_BYTES - 3 * 1024 * 1024

FFN_TM = 1024
FFN_SUB = 512
FFN_SUB_LAST = 256
FFN_TF = 512
FFN_TF_EXPORT = 256
LN_ROWS = 128
PROJ_TM = 1024
PROJ_TN = 1024
PROJ_TN_EXPORT = 512
MIX_R = 256
MERGE_TM = 256
MERGE_SUB = 128
CONV_PAD = 8


def _sigmoid(x):
    return 0.5 * jnp.tanh(0.5 * x) + 0.5


def _layer_norm_rows(y, g, b):
    mu = jnp.mean(y, axis=-1, keepdims=True)
    yc = y - mu
    var = jnp.mean(yc * yc, axis=-1, keepdims=True)
    return yc * lax.rsqrt(var + LN_EPS) * g + b


def _ffn_ln_kernel(x_ref, wg_ref, wu_ref, wd_ref, g_ref, b_ref, o_ref, *rest, export):
    f = pl.program_id(1)
    last = pl.num_programs(1) - 1
    xb_ref = rest[-1]

    def sub_tiles(size):
        return [slice(r * size, (r + 1) * size) for r in range(FFN_TM // size)]

    def weights():
        if not export:
            return wg_ref[...], wu_ref[...], wd_ref[...]
        ws = tuple(w[...].astype(BF16) for w in (wg_ref, wu_ref, wd_ref))
        for dst, w in zip(rest[:3], ws):
            dst[...] = w
        return ws

    def seed(rows):
        x = x_ref[rows, :]
        o_ref[rows, :] = (2.0 * DN_ALPHA) * x
        xb_ref[rows, :] = x.astype(BF16)

    def accumulate(rows, ws):
        wg, wu, wd = ws
        xb = xb_ref[rows, :]
        gate = jnp.dot(xb, wg, preferred_element_type=F32)
        up = jnp.dot(xb, wu, preferred_element_type=F32)
        h = (gate * jax.nn.sigmoid(gate) * up).astype(BF16)
        o_ref[rows, :] += jnp.dot(h, wd, preferred_element_type=F32)

    def finish(rows):
        for c in range(rows.start, rows.stop, LN_ROWS):
            chunk = slice(c, c + LN_ROWS)
            acc = o_ref[chunk, :]
            mu = jnp.mean(acc, axis=-1, keepdims=True)
            ac = acc - mu
            var = jnp.mean(ac * ac, axis=-1, keepdims=True)
            scale = 0.5 * lax.rsqrt(0.25 * var + LN_EPS)
            o_ref[chunk, :] = ac * scale * g_ref[...] + b_ref[...]

    @pl.when(f == 0)
    def _():
        ws = weights()
        for rows in sub_tiles(FFN_SUB):
            seed(rows)
            accumulate(rows, ws)

    @pl.when(jnp.logical_and(f > 0, f < last))
    def _():
        ws = weights()
        for rows in sub_tiles(FFN_SUB):
            accumulate(rows, ws)

    @pl.when(f == last)
    def _():
        ws = weights()
        for rows in sub_tiles(FFN_SUB_LAST):
            accumulate(rows, ws)
            finish(rows)


def _ffn_ln(x, wg, wu, wd, ln_g, ln_b, *, export=False):
    m = x.shape[0]
    n_tiles = m // FFN_TM
    tf = FFN_TF_EXPORT if export else FFN_TF
    grid = (n_tiles, D_FF // tf)
    out_shape = [jax.ShapeDtypeStruct((m, D_MODEL), F32)]
    out_specs = [pl.BlockSpec((FFN_TM, D_MODEL), lambda i, f: (i, 0))]
    x_mode = {}
    if export:
        out_shape += [jax.ShapeDtypeStruct((n_tiles,) + w.shape, BF16) for w in (wg, wu, wd)]
        out_specs += [
            pl.BlockSpec((None, D_MODEL, tf), lambda i, f: (i, 0, f)),
            pl.BlockSpec((None, D_MODEL, tf), lambda i, f: (i, 0, f)),
            pl.BlockSpec((None, tf, D_MODEL), lambda i, f: (i, f, 0)),
        ]
        x_mode = dict(pipeline_mode=pl.Buffered(1))

    def w_spec(w, block, idx):
        if w.ndim == 2:
            return pl.BlockSpec(block, idx)
        return pl.BlockSpec((None,) + block, lambda i, f: (0,) + idx(i, f))

    out = pl.pallas_call(
        functools.partial(_ffn_ln_kernel, export=export),
        out_shape=out_shape,
        grid=grid,
        in_specs=[
            pl.BlockSpec((FFN_TM, D_MODEL), lambda i, f: (i, 0), **x_mode),
            w_spec(wg, (D_MODEL, tf), lambda i, f: (0, f)),
            w_spec(wu, (D_MODEL, tf), lambda i, f: (0, f)),
            w_spec(wd, (tf, D_MODEL), lambda i, f: (f, 0)),
            pl.BlockSpec((1, D_MODEL), lambda i, f: (0, 0)),
            pl.BlockSpec((1, D_MODEL), lambda i, f: (0, 0)),
        ],
        out_specs=out_specs,
        scratch_shapes=[pltpu.VMEM((FFN_TM, D_MODEL), BF16)],
        compiler_params=pltpu.CompilerParams(
            dimension_semantics=("parallel", "arbitrary"), vmem_limit_bytes=VMEM_LIMIT_BIG),
        name="ffn_ln_export" if export else "ffn_ln",
    )(x, wg, wu, wd, ln_g, ln_b)
    return out if export else out[0]


def _in_proj_kernel(x_ref, w_ref, o_ref, *rest, export):
    xb_ref = rest[-1]

    @pl.when(pl.program_id(1) == 0)
    def _():
        xb_ref[...] = x_ref[...].astype(BF16)

    if export:
        w = w_ref[...].astype(BF16)
        rest[0][...] = w
    else:
        w = w_ref[...]
    o_ref[...] = jnp.dot(xb_ref[...], w, preferred_element_type=F32)


def _in_proj(x1, w_in, *, export=False):
    m = x1.shape[0]
    tm, tn = (m, PROJ_TN_EXPORT) if export else (PROJ_TM, PROJ_TN)
    assert not export or m == tm
    out_shape = [jax.ShapeDtypeStruct((m, D_IN), F32)]
    out_specs = [pl.BlockSpec((tm, tn), lambda i, j: (i, j))]
    x_mode = {}
    if export:
        out_shape.append(jax.ShapeDtypeStruct(w_in.shape, BF16))
        out_specs.append(pl.BlockSpec((D_MODEL, tn), lambda i, j: (0, j)))
        x_mode = dict(pipeline_mode=pl.Buffered(1))
    out = pl.pallas_call(
        functools.partial(_in_proj_kernel, export=export),
        out_shape=out_shape,
        grid=(m // tm, D_IN // tn),
        in_specs=[
            pl.BlockSpec((tm, D_MODEL), lambda i, j: (i, 0), **x_mode),
            pl.BlockSpec((D_MODEL, tn), lambda i, j: (0, j)),
        ],
        out_specs=out_specs,
        scratch_shapes=[pltpu.VMEM((tm, D_MODEL), BF16)],
        compiler_params=pltpu.CompilerParams(
            dimension_semantics=("parallel", "arbitrary"), vmem_limit_bytes=VMEM_LIMIT),
        name="in_proj_export" if export else "in_proj",
    )(x1, w_in)
    return out if export else out[0]


def _mixer_kernel(*refs, is_prompt, n_cast):
    n_chunks = MIX_R // CHUNK
    (xa_ref, ga_ref, q_ref, k_ref, v_ref, g_ref, cos_ref, sin_ref,
     dmask_ref, qdec_ref, kdec_ref, cdec_ref,
     convw_ref, convb_ref, wri_ref, bri_ref, lam_ref, gng_ref, gnb_ref) = refs[:19]
    refs = refs[19:]
    if is_prompt:
        conv0_ref = h0_ref = s0_ref = None
    else:
        conv0_ref, h0_ref, s0_ref = refs[:3]
        refs = refs[3:]
    cast_src, refs = refs[:n_cast], refs[n_cast:]
    ya_ref, yb_ref, convo_ref, ho_ref, so_ref = refs[:5]
    cast_dst, refs = refs[5:5 + n_cast], refs[5 + n_cast:]
    (xpad_ref, xc_ref, a_ref, u_ref, hs_ref,
     qb_ref, kb_ref, qd_ref, kd_ref, vb_ref, sc_ref, oh_ref) = refs

    for src, dst in zip(cast_src, cast_dst):
        dst[...] = src[...].astype(BF16)

    if is_prompt:
        @pl.when(pl.program_id(1) == 0)
        def _():
            convo_ref[...] = jnp.zeros_like(convo_ref)
            ho_ref[...] = jnp.zeros_like(ho_ref)
            so_ref[...] = jnp.zeros_like(so_ref)

    n_seg = 1 if is_prompt else n_chunks
    seg_len = MIX_R // n_seg
    conv_b = convb_ref[...]
    w0, w1, w2, w3 = (convw_ref[j:j + 1, :] for j in range(CONV_W))
    lo = CONV_PAD - (CONV_W - 1)
    for sg in range(n_seg):
        rows = slice(sg * seg_len, (sg + 1) * seg_len)
        xpad_ref[sg, lo:CONV_PAD, :] = convo_ref[0] if is_prompt else conv0_ref[sg]
        xpad_ref[sg, CONV_PAD:CONV_PAD + seg_len, :] = xa_ref[rows, :]
        acc = conv_b + w0 * xpad_ref[sg, lo:lo + seg_len, :]
        acc = acc + w1 * xpad_ref[sg, lo + 1:lo + 1 + seg_len, :]
        acc = acc + w2 * xpad_ref[sg, lo + 2:lo + 2 + seg_len, :]
        acc = acc + w3 * xpad_ref[sg, lo + 3:lo + 3 + seg_len, :]
        xc_ref[rows, :] = acc
        convo_ref[sg] = xpad_ref[sg, lo + seg_len:CONV_PAD + seg_len, :]

    xc = xc_ref[...]
    ri = jnp.dot(xc.astype(BF16), wri_ref[...], preferred_element_type=F32) + bri_ref[...]
    r = _sigmoid(ri[:, :D_RNN])
    i = _sigmoid(ri[:, D_RNN:])
    log_a = -LRU_C * r * jax.nn.softplus(-lam_ref[...])
    a = jnp.exp(log_a)
    a_ref[...] = a
    w = jnp.tanh(-log_a) * (1.0 + a * a)
    u_ref[...] = jnp.where(w > 0.0, w * lax.rsqrt(w), 0.0) * (i * xc)

    for sg in range(n_seg):
        h_init = ho_ref[0] if is_prompt else h0_ref[sg]

        def step(t, h, base=sg * seg_len):
            row = pl.ds(base + t, 1)
            h = a_ref[row, :] * h + u_ref[row, :]
            hs_ref[row, :] = h
            return h

        ho_ref[sg] = lax.fori_loop(0, seg_len, step, h_init, unroll=True)

    ya_ref[...] = (hs_ref[...] * jax.nn.gelu(ga_ref[...])).astype(BF16)

    k_scale = RET_DK ** -0.5

    blk = MIX_R if is_prompt else CHUNK
    head_cols = [slice(hd * RET_DK, (hd + 1) * RET_DK) for hd in range(RET_HEADS)]
    blk_rows = [slice(c * blk, (c + 1) * blk) for c in range(MIX_R // blk)]
    cos, sin = cos_ref[...], sin_ref[...]

    for hd, cols in enumerate(head_cols):
        qh, kh = q_ref[:, cols], k_ref[:, cols]
        qr = qh * cos + pltpu.roll(qh, RET_DK // 2, 1) * sin
        kr = (kh * cos + pltpu.roll(kh, RET_DK // 2, 1) * sin) * k_scale
        qb_ref[:, cols] = qr.astype(BF16)
        kb_ref[:, cols] = kr.astype(BF16)
        qd_ref[:, cols] = (qr * qdec_ref[hd]).astype(BF16)
        kd_ref[:, cols] = (kr * kdec_ref[hd]).astype(BF16)
        vb_ref[:, cols] = v_ref[:, cols].astype(BF16)

    for rows in blk_rows:
        for hd, cols in enumerate(head_cols):
            s = lax.dot_general(qb_ref[rows, cols], kb_ref[rows, cols], (((1,), (1,)), ((), ())),
                                preferred_element_type=F32) * dmask_ref[hd]
            sc_ref[hd, rows, :] = s.astype(BF16)

    for c, rows in enumerate(blk_rows):
        for hd, cols in enumerate(head_cols):
            state = so_ref[0, hd] if is_prompt else s0_ref[c, hd]
            vb = vb_ref[rows, cols]
            o = jnp.dot(sc_ref[hd, rows, :], vb, preferred_element_type=F32)
            o = o + jnp.dot(qd_ref[rows, cols], state.astype(BF16), preferred_element_type=F32)
            oh_ref[rows, cols] = o
            new_state = cdec_ref[hd] * state + lax.dot_general(
                kd_ref[rows, cols], vb, (((0,), (0,)), ((), ())), preferred_element_type=F32)
            if is_prompt:
                so_ref[0, hd] = new_state
            else:
                so_ref[c, hd] = new_state

    for cols in head_cols:
        o = oh_ref[:, cols]
        mu = jnp.mean(o, axis=-1, keepdims=True)
        oc = o - mu
        var = jnp.mean(oc * oc, axis=-1, keepdims=True)
        on = oc * lax.rsqrt(var + GN_EPS) * gng_ref[:, cols] + gnb_ref[:, cols]
        gh = g_ref[:, cols]
        yb_ref[:, cols] = (gh * _sigmoid(gh) * on).astype(BF16)


def _mixer(z, tables, params, states, *, is_prompt, cast_srcs=()):
    cos_t, sin_t, dmask, qdec, kdec, cdec = tables
    n_chunks = MIX_R // CHUNK
    if is_prompt:
        tiles_per_seq = SEQ // MIX_R
        grid = (BATCH, tiles_per_seq)
        n_seq = BATCH
        n_state = 1
        row_blk = lambda s, t: s * tiles_per_seq + t
        zspec = lambda col: pl.BlockSpec((MIX_R, D_RNN), lambda s, t, col=col: (row_blk(s, t), col))
        tab_spec = pl.BlockSpec((MIX_R, RET_DK), lambda s, t: (t, 0))
        state_idx = lambda s, t: s
        out_row = lambda s, t: (row_blk(s, t), 0)
        m_rows = M_PROMPT
        sem = ("parallel", "arbitrary")
    else:
        grid = (M_SAMPLE // MIX_R,)
        n_seq = DEC_BATCH
        n_state = n_chunks
        zspec = lambda col: pl.BlockSpec((MIX_R, D_RNN), lambda i, col=col: (i, col))
        tab_spec = pl.BlockSpec((MIX_R, RET_DK), lambda i: (0, 0))
        state_idx = lambda i: i
        out_row = lambda i: (i, 0)
        m_rows = M_SAMPLE
        sem = ("parallel",)

    def const_spec(shape):
        nd = len(shape)
        return pl.BlockSpec(shape, lambda *_: (0,) * nd)

    def state_spec(shape):
        nd = len(shape)
        return pl.BlockSpec((n_state,) + shape, lambda *g: (state_idx(*g),) + (0,) * nd)

    in_specs = [zspec(c) for c in range(6)] + [tab_spec, tab_spec] + [
        const_spec(dmask.shape), const_spec(qdec.shape), const_spec(kdec.shape), const_spec(cdec.shape)
    ] + [const_spec(p.shape) for p in params]
    args = [z] * 6 + [cos_t, sin_t, dmask, qdec, kdec, cdec] + list(params)
    if not is_prompt:
        in_specs += [state_spec((CONV_W - 1, D_RNN)), state_spec((1, D_RNN)),
                     state_spec((RET_HEADS, RET_DK, RET_DV))]
        args += list(states)

    n_steps = 1
    for g in grid:
        n_steps *= g
    step_idx = (lambda s, t: (s * grid[1] + t, 0)) if is_prompt else (lambda i: (i, 0))
    cast_specs = [pl.BlockSpec((w.shape[0] // n_steps, w.shape[1]), step_idx) for w in cast_srcs]
    in_specs += cast_specs
    args += list(cast_srcs)

    n_seg = 1 if is_prompt else n_chunks
    seg_len = MIX_R // n_seg
    out_shape = [
        jax.ShapeDtypeStruct((m_rows, D_RNN), BF16),
        jax.ShapeDtypeStruct((m_rows, D_RET_V), BF16),
        jax.ShapeDtypeStruct((n_seq, CONV_W - 1, D_RNN), F32),
        jax.ShapeDtypeStruct((n_seq, 1, D_RNN), F32),
        jax.ShapeDtypeStruct((n_seq, RET_HEADS, RET_DK, RET_DV), F32),
    ]
    out_specs = [
        pl.BlockSpec((MIX_R, D_RNN), out_row),
        pl.BlockSpec((MIX_R, D_RET_V), out_row),
        state_spec((CONV_W - 1, D_RNN)),
        state_spec((1, D_RNN)),
        state_spec((RET_HEADS, RET_DK, RET_DV)),
    ] + cast_specs
    out_shape += [jax.ShapeDtypeStruct(w.shape, BF16) for w in cast_srcs]
    scratch = [
        pltpu.VMEM((n_seg, CONV_PAD + seg_len, D_RNN), F32),
        pltpu.VMEM((MIX_R, D_RNN), F32),
        pltpu.VMEM((MIX_R, D_RNN), F32),
        pltpu.VMEM((MIX_R, D_RNN), F32),
        pltpu.VMEM((MIX_R, D_RNN), F32),
    ] + [pltpu.VMEM((MIX_R, D_RET_K), BF16)] * 5 + [
        pltpu.VMEM((RET_HEADS, MIX_R, MIX_R if is_prompt else CHUNK), BF16),
        pltpu.VMEM((MIX_R, D_RET_V), F32),
    ]
    return pl.pallas_call(
        functools.partial(_mixer_kernel, is_prompt=is_prompt, n_cast=len(cast_srcs)),
        out_shape=out_shape,
        grid=grid,
        in_specs=in_specs,
        out_specs=out_specs,
        scratch_shapes=scratch,
        compiler_params=pltpu.CompilerParams(dimension_semantics=sem, vmem_limit_bytes=VMEM_LIMIT),
        name="mixer_prompt" if is_prompt else "mixer_sample",
    )(*args)


def _merge_ln_kernel(ya_ref, yb_ref, ga_ref, gb_ref, x_ref, wa_ref, wb_ref, wo_ref, g_ref, b_ref, o_ref):
    for r in range(MERGE_TM // MERGE_SUB):
        rows = slice(r * MERGE_SUB, (r + 1) * MERGE_SUB)
        pa = jnp.dot(ya_ref[rows, :], wa_ref[...], preferred_element_type=F32)
        pb = jnp.dot(yb_ref[rows, :], wb_ref[...], preferred_element_type=F32)
        merged = jax.nn.sigmoid(ga_ref[rows, :]) * pa + jax.nn.sigmoid(gb_ref[rows, :]) * pb
        mo = jnp.dot(merged.astype(BF16), wo_ref[...], preferred_element_type=F32)
        o_ref[rows, :] = _layer_norm_rows(DN_ALPHA * x_ref[rows, :] + mo, g_ref[...], b_ref[...])


def _merge_ln(ya, yb, z, x1, wa, wb, wo, ln_g, ln_b):
    m = x1.shape[0]
    gate_a_blk = (2 * D_RNN + 2 * D_RET_K + 2 * D_RET_V) // D_MODEL
    row = lambda i: (i, 0)
    const = lambda i: (0, 0)
    return pl.pallas_call(
        _merge_ln_kernel,
        out_shape=jax.ShapeDtypeStruct((m, D_MODEL), F32),
        grid=(m // MERGE_TM,),
        in_specs=[
            pl.BlockSpec((MERGE_TM, D_RNN), row),
            pl.BlockSpec((MERGE_TM, D_RET_V), row),
            pl.BlockSpec((MERGE_TM, D_MODEL), lambda i: (i, gate_a_blk)),
            pl.BlockSpec((MERGE_TM, D_MODEL), lambda i: (i, gate_a_blk + 1)),
            pl.BlockSpec((MERGE_TM, D_MODEL), row),
            pl.BlockSpec((D_RNN, D_MODEL), const, pipeline_mode=pl.Buffered(1)),
            pl.BlockSpec((D_RET_V, D_MODEL), const, pipeline_mode=pl.Buffered(1)),
            pl.BlockSpec((D_MODEL, D_MODEL), const, pipeline_mode=pl.Buffered(1)),
            pl.BlockSpec((1, D_MODEL), const),
            pl.BlockSpec((1, D_MODEL), const),
        ],
        out_specs=pl.BlockSpec((MERGE_TM, D_MODEL), row),
        compiler_params=pltpu.CompilerParams(
            dimension_semantics=("parallel",), vmem_limit_bytes=VMEM_LIMIT),
        name="merge_ln",
    )(ya, yb, z, z, x1, wa, wb, wo, ln_g, ln_b)


def _rope_tables(pos):
    d = RET_DK
    inv_freq = ROPE_BASE ** (-jnp.arange(0, d, 2, dtype=F32) / d)
    ang = pos.astype(F32)[:, None] * inv_freq[None, :]
    cos, sin = jnp.cos(ang), jnp.sin(ang)
    return jnp.concatenate([cos, cos], axis=-1), jnp.concatenate([-sin, sin], axis=-1)


def _decay_tables(blk):
    log_g = jnp.log1p(-jnp.exp2(-5.0 - jnp.arange(RET_HEADS, dtype=F32)))
    idx = jnp.arange(blk, dtype=F32)
    diff = idx[:, None] - idx[None, :]
    dmask = jnp.where(diff >= 0, jnp.exp(log_g[:, None, None] * jnp.maximum(diff, 0.0)), 0.0)
    q_dec = jnp.exp(log_g[:, None] * (idx[None, :] + 1.0))
    k_dec = jnp.exp(log_g[:, None] * (blk - 1.0 - idx[None, :]))
    chunk_dec = jnp.exp(log_g * blk)
    lanes = (RET_HEADS, blk, RET_DK)
    return (dmask, jnp.broadcast_to(q_dec[:, :, None], lanes), jnp.broadcast_to(k_dec[:, :, None], lanes),
            jnp.broadcast_to(chunk_dec[:, None, None], (RET_HEADS, 1, RET_DV)))


def _block_diag(w):
    eye = jnp.eye(LRU_BLOCKS, dtype=w.dtype)
    return jnp.einsum("nkj,nm->nkmj", w, eye).reshape(D_RNN, D_RNN)


def kernel(x_prompt, x_sample, state_conv, state_lru, state_ret, ffn1_w_gate, ffn1_w_up, ffn1_w_down, ln1_g, ln1_b, w_in, conv_w, conv_b, rg_w, rg_b, ig_w, ig_b, lru_lambda, ret_gn_g, ret_gn_b, w_a_proj, w_b_proj, w_o, ln2_g, ln2_b, ffn2_w_gate, ffn2_w_up, ffn2_w_down, ln3_g, ln3_b):
    assert DEPTH == 1
    w_ri = jnp.concatenate([_block_diag(rg_w[0]), _block_diag(ig_w[0])], axis=1).astype(BF16)
    b_ri = jnp.concatenate([rg_b, ig_b], axis=1)
    params = (conv_w[0], conv_b, w_ri, b_ri, lru_lambda, ret_gn_g, ret_gn_b)
    rope = _rope_tables(jnp.arange(PAST_LEN + DEC_SEQ, dtype=jnp.int32))
    x_p = x_prompt.reshape(M_PROMPT, D_MODEL)
    x_s = x_sample.reshape(M_SAMPLE, D_MODEL)
    states = (state_conv[0], state_lru[0][:, None, :], state_ret[0])

    x1_s, *ffn1_w = _ffn_ln(x_s, ffn1_w_gate[0], ffn1_w_up[0], ffn1_w_down[0], ln1_g, ln1_b, export=True)
    x1_p = _ffn_ln(x_p, *ffn1_w, ln1_g, ln1_b)
    z_s, w_in_b = _in_proj(x1_s, w_in[0], export=True)
    z_p = _in_proj(x1_p, w_in_b)

    later_w = (ffn2_w_gate[0], ffn2_w_up[0], ffn2_w_down[0], w_a_proj[0], w_b_proj[0], w_o[0])
    ya_p, yb_p, conv_p, lru_p, ret_p, *later_b = _mixer(
        z_p, rope + _decay_tables(MIX_R), params, None, is_prompt=True, cast_srcs=later_w)
    reps = MIX_R // CHUNK
    dmask_s, qdec_s, kdec_s, cdec_s = _decay_tables(CHUNK)
    tables_s = tuple(jnp.tile(t[PAST_LEN:], (reps, 1)) for t in rope) + (
        dmask_s, jnp.tile(qdec_s, (1, reps, 1)), jnp.tile(kdec_s, (1, reps, 1)), cdec_s)
    ya_s, yb_s, conv_s, lru_s, ret_s = _mixer(z_s, tables_s, params, states, is_prompt=False)
    ffn2_w, merge_w = later_b[:3], later_b[3:]

    def tail(ya, yb, z, x1):
        x2 = _merge_ln(ya, yb, z, x1, *merge_w, ln2_g, ln2_b)
        return _ffn_ln(x2, *ffn2_w, ln3_g, ln3_b)

    y_p = tail(ya_p, yb_p, z_p, x1_p)
    y_s = tail(ya_s, yb_s, z_s, x1_s)

    return (y_p.reshape(BATCH, SEQ, D_MODEL), y_s.reshape(DEC_BATCH, DEC_SEQ, D_MODEL),
            conv_p[None], lru_p.reshape(1, BATCH, D_RNN), ret_p[None],
            conv_s[None], lru_s.reshape(1, DEC_BATCH, D_RNN), ret_s[None])
```

```python
import functools

import jax
import jax.numpy as jnp
from jax import lax
from jax.experimental import pallas as pl
from jax.experimental.pallas import tpu as pltpu

F32 = jnp.float32
BF16 = jnp.bfloat16

D_MODEL = 2048
BATCH = 4
SEQ = 2048
DEPTH = 1
DEC_BATCH = 32
DEC_SEQ = 64
PAST_LEN = 2048
CHUNK = 64
D_RNN = 1024
LRU_BLOCKS = 16
LRU_BLOCK = D_RNN // LRU_BLOCKS
CONV_W = 4
LRU_C = 8.0
RET_HEADS = 8
RET_DK = 128
RET_DV = 128
D_RET_K = RET_HEADS * RET_DK
D_RET_V = RET_HEADS * RET_DV
D_FF = 5632
DN_ALPHA = (2.0 * DEPTH) ** 0.25
LN_EPS = 1e-5
GN_EPS = 1e-5
ROPE_BASE = 10000.0
D_IN = 2 * D_RNN + 2 * D_RET_K + 2 * D_RET_V + 2 * D_MODEL

M_PROMPT = BATCH * SEQ
M_SAMPLE = DEC_BATCH * DEC_SEQ

V7X_VMEM_BYTES = 64 * 1024 * 1024
VMEM_LIMIT = V7X_VMEM_BYTES - 8 * 1024 * 1024
VMEM_LIMIT_BIG = V7X_VMEM_BYTES - 3 * 1024 * 1024

FFN_TM = 1024
FFN_SUB = 512
FFN_SUB_LAST = 256
FFN_TF = 512
FFN_TF_EXPORT = 256
LN_ROWS = 128
PROJ_TM = 1024
PROJ_TN = 1024
PROJ_TN_EXPORT = 512
MIX_R = 256
MERGE_TM = 256
MERGE_SUB = 128
GELU_GROUP = 1
SWISH_GROUP = 5
SIGMOID_GROUP0 = 6
CONV_PAD = 8


def _sigmoid(x):
    return 0.5 * jnp.tanh(0.5 * x) + 0.5


def _layer_norm_rows(y, g, b):
    mu = jnp.mean(y, axis=-1, keepdims=True)
    yc = y - mu
    var = jnp.mean(yc * yc, axis=-1, keepdims=True)
    return yc * lax.rsqrt(var + LN_EPS) * g + b


def _ffn_ln_kernel(x_ref, wg_ref, wu_ref, wd_ref, g_ref, b_ref, o_ref, *rest, export):
    f = pl.program_id(1)
    last = pl.num_programs(1) - 1
    xb_ref = rest[-1]

    def sub_tiles(size):
        return [slice(r * size, (r + 1) * size) for r in range(FFN_TM // size)]

    def weights():
        if not export:
            return wg_ref[...], wu_ref[...], wd_ref[...]
        ws = tuple(w[...].astype(BF16) for w in (wg_ref, wu_ref, wd_ref))
        for dst, w in zip(rest[:3], ws):
            dst[...] = w
        return ws

    def seed(rows):
        x = x_ref[rows, :]
        o_ref[rows, :] = (2.0 * DN_ALPHA) * x
        xb_ref[rows, :] = x.astype(BF16)

    def accumulate(rows, ws):
        wg, wu, wd = ws
        xb = xb_ref[rows, :]
        gate = jnp.dot(xb, wg, preferred_element_type=F32)
        up = jnp.dot(xb, wu, preferred_element_type=F32)
        h = (gate * jax.nn.sigmoid(gate) * up).astype(BF16)
        o_ref[rows, :] += jnp.dot(h, wd, preferred_element_type=F32)

    def finish(rows):
        for c in range(rows.start, rows.stop, LN_ROWS):
            chunk = slice(c, c + LN_ROWS)
            acc = o_ref[chunk, :]
            mu = jnp.mean(acc, axis=-1, keepdims=True)
            ac = acc - mu
            var = jnp.mean(ac * ac, axis=-1, keepdims=True)
            scale = 0.5 * lax.rsqrt(0.25 * var + LN_EPS)
            o_ref[chunk, :] = ac * scale * g_ref[...] + b_ref[...]

    @pl.when(f == 0)
    def _():
        ws = weights()
        for rows in sub_tiles(FFN_SUB):
            seed(rows)
            accumulate(rows, ws)

    @pl.when(jnp.logical_and(f > 0, f < last))
    def _():
        ws = weights()
        for rows in sub_tiles(FFN_SUB):
            accumulate(rows, ws)

    @pl.when(f == last)
    def _():
        ws = weights()
        for rows in sub_tiles(FFN_SUB_LAST):
            accumulate(rows, ws)
            finish(rows)


def _ffn_ln(x, wg, wu, wd, ln_g, ln_b, *, export=False):
    m = x.shape[0]
    n_tiles = m // FFN_TM
    tf = FFN_TF_EXPORT if export else FFN_TF
    grid = (n_tiles, D_FF // tf)
    out_shape = [jax.ShapeDtypeStruct((m, D_MODEL), F32)]
    out_specs = [pl.BlockSpec((FFN_TM, D_MODEL), lambda i, f: (i, 0))]
    x_mode = {}
    if export:
        out_shape += [jax.ShapeDtypeStruct((n_tiles,) + w.shape, BF16) for w in (wg, wu, wd)]
        out_specs += [
            pl.BlockSpec((None, D_MODEL, tf), lambda i, f: (i, 0, f)),
            pl.BlockSpec((None, D_MODEL, tf), lambda i, f: (i, 0, f)),
            pl.BlockSpec((None, tf, D_MODEL), lambda i, f: (i, f, 0)),
        ]
        x_mode = dict(pipeline_mode=pl.Buffered(1))

    def w_spec(w, block, idx):
        if w.ndim == 2:
            return pl.BlockSpec(block, idx)
        return pl.BlockSpec((None,) + block, lambda i, f: (0,) + idx(i, f))

    out = pl.pallas_call(
        functools.partial(_ffn_ln_kernel, export=export),
        out_shape=out_shape,
        grid=grid,
        in_specs=[
            pl.BlockSpec((FFN_TM, D_MODEL), lambda i, f: (i, 0), **x_mode),
            w_spec(wg, (D_MODEL, tf), lambda i, f: (0, f)),
            w_spec(wu, (D_MODEL, tf), lambda i, f: (0, f)),
            w_spec(wd, (tf, D_MODEL), lambda i, f: (f, 0)),
            pl.BlockSpec((1, D_MODEL), lambda i, f: (0, 0)),
            pl.BlockSpec((1, D_MODEL), lambda i, f: (0, 0)),
        ],
        out_specs=out_specs,
        scratch_shapes=[pltpu.VMEM((FFN_TM, D_MODEL), BF16)],
        compiler_params=pltpu.CompilerParams(
            dimension_semantics=("parallel", "arbitrary"), vmem_limit_bytes=VMEM_LIMIT_BIG),
        name="ffn_ln_export" if export else "ffn_ln",
    )(x, wg, wu, wd, ln_g, ln_b)
    return out if export else out[0]


def _in_proj_kernel(x_ref, w_ref, o_ref, *rest, export, blocks_per_group):
    xb_ref = rest[-1]
    j = pl.program_id(1)
    tm = o_ref.shape[0]

    @pl.when(j == 0)
    def _():
        xb_ref[...] = x_ref[...].astype(BF16)

    def emit(act):
        if export:
            w = w_ref[...].astype(BF16)
            rest[0][...] = w
        else:
            w = w_ref[...]
        for rows in [slice(r * tm // 4, (r + 1) * tm // 4) for r in range(4)]:
            o_ref[rows, :] = act(jnp.dot(xb_ref[rows, :], w, preferred_element_type=F32))

    group = lax.div(j, blocks_per_group)
    is_gelu, is_swish, is_sigmoid = group == GELU_GROUP, group == SWISH_GROUP, group >= SIGMOID_GROUP0
    pl.when(is_gelu)(lambda: emit(jax.nn.gelu))
    pl.when(is_swish)(lambda: emit(lambda t: t * _sigmoid(t)))
    pl.when(is_sigmoid)(lambda: emit(jax.nn.sigmoid))
    plain = jnp.logical_not(jnp.logical_or(jnp.logical_or(is_gelu, is_swish), is_sigmoid))
    pl.when(plain)(lambda: emit(lambda t: t))


def _in_proj(x1, w_in, *, export=False):
    m = x1.shape[0]
    tm, tn = (m, PROJ_TN_EXPORT) if export else (PROJ_TM, PROJ_TN)
    assert not export or m == tm
    out_shape = [jax.ShapeDtypeStruct((m, D_IN), F32)]
    out_specs = [pl.BlockSpec((tm, tn), lambda i, j: (i, j))]
    x_mode = {}
    if export:
        out_shape.append(jax.ShapeDtypeStruct(w_in.shape, BF16))
        out_specs.append(pl.BlockSpec((D_MODEL, tn), lambda i, j: (0, j)))
        x_mode = dict(pipeline_mode=pl.Buffered(1))
    out = pl.pallas_call(
        functools.partial(_in_proj_kernel, export=export, blocks_per_group=D_RNN // tn),
        out_shape=out_shape,
        grid=(m // tm, D_IN // tn),
        in_specs=[
            pl.BlockSpec((tm, D_MODEL), lambda i, j: (i, 0), **x_mode),
            pl.BlockSpec((D_MODEL, tn), lambda i, j: (0, j)),
        ],
        out_specs=out_specs,
        scratch_shapes=[pltpu.VMEM((tm, D_MODEL), BF16)],
        compiler_params=pltpu.CompilerParams(
            dimension_semantics=("parallel", "arbitrary"), vmem_limit_bytes=VMEM_LIMIT),
        name="in_proj_export" if export else "in_proj",
    )(x1, w_in)
    return out if export else out[0]


def _mixer_kernel(*refs, is_prompt, n_cast):
    n_chunks = MIX_R // CHUNK
    (xa_ref, ga_ref, q_ref, k_ref, v_ref, g_ref, cos_ref, sin_ref,
     dmask_ref, qdec_ref, kdec_ref, cdec_ref,
     convw_ref, convb_ref, wri_ref, bri_ref, lam_ref, gng_ref, gnb_ref) = refs[:19]
    refs = refs[19:]
    if is_prompt:
        conv0_ref = h0_ref = s0_ref = None
    else:
        conv0_ref, h0_ref, s0_ref = refs[:3]
        refs = refs[3:]
    cast_src, refs = refs[:n_cast], refs[n_cast:]
    ya_ref, yb_ref, convo_ref, ho_ref, so_ref = refs[:5]
    cast_dst, refs = refs[5:5 + n_cast], refs[5 + n_cast:]
    (xpad_ref, xc_ref, a_ref, u_ref, hs_ref,
     qb_ref, kb_ref, qd_ref, kd_ref, vb_ref, sc_ref, oh_ref) = refs

    for src, dst in zip(cast_src, cast_dst):
        dst[...] = src[...].astype(BF16)

    if is_prompt:
        @pl.when(pl.program_id(1) == 0)
        def _():
            convo_ref[...] = jnp.zeros_like(convo_ref)
            ho_ref[...] = jnp.zeros_like(ho_ref)
            so_ref[...] = jnp.zeros_like(so_ref)

    n_seg = 1 if is_prompt else n_chunks
    seg_len = MIX_R // n_seg
    conv_b = convb_ref[...]
    w0, w1, w2, w3 = (convw_ref[j:j + 1, :] for j in range(CONV_W))
    lo = CONV_PAD - (CONV_W - 1)
    for sg in range(n_seg):
        rows = slice(sg * seg_len, (sg + 1) * seg_len)
        xpad_ref[sg, lo:CONV_PAD, :] = convo_ref[0] if is_prompt else conv0_ref[sg]
        xpad_ref[sg, CONV_PAD:CONV_PAD + seg_len, :] = xa_ref[rows, :]
        acc = conv_b + w0 * xpad_ref[sg, lo:lo + seg_len, :]
        acc = acc + w1 * xpad_ref[sg, lo + 1:lo + 1 + seg_len, :]
        acc = acc + w2 * xpad_ref[sg, lo + 2:lo + 2 + seg_len, :]
        acc = acc + w3 * xpad_ref[sg, lo + 3:lo + 3 + seg_len, :]
        xc_ref[rows, :] = acc
        convo_ref[sg] = xpad_ref[sg, lo + seg_len:CONV_PAD + seg_len, :]

    xc = xc_ref[...]
    ri = jnp.dot(xc.astype(BF16), wri_ref[...], preferred_element_type=F32) + bri_ref[...]
    r = _sigmoid(ri[:, :D_RNN])
    i = _sigmoid(ri[:, D_RNN:])
    log_a = -LRU_C * r * jax.nn.softplus(-lam_ref[...])
    a = jnp.exp(log_a)
    a_ref[...] = a
    w = jnp.tanh(-log_a) * (1.0 + a * a)
    u_ref[...] = jnp.where(w > 0.0, w * lax.rsqrt(w), 0.0) * (i * xc)

    for sg in range(n_seg):
        h_init = ho_ref[0] if is_prompt else h0_ref[sg]

        def step(t, h, base=sg * seg_len):
            row = pl.ds(base + t, 1)
            h = a_ref[row, :] * h + u_ref[row, :]
            hs_ref[row, :] = h
            return h

        ho_ref[sg] = lax.fori_loop(0, seg_len, step, h_init, unroll=True)

    ya_ref[...] = (hs_ref[...] * ga_ref[...]).astype(BF16)

    k_scale = RET_DK ** -0.5

    blk = MIX_R if is_prompt else CHUNK
    head_cols = [slice(hd * RET_DK, (hd + 1) * RET_DK) for hd in range(RET_HEADS)]
    blk_rows = [slice(c * blk, (c + 1) * blk) for c in range(MIX_R // blk)]
    cos, sin = cos_ref[...], sin_ref[...]

    for hd, cols in enumerate(head_cols):
        qh, kh = q_ref[:, cols], k_ref[:, cols]
        qr = qh * cos + pltpu.roll(qh, RET_DK // 2, 1) * sin
        kr = (kh * cos + pltpu.roll(kh, RET_DK // 2, 1) * sin) * k_scale
        qb_ref[:, cols] = qr.astype(BF16)
        kb_ref[:, cols] = kr.astype(BF16)
        qd_ref[:, cols] = (qr * qdec_ref[hd]).astype(BF16)
        kd_ref[:, cols] = (kr * kdec_ref[hd]).astype(BF16)
        vb_ref[:, cols] = v_ref[:, cols].astype(BF16)

    for rows in blk_rows:
        for hd, cols in enumerate(head_cols):
            s = lax.dot_general(qb_ref[rows, cols], kb_ref[rows, cols], (((1,), (1,)), ((), ())),
                                preferred_element_type=F32) * dmask_ref[hd]
            sc_ref[hd, rows, :] = s.astype(BF16)

    for c, rows in enumerate(blk_rows):
        for hd, cols in enumerate(head_cols):
            state = so_ref[0, hd] if is_prompt else s0_ref[c, hd]
            vb = vb_ref[rows, cols]
            o = jnp.dot(sc_ref[hd, rows, :], vb, preferred_element_type=F32)
            o = o + jnp.dot(qd_ref[rows, cols], state.astype(BF16), preferred_element_type=F32)
            oh_ref[rows, cols] = o
            new_state = cdec_ref[hd] * state + lax.dot_general(
                kd_ref[rows, cols], vb, (((0,), (0,)), ((), ())), preferred_element_type=F32)
            if is_prompt:
                so_ref[0, hd] = new_state
            else:
                so_ref[c, hd] = new_state

    for cols in head_cols:
        o = oh_ref[:, cols]
        mu = jnp.mean(o, axis=-1, keepdims=True)
        oc = o - mu
        var = jnp.mean(oc * oc, axis=-1, keepdims=True)
        on = oc * lax.rsqrt(var + GN_EPS) * gng_ref[:, cols] + gnb_ref[:, cols]
        yb_ref[:, cols] = (g_ref[:, cols] * on).astype(BF16)


def _mixer(z, tables, params, states, *, is_prompt, cast_srcs=()):
    cos_t, sin_t, dmask, qdec, kdec, cdec = tables
    n_chunks = MIX_R // CHUNK
    if is_prompt:
        tiles_per_seq = SEQ // MIX_R
        grid = (BATCH, tiles_per_seq)
        n_seq = BATCH
        n_state = 1
        row_blk = lambda s, t: s * tiles_per_seq + t
        zspec = lambda col: pl.BlockSpec((MIX_R, D_RNN), lambda s, t, col=col: (row_blk(s, t), col))
        tab_spec = pl.BlockSpec((MIX_R, RET_DK), lambda s, t: (t, 0))
        state_idx = lambda s, t: s
        out_row = lambda s, t: (row_blk(s, t), 0)
        m_rows = M_PROMPT
        sem = ("parallel", "arbitrary")
    else:
        grid = (M_SAMPLE // MIX_R,)
        n_seq = DEC_BATCH
        n_state = n_chunks
        zspec = lambda col: pl.BlockSpec((MIX_R, D_RNN), lambda i, col=col: (i, col))
        tab_spec = pl.BlockSpec((MIX_R, RET_DK), lambda i: (0, 0))
        state_idx = lambda i: i
        out_row = lambda i: (i, 0)
        m_rows = M_SAMPLE
        sem = ("parallel",)

    def const_spec(shape):
        nd = len(shape)
        return pl.BlockSpec(shape, lambda *_: (0,) * nd)

    def state_spec(shape):
        nd = len(shape)
        return pl.BlockSpec((n_state,) + shape, lambda *g: (state_idx(*g),) + (0,) * nd)

    in_specs = [zspec(c) for c in range(6)] + [tab_spec, tab_spec] + [
        const_spec(dmask.shape), const_spec(qdec.shape), const_spec(kdec.shape), const_spec(cdec.shape)
    ] + [const_spec(p.shape) for p in params]
    args = [z] * 6 + [cos_t, sin_t, dmask, qdec, kdec, cdec] + list(params)
    if not is_prompt:
        in_specs += [state_spec((CONV_W - 1, D_RNN)), state_spec((1, D_RNN)),
                     state_spec((RET_HEADS, RET_DK, RET_DV))]
        args += list(states)

    n_steps = 1
    for g in grid:
        n_steps *= g
    step_idx = (lambda s, t: (s * grid[1] + t, 0)) if is_prompt else (lambda i: (i, 0))
    cast_specs = [pl.BlockSpec((w.shape[0] // n_steps, w.shape[1]), step_idx) for w in cast_srcs]
    in_specs += cast_specs
    args += list(cast_srcs)

    n_seg = 1 if is_prompt else n_chunks
    seg_len = MIX_R // n_seg
    out_shape = [
        jax.ShapeDtypeStruct((m_rows, D_RNN), BF16),
        jax.ShapeDtypeStruct((m_rows, D_RET_V), BF16),
        jax.ShapeDtypeStruct((n_seq, CONV_W - 1, D_RNN), F32),
        jax.ShapeDtypeStruct((n_seq, 1, D_RNN), F32),
        jax.ShapeDtypeStruct((n_seq, RET_HEADS, RET_DK, RET_DV), F32),
    ]
    out_specs = [
        pl.BlockSpec((MIX_R, D_RNN), out_row),
        pl.BlockSpec((MIX_R, D_RET_V), out_row),
        state_spec((CONV_W - 1, D_RNN)),
        state_spec((1, D_RNN)),
        state_spec((RET_HEADS, RET_DK, RET_DV)),
    ] + cast_specs
    out_shape += [jax.ShapeDtypeStruct(w.shape, BF16) for w in cast_srcs]
    scratch = [
        pltpu.VMEM((n_seg, CONV_PAD + seg_len, D_RNN), F32),
        pltpu.VMEM((MIX_R, D_RNN), F32),
        pltpu.VMEM((MIX_R, D_RNN), F32),
        pltpu.VMEM((MIX_R, D_RNN), F32),
        pltpu.VMEM((MIX_R, D_RNN), F32),
    ] + [pltpu.VMEM((MIX_R, D_RET_K), BF16)] * 5 + [
        pltpu.VMEM((RET_HEADS, MIX_R, MIX_R if is_prompt else CHUNK), BF16),
        pltpu.VMEM((MIX_R, D_RET_V), F32),
    ]
    return pl.pallas_call(
        functools.partial(_mixer_kernel, is_prompt=is_prompt, n_cast=len(cast_srcs)),
        out_shape=out_shape,
        grid=grid,
        in_specs=in_specs,
        out_specs=out_specs,
        scratch_shapes=scratch,
        compiler_params=pltpu.CompilerParams(dimension_semantics=sem, vmem_limit_bytes=VMEM_LIMIT),
        name="mixer_prompt" if is_prompt else "mixer_sample",
    )(*args)


def _merge_ln_kernel(ya_ref, yb_ref, ga_ref, gb_ref, x_ref, wa_ref, wb_ref, wo_ref, g_ref, b_ref, o_ref):
    for r in range(MERGE_TM // MERGE_SUB):
        rows = slice(r * MERGE_SUB, (r + 1) * MERGE_SUB)
        pa = jnp.dot(ya_ref[rows, :], wa_ref[...], preferred_element_type=F32)
        pb = jnp.dot(yb_ref[rows, :], wb_ref[...], preferred_element_type=F32)
        merged = ga_ref[rows, :] * pa + gb_ref[rows, :] * pb
        mo = jnp.dot(merged.astype(BF16), wo_ref[...], preferred_element_type=F32)
        o_ref[rows, :] = _layer_norm_rows(DN_ALPHA * x_ref[rows, :] + mo, g_ref[...], b_ref[...])


def _merge_ln(ya, yb, z, x1, wa, wb, wo, ln_g, ln_b):
    m = x1.shape[0]
    gate_a_blk = (2 * D_RNN + 2 * D_RET_K + 2 * D_RET_V) // D_MODEL
    row = lambda i: (i, 0)
    const = lambda i: (0, 0)
    return pl.pallas_call(
        _merge_ln_kernel,
        out_shape=jax.ShapeDtypeStruct((m, D_MODEL), F32),
        grid=(m // MERGE_TM,),
        in_specs=[
            pl.BlockSpec((MERGE_TM, D_RNN), row),
            pl.BlockSpec((MERGE_TM, D_RET_V), row),
            pl.BlockSpec((MERGE_TM, D_MODEL), lambda i: (i, gate_a_blk)),
            pl.BlockSpec((MERGE_TM, D_MODEL), lambda i: (i, gate_a_blk + 1)),
            pl.BlockSpec((MERGE_TM, D_MODEL), row),
            pl.BlockSpec((D_RNN, D_MODEL), const, pipeline_mode=pl.Buffered(1)),
            pl.BlockSpec((D_RET_V, D_MODEL), const, pipeline_mode=pl.Buffered(1)),
            pl.BlockSpec((D_MODEL, D_MODEL), const, pipeline_mode=pl.Buffered(1)),
            pl.BlockSpec((1, D_MODEL), const),
            pl.BlockSpec((1, D_MODEL), const),
        ],
        out_specs=pl.BlockSpec((MERGE_TM, D_MODEL), row),
        compiler_params=pltpu.CompilerParams(
            dimension_semantics=("parallel",), vmem_limit_bytes=VMEM_LIMIT),
        name="merge_ln",
    )(ya, yb, z, z, x1, wa, wb, wo, ln_g, ln_b)


def _rope_tables(pos):
    d = RET_DK
    inv_freq = ROPE_BASE ** (-jnp.arange(0, d, 2, dtype=F32) / d)
    ang = pos.astype(F32)[:, None] * inv_freq[None, :]
    cos, sin = jnp.cos(ang), jnp.sin(ang)
    return jnp.concatenate([cos, cos], axis=-1), jnp.concatenate([-sin, sin], axis=-1)


def _decay_tables(blk):
    log_g = jnp.log1p(-jnp.exp2(-5.0 - jnp.arange(RET_HEADS, dtype=F32)))
    idx = jnp.arange(blk, dtype=F32)
    diff = idx[:, None] - idx[None, :]
    dmask = jnp.where(diff >= 0, jnp.exp(log_g[:, None, None] * jnp.maximum(diff, 0.0)), 0.0)
    q_dec = jnp.exp(log_g[:, None] * (idx[None, :] + 1.0))
    k_dec = jnp.exp(log_g[:, None] * (blk - 1.0 - idx[None, :]))
    chunk_dec = jnp.exp(log_g * blk)
    lanes = (RET_HEADS, blk, RET_DK)
    return (dmask, jnp.broadcast_to(q_dec[:, :, None], lanes), jnp.broadcast_to(k_dec[:, :, None], lanes),
            jnp.broadcast_to(chunk_dec[:, None, None], (RET_HEADS, 1, RET_DV)))


def _block_diag(w):
    tiled = jnp.tile(w.reshape(D_RNN, LRU_BLOCK), (1, LRU_BLOCKS))
    row_blk = lax.broadcasted_iota(jnp.int32, (D_RNN, D_RNN), 0) // LRU_BLOCK
    col_blk = lax.broadcasted_iota(jnp.int32, (D_RNN, D_RNN), 1) // LRU_BLOCK
    return jnp.where(row_blk == col_blk, tiled, 0.0)


def kernel(x_prompt, x_sample, state_conv, state_lru, state_ret, ffn1_w_gate, ffn1_w_up, ffn1_w_down, ln1_g, ln1_b, w_in, conv_w, conv_b, rg_w, rg_b, ig_w, ig_b, lru_lambda, ret_gn_g, ret_gn_b, w_a_proj, w_b_proj, w_o, ln2_g, ln2_b, ffn2_w_gate, ffn2_w_up, ffn2_w_down, ln3_g, ln3_b):
    assert DEPTH == 1
    w_ri = jnp.concatenate([_block_diag(rg_w[0]), _block_diag(ig_w[0])], axis=1).astype(BF16)
    b_ri = jnp.concatenate([rg_b, ig_b], axis=1)
    params = (conv_w[0], conv_b, w_ri, b_ri, lru_lambda, ret_gn_g, ret_gn_b)
    rope = _rope_tables(jnp.arange(PAST_LEN + DEC_SEQ, dtype=jnp.int32))
    x_p = x_prompt.reshape(M_PROMPT, D_MODEL)
    x_s = x_sample.reshape(M_SAMPLE, D_MODEL)
    states = (state_conv[0], state_lru[0][:, None, :], state_ret[0])

    x1_s, *ffn1_w = _ffn_ln(x_s, ffn1_w_gate[0], ffn1_w_up[0], ffn1_w_down[0], ln1_g, ln1_b, export=True)
    x1_p = _ffn_ln(x_p, *ffn1_w, ln1_g, ln1_b)
    z_s, w_in_b = _in_proj(x1_s, w_in[0], export=True)
    z_p = _in_proj(x1_p, w_in_b)

    later_w = (ffn2_w_gate[0], ffn2_w_up[0], ffn2_w_down[0], w_a_proj[0], w_b_proj[0], w_o[0])
    ya_p, yb_p, conv_p, lru_p, ret_p, *later_b = _mixer(
        z_p, rope + _decay_tables(MIX_R), params, None, is_prompt=True, cast_srcs=later_w)
    reps = MIX_R // CHUNK
    dmask_s, qdec_s, kdec_s, cdec_s = _decay_tables(CHUNK)
    tables_s = tuple(jnp.tile(t[PAST_LEN:], (reps, 1)) for t in rope) + (
        dmask_s, jnp.tile(qdec_s, (1, reps, 1)), jnp.tile(kdec_s, (1, reps, 1)), cdec_s)
    ya_s, yb_s, conv_s, lru_s, ret_s = _mixer(z_s, tables_s, params, states, is_prompt=False)
    ffn2_w, merge_w = later_b[:3], later_b[3:]

    def tail(ya, yb, z, x1):
        x2 = _merge_ln(ya, yb, z, x1, *merge_w, ln2_g, ln2_b)
        return _ffn_ln(x2, *ffn2_w, ln3_g, ln3_b)

    y_p = tail(ya_p, yb_p, z_p, x1_p)
    y_s = tail(ya_s, yb_s, z_s, x1_s)

    return (y_p.reshape(BATCH, SEQ, D_MODEL), y_s.reshape(DEC_BATCH, DEC_SEQ, D_MODEL),
            conv_p[None], lru_p.reshape(1, BATCH, D_RNN), ret_p[None],
            conv_s[None], lru_s.reshape(1, DEC_BATCH, D_RNN), ret_s[None])
```

```python
import functools

import jax
import jax.numpy as jnp
from jax import lax
from jax.experimental import pallas as pl
from jax.experimental.pallas import tpu as pltpu

F32 = jnp.float32
BF16 = jnp.bfloat16

D_MODEL = 2048
BATCH = 4
SEQ = 2048
DEPTH = 1
DEC_BATCH = 32
DEC_SEQ = 64
PAST_LEN = 2048
CHUNK = 64
D_RNN = 1024
LRU_BLOCKS = 16
LRU_BLOCK = D_RNN // LRU_BLOCKS
CONV_W = 4
LRU_C = 8.0
RET_HEADS = 8
RET_DK = 128
RET_DV = 128
D_RET_K = RET_HEADS * RET_DK
D_RET_V = RET_HEADS * RET_DV
D_FF = 5632
DN_ALPHA = (2.0 * DEPTH) ** 0.25
LN_EPS = 1e-5
GN_EPS = 1e-5
ROPE_BASE = 10000.0
D_IN = 2 * D_RNN + 2 * D_RET_K + 2 * D_RET_V + 2 * D_MODEL

M_PROMPT = BATCH * SEQ
M_SAMPLE = DEC_BATCH * DEC_SEQ

V7X_VMEM_BYTES = 64 * 1024 * 1024
VMEM_LIMIT = V7X_VMEM_BYTES - 8 * 1024 * 1024
VMEM_LIMIT_BIG = V7X_VMEM_BYTES - 3 * 1024 * 1024

FFN_TM = 1024
FFN_SUB = 512
FFN_SUB_LAST = 256
FFN_TF = 512
FFN_TF_EXPORT = 256
LN_ROWS = 128
PROJ_TM = 1024
PROJ_TN = 1024
PROJ_TN_EXPORT = 512
MIX_R = 256
MERGE_TM = 512
MERGE_SUB = 128
GELU_GROUP = 1
SWISH_GROUP = 5
SIGMOID_GROUP0 = 6
CONV_PAD = 8


def _sigmoid(x):
    return 0.5 * jnp.tanh(0.5 * x) + 0.5


def _layer_norm_rows(y, g, b):
    mu = jnp.mean(y, axis=-1, keepdims=True)
    yc = y - mu
    var = jnp.mean(yc * yc, axis=-1, keepdims=True)
    return yc * lax.rsqrt(var + LN_EPS) * g + b


def _ffn_ln_kernel(x_ref, wg_ref, wu_ref, wd_ref, g_ref, b_ref, o_ref, *rest, export):
    f = pl.program_id(1)
    last = pl.num_programs(1) - 1
    xb_ref = rest[-1]

    def sub_tiles(size):
        return [slice(r * size, (r + 1) * size) for r in range(FFN_TM // size)]

    def weights():
        if not export:
            return wg_ref[...], wu_ref[...], wd_ref[...]
        ws = tuple(w[...].astype(BF16) for w in (wg_ref, wu_ref, wd_ref))
        for dst, w in zip(rest[:3], ws):
            dst[...] = w
        return ws

    def seed(rows):
        x = x_ref[rows, :]
        o_ref[rows, :] = (2.0 * DN_ALPHA) * x
        xb_ref[rows, :] = x.astype(BF16)

    def accumulate(rows, ws):
        wg, wu, wd = ws
        xb = xb_ref[rows, :]
        gate = jnp.dot(xb, wg, preferred_element_type=F32)
        up = jnp.dot(xb, wu, preferred_element_type=F32)
        h = (gate * jax.nn.sigmoid(gate) * up).astype(BF16)
        o_ref[rows, :] += jnp.dot(h, wd, preferred_element_type=F32)

    def finish(rows):
        for c in range(rows.start, rows.stop, LN_ROWS):
            chunk = slice(c, c + LN_ROWS)
            acc = o_ref[chunk, :]
            mu = jnp.mean(acc, axis=-1, keepdims=True)
            ac = acc - mu
            var = jnp.mean(ac * ac, axis=-1, keepdims=True)
            scale = 0.5 * lax.rsqrt(0.25 * var + LN_EPS)
            o_ref[chunk, :] = ac * scale * g_ref[...] + b_ref[...]

    @pl.when(f == 0)
    def _():
        ws = weights()
        for rows in sub_tiles(FFN_SUB):
            seed(rows)
            accumulate(rows, ws)

    @pl.when(jnp.logical_and(f > 0, f < last))
    def _():
        ws = weights()
        for rows in sub_tiles(FFN_SUB):
            accumulate(rows, ws)

    @pl.when(f == last)
    def _():
        ws = weights()
        for rows in sub_tiles(FFN_SUB_LAST):
            accumulate(rows, ws)
            finish(rows)


def _ffn_ln(x, wg, wu, wd, ln_g, ln_b, *, export=False):
    m = x.shape[0]
    n_tiles = m // FFN_TM
    tf = FFN_TF_EXPORT if export else FFN_TF
    grid = (n_tiles, D_FF // tf)
    out_shape = [jax.ShapeDtypeStruct((m, D_MODEL), F32)]
    out_specs = [pl.BlockSpec((FFN_TM, D_MODEL), lambda i, f: (i, 0))]
    x_mode = {}
    if export:
        out_shape += [jax.ShapeDtypeStruct((n_tiles,) + w.shape, BF16) for w in (wg, wu, wd)]
        out_specs += [
            pl.BlockSpec((None, D_MODEL, tf), lambda i, f: (i, 0, f)),
            pl.BlockSpec((None, D_MODEL, tf), lambda i, f: (i, 0, f)),
            pl.BlockSpec((None, tf, D_MODEL), lambda i, f: (i, f, 0)),
        ]
        x_mode = dict(pipeline_mode=pl.Buffered(1))

    def w_spec(w, block, idx):
        if w.ndim == 2:
            return pl.BlockSpec(block, idx)
        return pl.BlockSpec((None,) + block, lambda i, f: (0,) + idx(i, f))

    out = pl.pallas_call(
        functools.partial(_ffn_ln_kernel, export=export),
        out_shape=out_shape,
        grid=grid,
        in_specs=[
            pl.BlockSpec((FFN_TM, D_MODEL), lambda i, f: (i, 0), **x_mode),
            w_spec(wg, (D_MODEL, tf), lambda i, f: (0, f)),
            w_spec(wu, (D_MODEL, tf), lambda i, f: (0, f)),
            w_spec(wd, (tf, D_MODEL), lambda i, f: (f, 0)),
            pl.BlockSpec((1, D_MODEL), lambda i, f: (0, 0)),
            pl.BlockSpec((1, D_MODEL), lambda i, f: (0, 0)),
        ],
        out_specs=out_specs,
        scratch_shapes=[pltpu.VMEM((FFN_TM, D_MODEL), BF16)],
        compiler_params=pltpu.CompilerParams(
            dimension_semantics=("parallel", "arbitrary"), vmem_limit_bytes=VMEM_LIMIT_BIG),
        name="ffn_ln_export" if export else "ffn_ln",
    )(x, wg, wu, wd, ln_g, ln_b)
    return out if export else out[0]


def _in_proj_kernel(x_ref, w_ref, o_ref, *rest, export, blocks_per_group):
    xb_ref = rest[-1]
    j = pl.program_id(1)
    tm = o_ref.shape[0]

    @pl.when(j == 0)
    def _():
        xb_ref[...] = x_ref[...].astype(BF16)

    def emit(act):
        if export:
            w = w_ref[...].astype(BF16)
            rest[0][...] = w
        else:
            w = w_ref[...]
        for rows in [slice(r * tm // 4, (r + 1) * tm // 4) for r in range(4)]:
            o_ref[rows, :] = act(jnp.dot(xb_ref[rows, :], w, preferred_element_type=F32))

    group = lax.div(j, blocks_per_group)
    is_gelu, is_swish, is_sigmoid = group == GELU_GROUP, group == SWISH_GROUP, group >= SIGMOID_GROUP0
    pl.when(is_gelu)(lambda: emit(jax.nn.gelu))
    pl.when(is_swish)(lambda: emit(lambda t: t * _sigmoid(t)))
    pl.when(is_sigmoid)(lambda: emit(jax.nn.sigmoid))
    plain = jnp.logical_not(jnp.logical_or(jnp.logical_or(is_gelu, is_swish), is_sigmoid))
    pl.when(plain)(lambda: emit(lambda t: t))


def _in_proj(x1, w_in, *, export=False):
    m = x1.shape[0]
    tm, tn = (m, PROJ_TN_EXPORT) if export else (PROJ_TM, PROJ_TN)
    assert not export or m == tm
    out_shape = [jax.ShapeDtypeStruct((m, D_IN), F32)]
    out_specs = [pl.BlockSpec((tm, tn), lambda i, j: (i, j))]
    x_mode = {}
    if export:
        out_shape.append(jax.ShapeDtypeStruct(w_in.shape, BF16))
        out_specs.append(pl.BlockSpec((D_MODEL, tn), lambda i, j: (0, j)))
        x_mode = dict(pipeline_mode=pl.Buffered(1))
    out = pl.pallas_call(
        functools.partial(_in_proj_kernel, export=export, blocks_per_group=D_RNN // tn),
        out_shape=out_shape,
        grid=(m // tm, D_IN // tn),
        in_specs=[
            pl.BlockSpec((tm, D_MODEL), lambda i, j: (i, 0), **x_mode),
            pl.BlockSpec((D_MODEL, tn), lambda i, j: (0, j)),
        ],
        out_specs=out_specs,
        scratch_shapes=[pltpu.VMEM((tm, D_MODEL), BF16)],
        compiler_params=pltpu.CompilerParams(
            dimension_semantics=("parallel", "arbitrary"), vmem_limit_bytes=VMEM_LIMIT),
        name="in_proj_export" if export else "in_proj",
    )(x1, w_in)
    return out if export else out[0]


def _mixer_kernel(*refs, is_prompt, n_cast):
    n_chunks = MIX_R // CHUNK
    (xa_ref, ga_ref, q_ref, k_ref, v_ref, g_ref, cos_ref, sin_ref,
     dmask_ref, qdec_ref, kdec_ref, cdec_ref,
     convw_ref, convb_ref, wri_ref, bri_ref, lam_ref, gng_ref, gnb_ref) = refs[:19]
    refs = refs[19:]
    if is_prompt:
        conv0_ref = h0_ref = s0_ref = None
    else:
        conv0_ref, h0_ref, s0_ref = refs[:3]
        refs = refs[3:]
    cast_src, refs = refs[:n_cast], refs[n_cast:]
    ya_ref, yb_ref, convo_ref, ho_ref, so_ref = refs[:5]
    cast_dst, refs = refs[5:5 + n_cast], refs[5 + n_cast:]
    (xpad_ref, xc_ref, a_ref, u_ref, hs_ref,
     qb_ref, kb_ref, qd_ref, kd_ref, vb_ref, sc_ref, oh_ref) = refs

    for src, dst in zip(cast_src, cast_dst):
        dst[...] = src[...].astype(BF16)

    if is_prompt:
        @pl.when(pl.program_id(1) == 0)
        def _():
            convo_ref[...] = jnp.zeros_like(convo_ref)
            ho_ref[...] = jnp.zeros_like(ho_ref)
            so_ref[...] = jnp.zeros_like(so_ref)

    n_seg = 1 if is_prompt else n_chunks
    seg_len = MIX_R // n_seg
    conv_b = convb_ref[...]
    w0, w1, w2, w3 = (convw_ref[j:j + 1, :] for j in range(CONV_W))
    lo = CONV_PAD - (CONV_W - 1)
    for sg in range(n_seg):
        rows = slice(sg * seg_len, (sg + 1) * seg_len)
        xpad_ref[sg, lo:CONV_PAD, :] = convo_ref[0] if is_prompt else conv0_ref[sg]
        xpad_ref[sg, CONV_PAD:CONV_PAD + seg_len, :] = xa_ref[rows, :]
        acc = conv_b + w0 * xpad_ref[sg, lo:lo + seg_len, :]
        acc = acc + w1 * xpad_ref[sg, lo + 1:lo + 1 + seg_len, :]
        acc = acc + w2 * xpad_ref[sg, lo + 2:lo + 2 + seg_len, :]
        acc = acc + w3 * xpad_ref[sg, lo + 3:lo + 3 + seg_len, :]
        xc_ref[rows, :] = acc
        convo_ref[sg] = xpad_ref[sg, lo + seg_len:CONV_PAD + seg_len, :]

    xc = xc_ref[...]
    ri = jnp.dot(xc.astype(BF16), wri_ref[...], preferred_element_type=F32) + bri_ref[...]
    r = _sigmoid(ri[:, :D_RNN])
    i = _sigmoid(ri[:, D_RNN:])
    log_a = -LRU_C * r * jax.nn.softplus(-lam_ref[...])
    a = jnp.exp(log_a)
    a_ref[...] = a
    w = jnp.tanh(-log_a) * (1.0 + a * a)
    u_ref[...] = jnp.where(w > 0.0, w * lax.rsqrt(w), 0.0) * (i * xc)

    for sg in range(n_seg):
        h_init = ho_ref[0] if is_prompt else h0_ref[sg]

        def step(t, h, base=sg * seg_len):
            row = pl.ds(base + t, 1)
            h = a_ref[row, :] * h + u_ref[row, :]
            hs_ref[row, :] = h
            return h

        ho_ref[sg] = lax.fori_loop(0, seg_len, step, h_init, unroll=True)

    ya_ref[...] = (hs_ref[...] * ga_ref[...]).astype(BF16)

    k_scale = RET_DK ** -0.5

    blk = MIX_R if is_prompt else CHUNK
    head_cols = [slice(hd * RET_DK, (hd + 1) * RET_DK) for hd in range(RET_HEADS)]
    blk_rows = [slice(c * blk, (c + 1) * blk) for c in range(MIX_R // blk)]
    cos, sin = cos_ref[...], sin_ref[...]

    for hd, cols in enumerate(head_cols):
        qh, kh = q_ref[:, cols], k_ref[:, cols]
        qr = qh * cos + pltpu.roll(qh, RET_DK // 2, 1) * sin
        kr = (kh * cos + pltpu.roll(kh, RET_DK // 2, 1) * sin) * k_scale
        qb_ref[:, cols] = qr.astype(BF16)
        kb_ref[:, cols] = kr.astype(BF16)
        qd_ref[:, cols] = (qr * qdec_ref[hd]).astype(BF16)
        kd_ref[:, cols] = (kr * kdec_ref[hd]).astype(BF16)
        vb_ref[:, cols] = v_ref[:, cols].astype(BF16)

    for rows in blk_rows:
        for hd, cols in enumerate(head_cols):
            s = lax.dot_general(qb_ref[rows, cols], kb_ref[rows, cols], (((1,), (1,)), ((), ())),
                                preferred_element_type=F32) * dmask_ref[hd]
            sc_ref[hd, rows, :] = s.astype(BF16)

    for c, rows in enumerate(blk_rows):
        for hd, cols in enumerate(head_cols):
            state = so_ref[0, hd] if is_prompt else s0_ref[c, hd]
            vb = vb_ref[rows, cols]
            o = jnp.dot(sc_ref[hd, rows, :], vb, preferred_element_type=F32)
            o = o + jnp.dot(qd_ref[rows, cols], state.astype(BF16), preferred_element_type=F32)
            oh_ref[rows, cols] = o
            new_state = cdec_ref[hd] * state + lax.dot_general(
                kd_ref[rows, cols], vb, (((0,), (0,)), ((), ())), preferred_element_type=F32)
            if is_prompt:
                so_ref[0, hd] = new_state
            else:
                so_ref[c, hd] = new_state

    for cols in head_cols:
        o = oh_ref[:, cols]
        mu = jnp.mean(o, axis=-1, keepdims=True)
        oc = o - mu
        var = jnp.mean(oc * oc, axis=-1, keepdims=True)
        on = oc * lax.rsqrt(var + GN_EPS) * gng_ref[:, cols] + gnb_ref[:, cols]
        yb_ref[:, cols] = (g_ref[:, cols] * on).astype(BF16)


def _mixer(z, tables, params, states, *, is_prompt, cast_srcs=()):
    cos_t, sin_t, dmask, qdec, kdec, cdec = tables
    n_chunks = MIX_R // CHUNK
    if is_prompt:
        tiles_per_seq = SEQ // MIX_R
        grid = (BATCH, tiles_per_seq)
        n_seq = BATCH
        n_state = 1
        row_blk = lambda s, t: s * tiles_per_seq + t
        zspec = lambda col: pl.BlockSpec((MIX_R, D_RNN), lambda s, t, col=col: (row_blk(s, t), col))
        tab_spec = pl.BlockSpec((MIX_R, RET_DK), lambda s, t: (t, 0))
        state_idx = lambda s, t: s
        out_row = lambda s, t: (row_blk(s, t), 0)
        m_rows = M_PROMPT
        sem = ("parallel", "arbitrary")
    else:
        grid = (M_SAMPLE // MIX_R,)
        n_seq = DEC_BATCH
        n_state = n_chunks
        zspec = lambda col: pl.BlockSpec((MIX_R, D_RNN), lambda i, col=col: (i, col))
        tab_spec = pl.BlockSpec((MIX_R, RET_DK), lambda i: (0, 0))
        state_idx = lambda i: i
        out_row = lambda i: (i, 0)
        m_rows = M_SAMPLE
        sem = ("parallel",)

    def const_spec(shape):
        nd = len(shape)
        return pl.BlockSpec(shape, lambda *_: (0,) * nd)

    def state_spec(shape):
        nd = len(shape)
        return pl.BlockSpec((n_state,) + shape, lambda *g: (state_idx(*g),) + (0,) * nd)

    in_specs = [zspec(c) for c in range(6)] + [tab_spec, tab_spec] + [
        const_spec(dmask.shape), const_spec(qdec.shape), const_spec(kdec.shape), const_spec(cdec.shape)
    ] + [const_spec(p.shape) for p in params]
    args = [z] * 6 + [cos_t, sin_t, dmask, qdec, kdec, cdec] + list(params)
    if not is_prompt:
        in_specs += [state_spec((CONV_W - 1, D_RNN)), state_spec((1, D_RNN)),
                     state_spec((RET_HEADS, RET_DK, RET_DV))]
        args += list(states)

    n_steps = 1
    for g in grid:
        n_steps *= g
    step_idx = (lambda s, t: (s * grid[1] + t, 0)) if is_prompt else (lambda i: (i, 0))
    cast_specs = [pl.BlockSpec((w.shape[0] // n_steps, w.shape[1]), step_idx) for w in cast_srcs]
    in_specs += cast_specs
    args += list(cast_srcs)

    n_seg = 1 if is_prompt else n_chunks
    seg_len = MIX_R // n_seg
    out_shape = [
        jax.ShapeDtypeStruct((m_rows, D_RNN), BF16),
        jax.ShapeDtypeStruct((m_rows, D_RET_V), BF16),
        jax.ShapeDtypeStruct((n_seq, CONV_W - 1, D_RNN), F32),
        jax.ShapeDtypeStruct((n_seq, 1, D_RNN), F32),
        jax.ShapeDtypeStruct((n_seq, RET_HEADS, RET_DK, RET_DV), F32),
    ]
    out_specs = [
        pl.BlockSpec((MIX_R, D_RNN), out_row),
        pl.BlockSpec((MIX_R, D_RET_V), out_row),
        state_spec((CONV_W - 1, D_RNN)),
        state_spec((1, D_RNN)),
        state_spec((RET_HEADS, RET_DK, RET_DV)),
    ] + cast_specs
    out_shape += [jax.ShapeDtypeStruct(w.shape, BF16) for w in cast_srcs]
    scratch = [
        pltpu.VMEM((n_seg, CONV_PAD + seg_len, D_RNN), F32),
        pltpu.VMEM((MIX_R, D_RNN), F32),
        pltpu.VMEM((MIX_R, D_RNN), F32),
        pltpu.VMEM((MIX_R, D_RNN), F32),
        pltpu.VMEM((MIX_R, D_RNN), F32),
    ] + [pltpu.VMEM((MIX_R, D_RET_K), BF16)] * 5 + [
        pltpu.VMEM((RET_HEADS, MIX_R, MIX_R if is_prompt else CHUNK), BF16),
        pltpu.VMEM((MIX_R, D_RET_V), F32),
    ]
    return pl.pallas_call(
        functools.partial(_mixer_kernel, is_prompt=is_prompt, n_cast=len(cast_srcs)),
        out_shape=out_shape,
        grid=grid,
        in_specs=in_specs,
        out_specs=out_specs,
        scratch_shapes=scratch,
        compiler_params=pltpu.CompilerParams(dimension_semantics=sem, vmem_limit_bytes=VMEM_LIMIT),
        name="mixer_prompt" if is_prompt else "mixer_sample",
    )(*args)


def _merge_ln_kernel(ya_ref, yb_ref, ga_ref, gb_ref, x_ref, wa_ref, wb_ref, wo_ref, g_ref, b_ref, o_ref):
    for r in range(MERGE_TM // MERGE_SUB):
        rows = slice(r * MERGE_SUB, (r + 1) * MERGE_SUB)
        pa = jnp.dot(ya_ref[rows, :], wa_ref[...], preferred_element_type=F32)
        pb = jnp.dot(yb_ref[rows, :], wb_ref[...], preferred_element_type=F32)
        merged = ga_ref[rows, :] * pa + gb_ref[rows, :] * pb
        mo = jnp.dot(merged.astype(BF16), wo_ref[...], preferred_element_type=F32)
        o_ref[rows, :] = _layer_norm_rows(DN_ALPHA * x_ref[rows, :] + mo, g_ref[...], b_ref[...])


def _merge_ln(ya, yb, z, x1, wa, wb, wo, ln_g, ln_b):
    m = x1.shape[0]
    gate_a_blk = (2 * D_RNN + 2 * D_RET_K + 2 * D_RET_V) // D_MODEL
    row = lambda i: (i, 0)
    const = lambda i: (0, 0)
    return pl.pallas_call(
        _merge_ln_kernel,
        out_shape=jax.ShapeDtypeStruct((m, D_MODEL), F32),
        grid=(m // MERGE_TM,),
        in_specs=[
            pl.BlockSpec((MERGE_TM, D_RNN), row),
            pl.BlockSpec((MERGE_TM, D_RET_V), row),
            pl.BlockSpec((MERGE_TM, D_MODEL), lambda i: (i, gate_a_blk)),
            pl.BlockSpec((MERGE_TM, D_MODEL), lambda i: (i, gate_a_blk + 1)),
            pl.BlockSpec((MERGE_TM, D_MODEL), row),
            pl.BlockSpec((D_RNN, D_MODEL), const, pipeline_mode=pl.Buffered(1)),
            pl.BlockSpec((D_RET_V, D_MODEL), const, pipeline_mode=pl.Buffered(1)),
            pl.BlockSpec((D_MODEL, D_MODEL), const, pipeline_mode=pl.Buffered(1)),
            pl.BlockSpec((1, D_MODEL), const),
            pl.BlockSpec((1, D_MODEL), const),
        ],
        out_specs=pl.BlockSpec((MERGE_TM, D_MODEL), row),
        compiler_params=pltpu.CompilerParams(
            dimension_semantics=("parallel",), vmem_limit_bytes=VMEM_LIMIT_BIG),
        name="merge_ln",
    )(ya, yb, z, z, x1, wa, wb, wo, ln_g, ln_b)


def _rope_tables(pos):
    d = RET_DK
    inv_freq = ROPE_BASE ** (-jnp.arange(0, d, 2, dtype=F32) / d)
    ang = pos.astype(F32)[:, None] * inv_freq[None, :]
    cos, sin = jnp.cos(ang), jnp.sin(ang)
    return jnp.concatenate([cos, cos], axis=-1), jnp.concatenate([-sin, sin], axis=-1)


def _decay_tables(blk):
    log_g = jnp.log1p(-jnp.exp2(-5.0 - jnp.arange(RET_HEADS, dtype=F32)))
    idx = jnp.arange(blk, dtype=F32)
    diff = idx[:, None] - idx[None, :]
    dmask = jnp.where(diff >= 0, jnp.exp(log_g[:, None, None] * jnp.maximum(diff, 0.0)), 0.0)
    q_dec = jnp.exp(log_g[:, None] * (idx[None, :] + 1.0))
    k_dec = jnp.exp(log_g[:, None] * (blk - 1.0 - idx[None, :]))
    chunk_dec = jnp.exp(log_g * blk)
    lanes = (RET_HEADS, blk, RET_DK)
    return (dmask, jnp.broadcast_to(q_dec[:, :, None], lanes), jnp.broadcast_to(k_dec[:, :, None], lanes),
            jnp.broadcast_to(chunk_dec[:, None, None], (RET_HEADS, 1, RET_DV)))


def _block_diag(w):
    tiled = jnp.tile(w.reshape(D_RNN, LRU_BLOCK), (1, LRU_BLOCKS))
    row_blk = lax.broadcasted_iota(jnp.int32, (D_RNN, D_RNN), 0) // LRU_BLOCK
    col_blk = lax.broadcasted_iota(jnp.int32, (D_RNN, D_RNN), 1) // LRU_BLOCK
    return jnp.where(row_blk == col_blk, tiled, 0.0)


def kernel(x_prompt, x_sample, state_conv, state_lru, state_ret, ffn1_w_gate, ffn1_w_up, ffn1_w_down, ln1_g, ln1_b, w_in, conv_w, conv_b, rg_w, rg_b, ig_w, ig_b, lru_lambda, ret_gn_g, ret_gn_b, w_a_proj, w_b_proj, w_o, ln2_g, ln2_b, ffn2_w_gate, ffn2_w_up, ffn2_w_down, ln3_g, ln3_b):
    assert DEPTH == 1
    w_ri = jnp.concatenate([_block_diag(rg_w[0]), _block_diag(ig_w[0])], axis=1).astype(BF16)
    b_ri = jnp.concatenate([rg_b, ig_b], axis=1)
    params = (conv_w[0], conv_b, w_ri, b_ri, lru_lambda, ret_gn_g, ret_gn_b)
    rope = _rope_tables(jnp.arange(PAST_LEN + DEC_SEQ, dtype=jnp.int32))
    x_p = x_prompt.reshape(M_PROMPT, D_MODEL)
    x_s = x_sample.reshape(M_SAMPLE, D_MODEL)
    states = (state_conv[0], state_lru[0][:, None, :], state_ret[0])

    x1_s, *ffn1_w = _ffn_ln(x_s, ffn1_w_gate[0], ffn1_w_up[0], ffn1_w_down[0], ln1_g, ln1_b, export=True)
    x1_p = _ffn_ln(x_p, *ffn1_w, ln1_g, ln1_b)
    z_s, w_in_b = _in_proj(x1_s, w_in[0], export=True)
    z_p = _in_proj(x1_p, w_in_b)

    later_w = (ffn2_w_gate[0], ffn2_w_up[0], ffn2_w_down[0], w_a_proj[0], w_b_proj[0], w_o[0])
    ya_p, yb_p, conv_p, lru_p, ret_p, *later_b = _mixer(
        z_p, rope + _decay_tables(MIX_R), params, None, is_prompt=True, cast_srcs=later_w)
    reps = MIX_R // CHUNK
    dmask_s, qdec_s, kdec_s, cdec_s = _decay_tables(CHUNK)
    tables_s = tuple(jnp.tile(t[PAST_LEN:], (reps, 1)) for t in rope) + (
        dmask_s, jnp.tile(qdec_s, (1, reps, 1)), jnp.tile(kdec_s, (1, reps, 1)), cdec_s)
    ya_s, yb_s, conv_s, lru_s, ret_s = _mixer(z_s, tables_s, params, states, is_prompt=False)
    ffn2_w, merge_w = later_b[:3], later_b[3:]

    def tail(ya, yb, z, x1):
        x2 = _merge_ln(ya, yb, z, x1, *merge_w, ln2_g, ln2_b)
        return _ffn_ln(x2, *ffn2_w, ln3_g, ln3_b)

    y_p = tail(ya_p, yb_p, z_p, x1_p)
    y_s = tail(ya_s, yb_s, z_s, x1_s)

    return (y_p.reshape(BATCH, SEQ, D_MODEL), y_s.reshape(DEC_BATCH, DEC_SEQ, D_MODEL),
            conv_p[None], lru_p.reshape(1, BATCH, D_RNN), ret_p[None],
            conv_s[None], lru_s.reshape(1, DEC_BATCH, D_RNN), ret_s[None])
```

```python
import functools

import jax
import jax.numpy as jnp
from jax import lax
from jax.experimental import pallas as pl
from jax.experimental.pallas import tpu as pltpu

F32 = jnp.float32
BF16 = jnp.bfloat16

D_MODEL = 2048
BATCH = 4
SEQ = 2048
DEPTH = 1
DEC_BATCH = 32
DEC_SEQ = 64
PAST_LEN = 2048
CHUNK = 64
D_RNN = 1024
LRU_BLOCKS = 16
LRU_BLOCK = D_RNN // LRU_BLOCKS
CONV_W = 4
LRU_C = 8.0
RET_HEADS = 8
RET_DK = 128
RET_DV = 128
D_RET_K = RET_HEADS * RET_DK
D_RET_V = RET_HEADS * RET_DV
D_FF = 5632
DN_ALPHA = (2.0 * DEPTH) ** 0.25
LN_EPS = 1e-5
GN_EPS = 1e-5
ROPE_BASE = 10000.0
D_IN = 2 * D_RNN + 2 * D_RET_K + 2 * D_RET_V + 2 * D_MODEL

M_PROMPT = BATCH * SEQ
M_SAMPLE = DEC_BATCH * DEC_SEQ

V7X_VMEM_BYTES = 64 * 1024 * 1024
VMEM_LIMIT = V7X_VMEM_BYTES - 8 * 1024 * 1024
VMEM_LIMIT_BIG = V7X_VMEM_BYTES - 3 * 1024 * 1024

FFN_TM = 1024
FFN_SUB = 512
FFN_SUB_LAST = 256
FFN_TF = 512
FFN_TF_EXPORT = 256
LN_ROWS = 128
PROJ_TM = 1024
PROJ_TN = 1024
PROJ_TN_EXPORT = 512
CAST_STEPS = 64
MIX_R = 256
MERGE_TM = 256
MERGE_SUB = 128
GELU_GROUP = 1
SWISH_GROUP = 5
SIGMOID_GROUP0 = 6
CONV_PAD = 8


def _sigmoid(x):
    return 0.5 * jnp.tanh(0.5 * x) + 0.5


def _layer_norm_rows(y, g, b):
    mu = jnp.mean(y, axis=-1, keepdims=True)
    yc = y - mu
    var = jnp.mean(yc * yc, axis=-1, keepdims=True)
    return yc * lax.rsqrt(var + LN_EPS) * g + b


def _ffn_ln_kernel(x_ref, wg_ref, wu_ref, wd_ref, g_ref, b_ref, o_ref, *rest, export):
    f = pl.program_id(1)
    last = pl.num_programs(1) - 1
    xb_ref = rest[-1]

    def sub_tiles(size):
        return [slice(r * size, (r + 1) * size) for r in range(FFN_TM // size)]

    def weights():
        if not export:
            return wg_ref[...], wu_ref[...], wd_ref[...]
        ws = tuple(w[...].astype(BF16) for w in (wg_ref, wu_ref, wd_ref))
        for dst, w in zip(rest[:3], ws):
            dst[...] = w
        return ws

    def seed(rows):
        x = x_ref[rows, :]
        o_ref[rows, :] = (2.0 * DN_ALPHA) * x
        xb_ref[rows, :] = x.astype(BF16)

    def accumulate(rows, ws):
        wg, wu, wd = ws
        xb = xb_ref[rows, :]
        gate = jnp.dot(xb, wg, preferred_element_type=F32)
        up = jnp.dot(xb, wu, preferred_element_type=F32)
        h = (gate * jax.nn.sigmoid(gate) * up).astype(BF16)
        o_ref[rows, :] += jnp.dot(h, wd, preferred_element_type=F32)

    def finish(rows):
        for c in range(rows.start, rows.stop, LN_ROWS):
            chunk = slice(c, c + LN_ROWS)
            acc = o_ref[chunk, :]
            mu = jnp.mean(acc, axis=-1, keepdims=True)
            ac = acc - mu
            var = jnp.mean(ac * ac, axis=-1, keepdims=True)
            scale = 0.5 * lax.rsqrt(0.25 * var + LN_EPS)
            o_ref[chunk, :] = ac * scale * g_ref[...] + b_ref[...]

    @pl.when(f == 0)
    def _():
        ws = weights()
        for rows in sub_tiles(FFN_SUB):
            seed(rows)
            accumulate(rows, ws)

    @pl.when(jnp.logical_and(f > 0, f < last))
    def _():
        ws = weights()
        for rows in sub_tiles(FFN_SUB):
            accumulate(rows, ws)

    @pl.when(f == last)
    def _():
        ws = weights()
        for rows in sub_tiles(FFN_SUB_LAST):
            accumulate(rows, ws)
            finish(rows)


def _ffn_ln(x, wg, wu, wd, ln_g, ln_b, *, export=False):
    m = x.shape[0]
    n_tiles = m // FFN_TM
    tf = FFN_TF_EXPORT if export else FFN_TF
    grid = (n_tiles, D_FF // tf)
    out_shape = [jax.ShapeDtypeStruct((m, D_MODEL), F32)]
    out_specs = [pl.BlockSpec((FFN_TM, D_MODEL), lambda i, f: (i, 0))]
    x_mode = {}
    if export:
        out_shape += [jax.ShapeDtypeStruct((n_tiles,) + w.shape, BF16) for w in (wg, wu, wd)]
        out_specs += [
            pl.BlockSpec((None, D_MODEL, tf), lambda i, f: (i, 0, f)),
            pl.BlockSpec((None, D_MODEL, tf), lambda i, f: (i, 0, f)),
            pl.BlockSpec((None, tf, D_MODEL), lambda i, f: (i, f, 0)),
        ]
        x_mode = dict(pipeline_mode=pl.Buffered(1))

    def w_spec(w, block, idx):
        if w.ndim == 2:
            return pl.BlockSpec(block, idx)
        return pl.BlockSpec((None,) + block, lambda i, f: (0,) + idx(i, f))

    out = pl.pallas_call(
        functools.partial(_ffn_ln_kernel, export=export),
        out_shape=out_shape,
        grid=grid,
        in_specs=[
            pl.BlockSpec((FFN_TM, D_MODEL), lambda i, f: (i, 0), **x_mode),
            w_spec(wg, (D_MODEL, tf), lambda i, f: (0, f)),
            w_spec(wu, (D_MODEL, tf), lambda i, f: (0, f)),
            w_spec(wd, (tf, D_MODEL), lambda i, f: (f, 0)),
            pl.BlockSpec((1, D_MODEL), lambda i, f: (0, 0)),
            pl.BlockSpec((1, D_MODEL), lambda i, f: (0, 0)),
        ],
        out_specs=out_specs,
        scratch_shapes=[pltpu.VMEM((FFN_TM, D_MODEL), BF16)],
        compiler_params=pltpu.CompilerParams(
            dimension_semantics=("parallel", "arbitrary"), vmem_limit_bytes=VMEM_LIMIT_BIG),
        name="ffn_ln_export" if export else "ffn_ln",
    )(x, wg, wu, wd, ln_g, ln_b)
    return out if export else out[0]


def _in_proj_kernel(x_ref, w_ref, *rest, export, blocks_per_group, n_cast):
    cast_src, rest = rest[:n_cast], rest[n_cast:]
    o_ref, xb_ref = rest[0], rest[-1]
    cast_dst = rest[len(rest) - 1 - n_cast:-1]
    j = pl.program_id(1)
    tm = o_ref.shape[0]

    @pl.when(j == 0)
    def _():
        xb_ref[...] = x_ref[...].astype(BF16)

    def emit(act):
        for src, dst in zip(cast_src, cast_dst):
            dst[...] = src[...].astype(BF16)
        if export:
            w = w_ref[...].astype(BF16)
            rest[1][...] = w
        else:
            w = w_ref[...]
        for rows in [slice(r * tm // 4, (r + 1) * tm // 4) for r in range(4)]:
            o_ref[rows, :] = act(jnp.dot(xb_ref[rows, :], w, preferred_element_type=F32))

    group = lax.div(j, blocks_per_group)
    is_gelu, is_swish, is_sigmoid = group == GELU_GROUP, group == SWISH_GROUP, group >= SIGMOID_GROUP0
    pl.when(is_gelu)(lambda: emit(jax.nn.gelu))
    pl.when(is_swish)(lambda: emit(lambda t: t * _sigmoid(t)))
    pl.when(is_sigmoid)(lambda: emit(jax.nn.sigmoid))
    plain = jnp.logical_not(jnp.logical_or(jnp.logical_or(is_gelu, is_swish), is_sigmoid))
    pl.when(plain)(lambda: emit(lambda t: t))


def _in_proj(x1, w_in, *, export=False, cast_srcs=()):
    m = x1.shape[0]
    tm, tn = (m, PROJ_TN_EXPORT) if export else (PROJ_TM, PROJ_TN)
    assert not export or m == tm
    nj = D_IN // tn
    out_shape = [jax.ShapeDtypeStruct((m, D_IN), F32)]
    out_specs = [pl.BlockSpec((tm, tn), lambda i, j: (i, j))]
    x_mode = {}
    if export:
        out_shape.append(jax.ShapeDtypeStruct(w_in.shape, BF16))
        out_specs.append(pl.BlockSpec((D_MODEL, tn), lambda i, j: (0, j)))
        x_mode = dict(pipeline_mode=pl.Buffered(1))
    assert not cast_srcs or (m // tm) * nj >= CAST_STEPS
    slab = lambda i, j: (jnp.minimum(i * nj + j, CAST_STEPS - 1), 0)
    cast_specs = [pl.BlockSpec((w.shape[0] // CAST_STEPS, w.shape[1]), slab) for w in cast_srcs]
    out_shape += [jax.ShapeDtypeStruct(w.shape, BF16) for w in cast_srcs]
    out_specs += cast_specs
    out = pl.pallas_call(
        functools.partial(_in_proj_kernel, export=export, blocks_per_group=D_RNN // tn, n_cast=len(cast_srcs)),
        out_shape=out_shape,
        grid=(m // tm, nj),
        in_specs=[
            pl.BlockSpec((tm, D_MODEL), lambda i, j: (i, 0), **x_mode),
            pl.BlockSpec((D_MODEL, tn), lambda i, j: (0, j)),
        ] + cast_specs,
        out_specs=out_specs,
        scratch_shapes=[pltpu.VMEM((tm, D_MODEL), BF16)],
        compiler_params=pltpu.CompilerParams(
            dimension_semantics=("arbitrary", "arbitrary"), vmem_limit_bytes=VMEM_LIMIT),
        name="in_proj_export" if export else "in_proj",
    )(x1, w_in, *cast_srcs)
    return out if (export or cast_srcs) else out[0]


def _mixer_kernel(*refs, is_prompt, n_cast):
    n_chunks = MIX_R // CHUNK
    (xa_ref, ga_ref, q_ref, k_ref, v_ref, g_ref, cos_ref, sin_ref,
     dmask_ref, qdec_ref, kdec_ref, cdec_ref,
     convw_ref, convb_ref, wri_ref, bri_ref, lam_ref, gng_ref, gnb_ref) = refs[:19]
    refs = refs[19:]
    if is_prompt:
        conv0_ref = h0_ref = s0_ref = None
    else:
        conv0_ref, h0_ref, s0_ref = refs[:3]
        refs = refs[3:]
    cast_src, refs = refs[:n_cast], refs[n_cast:]
    ya_ref, yb_ref, convo_ref, ho_ref, so_ref = refs[:5]
    cast_dst, refs = refs[5:5 + n_cast], refs[5 + n_cast:]
    (xpad_ref, xc_ref, a_ref, u_ref, hs_ref,
     qb_ref, kb_ref, qd_ref, kd_ref, vb_ref, sc_ref, oh_ref) = refs

    for src, dst in zip(cast_src, cast_dst):
        dst[...] = src[...].astype(BF16)

    if is_prompt:
        @pl.when(pl.program_id(1) == 0)
        def _():
            convo_ref[...] = jnp.zeros_like(convo_ref)
            ho_ref[...] = jnp.zeros_like(ho_ref)
            so_ref[...] = jnp.zeros_like(so_ref)

    n_seg = 1 if is_prompt else n_chunks
    seg_len = MIX_R // n_seg
    conv_b = convb_ref[...]
    w0, w1, w2, w3 = (convw_ref[j:j + 1, :] for j in range(CONV_W))
    lo = CONV_PAD - (CONV_W - 1)
    for sg in range(n_seg):
        rows = slice(sg * seg_len, (sg + 1) * seg_len)
        xpad_ref[sg, lo:CONV_PAD, :] = convo_ref[0] if is_prompt else conv0_ref[sg]
        xpad_ref[sg, CONV_PAD:CONV_PAD + seg_len, :] = xa_ref[rows, :]
        acc = conv_b + w0 * xpad_ref[sg, lo:lo + seg_len, :]
        acc = acc + w1 * xpad_ref[sg, lo + 1:lo + 1 + seg_len, :]
        acc = acc + w2 * xpad_ref[sg, lo + 2:lo + 2 + seg_len, :]
        acc = acc + w3 * xpad_ref[sg, lo + 3:lo + 3 + seg_len, :]
        xc_ref[rows, :] = acc
        convo_ref[sg] = xpad_ref[sg, lo + seg_len:CONV_PAD + seg_len, :]

    xc = xc_ref[...]
    ri = jnp.dot(xc.astype(BF16), wri_ref[...], preferred_element_type=F32) + bri_ref[...]
    r = _sigmoid(ri[:, :D_RNN])
    i = _sigmoid(ri[:, D_RNN:])
    log_a = -LRU_C * r * jax.nn.softplus(-lam_ref[...])
    a = jnp.exp(log_a)
    a_ref[...] = a
    w = jnp.tanh(-log_a) * (1.0 + a * a)
    u_ref[...] = jnp.where(w > 0.0, w * lax.rsqrt(w), 0.0) * (i * xc)

    for sg in range(n_seg):
        h_init = ho_ref[0] if is_prompt else h0_ref[sg]

        def step(t, h, base=sg * seg_len):
            row = pl.ds(base + t, 1)
            h = a_ref[row, :] * h + u_ref[row, :]
            hs_ref[row, :] = h
            return h

        ho_ref[sg] = lax.fori_loop(0, seg_len, step, h_init, unroll=True)

    ya_ref[...] = (hs_ref[...] * ga_ref[...]).astype(BF16)

    k_scale = RET_DK ** -0.5

    blk = MIX_R if is_prompt else CHUNK
    head_cols = [slice(hd * RET_DK, (hd + 1) * RET_DK) for hd in range(RET_HEADS)]
    blk_rows = [slice(c * blk, (c + 1) * blk) for c in range(MIX_R // blk)]
    cos, sin = cos_ref[...], sin_ref[...]

    for hd, cols in enumerate(head_cols):
        qh, kh = q_ref[:, cols], k_ref[:, cols]
        qr = qh * cos + pltpu.roll(qh, RET_DK // 2, 1) * sin
        kr = (kh * cos + pltpu.roll(kh, RET_DK // 2, 1) * sin) * k_scale
        qb_ref[:, cols] = qr.astype(BF16)
        kb_ref[:, cols] = kr.astype(BF16)
        qd_ref[:, cols] = (qr * qdec_ref[hd]).astype(BF16)
        kd_ref[:, cols] = (kr * kdec_ref[hd]).astype(BF16)
        vb_ref[:, cols] = v_ref[:, cols].astype(BF16)

    for rows in blk_rows:
        for hd, cols in enumerate(head_cols):
            s = lax.dot_general(qb_ref[rows, cols], kb_ref[rows, cols], (((1,), (1,)), ((), ())),
                                preferred_element_type=F32) * dmask_ref[hd]
            sc_ref[hd, rows, :] = s.astype(BF16)

    for c, rows in enumerate(blk_rows):
        for hd, cols in enumerate(head_cols):
            state = so_ref[0, hd] if is_prompt else s0_ref[c, hd]
            vb = vb_ref[rows, cols]
            o = jnp.dot(sc_ref[hd, rows, :], vb, preferred_element_type=F32)
            o = o + jnp.dot(qd_ref[rows, cols], state.astype(BF16), preferred_element_type=F32)
            oh_ref[rows, cols] = o
            new_state = cdec_ref[hd] * state + lax.dot_general(
                kd_ref[rows, cols], vb, (((0,), (0,)), ((), ())), preferred_element_type=F32)
            if is_prompt:
                so_ref[0, hd] = new_state
            else:
                so_ref[c, hd] = new_state

    for cols in head_cols:
        o = oh_ref[:, cols]
        mu = jnp.mean(o, axis=-1, keepdims=True)
        oc = o - mu
        var = jnp.mean(oc * oc, axis=-1, keepdims=True)
        on = oc * lax.rsqrt(var + GN_EPS) * gng_ref[:, cols] + gnb_ref[:, cols]
        yb_ref[:, cols] = (g_ref[:, cols] * on).astype(BF16)


def _mixer(z, tables, params, states, *, is_prompt, cast_srcs=()):
    cos_t, sin_t, dmask, qdec, kdec, cdec = tables
    n_chunks = MIX_R // CHUNK
    if is_prompt:
        tiles_per_seq = SEQ // MIX_R
        grid = (BATCH, tiles_per_seq)
        n_seq = BATCH
        n_state = 1
        row_blk = lambda s, t: s * tiles_per_seq + t
        zspec = lambda col: pl.BlockSpec((MIX_R, D_RNN), lambda s, t, col=col: (row_blk(s, t), col))
        tab_spec = pl.BlockSpec((MIX_R, RET_DK), lambda s, t: (t, 0))
        state_idx = lambda s, t: s
        out_row = lambda s, t: (row_blk(s, t), 0)
        m_rows = M_PROMPT
        sem = ("parallel", "arbitrary")
    else:
        grid = (M_SAMPLE // MIX_R,)
        n_seq = DEC_BATCH
        n_state = n_chunks
        zspec = lambda col: pl.BlockSpec((MIX_R, D_RNN), lambda i, col=col: (i, col))
        tab_spec = pl.BlockSpec((MIX_R, RET_DK), lambda i: (0, 0))
        state_idx = lambda i: i
        out_row = lambda i: (i, 0)
        m_rows = M_SAMPLE
        sem = ("parallel",)

    def const_spec(shape):
        nd = len(shape)
        return pl.BlockSpec(shape, lambda *_: (0,) * nd)

    def state_spec(shape):
        nd = len(shape)
        return pl.BlockSpec((n_state,) + shape, lambda *g: (state_idx(*g),) + (0,) * nd)

    in_specs = [zspec(c) for c in range(6)] + [tab_spec, tab_spec] + [
        const_spec(dmask.shape), const_spec(qdec.shape), const_spec(kdec.shape), const_spec(cdec.shape)
    ] + [const_spec(p.shape) for p in params]
    args = [z] * 6 + [cos_t, sin_t, dmask, qdec, kdec, cdec] + list(params)
    if not is_prompt:
        in_specs += [state_spec((CONV_W - 1, D_RNN)), state_spec((1, D_RNN)),
                     state_spec((RET_HEADS, RET_DK, RET_DV))]
        args += list(states)

    n_steps = 1
    for g in grid:
        n_steps *= g
    step_idx = (lambda s, t: (s * grid[1] + t, 0)) if is_prompt else (lambda i: (i, 0))
    cast_specs = [pl.BlockSpec((w.shape[0] // n_steps, w.shape[1]), step_idx) for w in cast_srcs]
    in_specs += cast_specs
    args += list(cast_srcs)

    n_seg = 1 if is_prompt else n_chunks
    seg_len = MIX_R // n_seg
    out_shape = [
        jax.ShapeDtypeStruct((m_rows, D_RNN), BF16),
        jax.ShapeDtypeStruct((m_rows, D_RET_V), BF16),
        jax.ShapeDtypeStruct((n_seq, CONV_W - 1, D_RNN), F32),
        jax.ShapeDtypeStruct((n_seq, 1, D_RNN), F32),
        jax.ShapeDtypeStruct((n_seq, RET_HEADS, RET_DK, RET_DV), F32),
    ]
    out_specs = [
        pl.BlockSpec((MIX_R, D_RNN), out_row),
        pl.BlockSpec((MIX_R, D_RET_V), out_row),
        state_spec((CONV_W - 1, D_RNN)),
        state_spec((1, D_RNN)),
        state_spec((RET_HEADS, RET_DK, RET_DV)),
    ] + cast_specs
    out_shape += [jax.ShapeDtypeStruct(w.shape, BF16) for w in cast_srcs]
    scratch = [
        pltpu.VMEM((n_seg, CONV_PAD + seg_len, D_RNN), F32),
        pltpu.VMEM((MIX_R, D_RNN), F32),
        pltpu.VMEM((MIX_R, D_RNN), F32),
        pltpu.VMEM((MIX_R, D_RNN), F32),
        pltpu.VMEM((MIX_R, D_RNN), F32),
    ] + [pltpu.VMEM((MIX_R, D_RET_K), BF16)] * 5 + [
        pltpu.VMEM((RET_HEADS, MIX_R, MIX_R if is_prompt else CHUNK), BF16),
        pltpu.VMEM((MIX_R, D_RET_V), F32),
    ]
    return pl.pallas_call(
        functools.partial(_mixer_kernel, is_prompt=is_prompt, n_cast=len(cast_srcs)),
        out_shape=out_shape,
        grid=grid,
        in_specs=in_specs,
        out_specs=out_specs,
        scratch_shapes=scratch,
        compiler_params=pltpu.CompilerParams(dimension_semantics=sem, vmem_limit_bytes=VMEM_LIMIT),
        name="mixer_prompt" if is_prompt else "mixer_sample",
    )(*args)


def _merge_ln_kernel(ya_ref, yb_ref, ga_ref, gb_ref, x_ref, wa_ref, wb_ref, wo_ref, g_ref, b_ref, o_ref):
    for r in range(MERGE_TM // MERGE_SUB):
        rows = slice(r * MERGE_SUB, (r + 1) * MERGE_SUB)
        pa = jnp.dot(ya_ref[rows, :], wa_ref[...], preferred_element_type=F32)
        pb = jnp.dot(yb_ref[rows, :], wb_ref[...], preferred_element_type=F32)
        merged = ga_ref[rows, :] * pa + gb_ref[rows, :] * pb
        mo = jnp.dot(merged.astype(BF16), wo_ref[...], preferred_element_type=F32)
        o_ref[rows, :] = _layer_norm_rows(DN_ALPHA * x_ref[rows, :] + mo, g_ref[...], b_ref[...])


def _merge_ln(ya, yb, z, x1, wa, wb, wo, ln_g, ln_b):
    m = x1.shape[0]
    gate_a_blk = (2 * D_RNN + 2 * D_RET_K + 2 * D_RET_V) // D_MODEL
    row = lambda i: (i, 0)
    const = lambda i: (0, 0)
    return pl.pallas_call(
        _merge_ln_kernel,
        out_shape=jax.ShapeDtypeStruct((m, D_MODEL), F32),
        grid=(m // MERGE_TM,),
        in_specs=[
            pl.BlockSpec((MERGE_TM, D_RNN), row),
            pl.BlockSpec((MERGE_TM, D_RET_V), row),
            pl.BlockSpec((MERGE_TM, D_MODEL), lambda i: (i, gate_a_blk)),
            pl.BlockSpec((MERGE_TM, D_MODEL), lambda i: (i, gate_a_blk + 1)),
            pl.BlockSpec((MERGE_TM, D_MODEL), row),
            pl.BlockSpec((D_RNN, D_MODEL), const, pipeline_mode=pl.Buffered(1)),
            pl.BlockSpec((D_RET_V, D_MODEL), const, pipeline_mode=pl.Buffered(1)),
            pl.BlockSpec((D_MODEL, D_MODEL), const, pipeline_mode=pl.Buffered(1)),
            pl.BlockSpec((1, D_MODEL), const),
            pl.BlockSpec((1, D_MODEL), const),
        ],
        out_specs=pl.BlockSpec((MERGE_TM, D_MODEL), row),
        compiler_params=pltpu.CompilerParams(
            dimension_semantics=("parallel",), vmem_limit_bytes=VMEM_LIMIT),
        name="merge_ln",
    )(ya, yb, z, z, x1, wa, wb, wo, ln_g, ln_b)


def _rope_tables(pos):
    d = RET_DK
    inv_freq = ROPE_BASE ** (-jnp.arange(0, d, 2, dtype=F32) / d)
    ang = pos.astype(F32)[:, None] * inv_freq[None, :]
    cos, sin = jnp.cos(ang), jnp.sin(ang)
    return jnp.concatenate([cos, cos], axis=-1), jnp.concatenate([-sin, sin], axis=-1)


def _decay_tables(blk):
    log_g = jnp.log1p(-jnp.exp2(-5.0 - jnp.arange(RET_HEADS, dtype=F32)))
    idx = jnp.arange(blk, dtype=F32)
    diff = idx[:, None] - idx[None, :]
    dmask = jnp.where(diff >= 0, jnp.exp(log_g[:, None, None] * jnp.maximum(diff, 0.0)), 0.0)
    q_dec = jnp.exp(log_g[:, None] * (idx[None, :] + 1.0))
    k_dec = jnp.exp(log_g[:, None] * (blk - 1.0 - idx[None, :]))
    chunk_dec = jnp.exp(log_g * blk)
    lanes = (RET_HEADS, blk, RET_DK)
    return (dmask, jnp.broadcast_to(q_dec[:, :, None], lanes), jnp.broadcast_to(k_dec[:, :, None], lanes),
            jnp.broadcast_to(chunk_dec[:, None, None], (RET_HEADS, 1, RET_DV)))


def _block_diag(w):
    tiled = jnp.tile(w.reshape(D_RNN, LRU_BLOCK), (1, LRU_BLOCKS))
    row_blk = lax.broadcasted_iota(jnp.int32, (D_RNN, D_RNN), 0) // LRU_BLOCK
    col_blk = lax.broadcasted_iota(jnp.int32, (D_RNN, D_RNN), 1) // LRU_BLOCK
    return jnp.where(row_blk == col_blk, tiled, 0.0)


def kernel(x_prompt, x_sample, state_conv, state_lru, state_ret, ffn1_w_gate, ffn1_w_up, ffn1_w_down, ln1_g, ln1_b, w_in, conv_w, conv_b, rg_w, rg_b, ig_w, ig_b, lru_lambda, ret_gn_g, ret_gn_b, w_a_proj, w_b_proj, w_o, ln2_g, ln2_b, ffn2_w_gate, ffn2_w_up, ffn2_w_down, ln3_g, ln3_b):
    assert DEPTH == 1
    w_ri = jnp.concatenate([_block_diag(rg_w[0]), _block_diag(ig_w[0])], axis=1).astype(BF16)
    b_ri = jnp.concatenate([rg_b, ig_b], axis=1)
    params = (conv_w[0], conv_b, w_ri, b_ri, lru_lambda, ret_gn_g, ret_gn_b)
    rope = _rope_tables(jnp.arange(PAST_LEN + DEC_SEQ, dtype=jnp.int32))
    x_p = x_prompt.reshape(M_PROMPT, D_MODEL)
    x_s = x_sample.reshape(M_SAMPLE, D_MODEL)
    states = (state_conv[0], state_lru[0][:, None, :], state_ret[0])

    x1_s, *ffn1_w = _ffn_ln(x_s, ffn1_w_gate[0], ffn1_w_up[0], ffn1_w_down[0], ln1_g, ln1_b, export=True)
    x1_p = _ffn_ln(x_p, *ffn1_w, ln1_g, ln1_b)
    z_s, w_in_b = _in_proj(x1_s, w_in[0], export=True)
    ffn2_f32 = (ffn2_w_gate[0], ffn2_w_up[0], ffn2_w_down[0].reshape(2 * D_FF, D_MODEL // 2))
    z_p, wg2, wu2, wd2 = _in_proj(x1_p, w_in_b, cast_srcs=ffn2_f32)
    ffn2_w = (wg2, wu2, wd2.reshape(D_FF, D_MODEL))

    ya_p, yb_p, conv_p, lru_p, ret_p, *merge_w = _mixer(
        z_p, rope + _decay_tables(MIX_R), params, None, is_prompt=True,
        cast_srcs=(w_a_proj[0], w_b_proj[0], w_o[0]))
    reps = MIX_R // CHUNK
    dmask_s, qdec_s, kdec_s, cdec_s = _decay_tables(CHUNK)
    tables_s = tuple(jnp.tile(t[PAST_LEN:], (reps, 1)) for t in rope) + (
        dmask_s, jnp.tile(qdec_s, (1, reps, 1)), jnp.tile(kdec_s, (1, reps, 1)), cdec_s)
    ya_s, yb_s, conv_s, lru_s, ret_s = _mixer(z_s, tables_s, params, states, is_prompt=False)

    def tail(ya, yb, z, x1):
        x2 = _merge_ln(ya, yb, z, x1, *merge_w, ln2_g, ln2_b)
        return _ffn_ln(x2, *ffn2_w, ln3_g, ln3_b)

    y_p = tail(ya_p, yb_p, z_p, x1_p)
    y_s = tail(ya_s, yb_s, z_s, x1_s)

    return (y_p.reshape(BATCH, SEQ, D_MODEL), y_s.reshape(DEC_BATCH, DEC_SEQ, D_MODEL),
            conv_p[None], lru_p.reshape(1, BATCH, D_RNN), ret_p[None],
            conv_s[None], lru_s.reshape(1, DEC_BATCH, D_RNN), ret_s[None])
```

```python
import functools

import jax
import jax.numpy as jnp
from jax import lax
from jax.experimental import pallas as pl
from jax.experimental.pallas import tpu as pltpu

F32 = jnp.float32
BF16 = jnp.bfloat16

D_MODEL = 2048
BATCH = 4
SEQ = 2048
DEPTH = 1
DEC_BATCH = 32
DEC_SEQ = 64
PAST_LEN = 2048
CHUNK = 64
D_RNN = 1024
LRU_BLOCKS = 16
LRU_BLOCK = D_RNN // LRU_BLOCKS
CONV_W = 4
LRU_C = 8.0
RET_HEADS = 8
RET_DK = 128
RET_DV = 128
D_RET_K = RET_HEADS * RET_DK
D_RET_V = RET_HEADS * RET_DV
D_FF = 5632
DN_ALPHA = (2.0 * DEPTH) ** 0.25
LN_EPS = 1e-5
GN_EPS = 1e-5
ROPE_BASE = 10000.0
D_IN = 2 * D_RNN + 2 * D_RET_K + 2 * D_RET_V + 2 * D_MODEL

M_PROMPT = BATCH * SEQ
M_SAMPLE = DEC_BATCH * DEC_SEQ

V7X_VMEM_BYTES = 64 * 1024 * 1024
VMEM_LIMIT = V7X_VMEM_BYTES - 8 * 1024 * 1024
VMEM_LIMIT_BIG = V7X_VMEM_BYTES - 3 * 1024 * 1024

FFN_TM = 1024
FFN_SUB = 512
FFN_SUB_LAST = 256
FFN_TF = 512
FFN_TF_EXPORT = 256
LN_ROWS = 128
PROJ_TM = 1024
PROJ_TN = 1024
PROJ_TN_EXPORT = 512
MIX_R = 256
MERGE_TM = 256
MERGE_SUB = 128
GELU_GROUP = 1
SWISH_GROUP = 5
CONV_PAD = 8


def _sigmoid(x):
    return 0.5 * jnp.tanh(0.5 * x) + 0.5


def _layer_norm_rows(y, g, b):
    mu = jnp.mean(y, axis=-1, keepdims=True)
    yc = y - mu
    var = jnp.mean(yc * yc, axis=-1, keepdims=True)
    return yc * lax.rsqrt(var + LN_EPS) * g + b


def _ffn_ln_kernel(x_ref, wg_ref, wu_ref, wd_ref, g_ref, b_ref, o_ref, *rest, export):
    f = pl.program_id(1)
    last = pl.num_programs(1) - 1
    xb_ref = rest[-1]

    def sub_tiles(size):
        return [slice(r * size, (r + 1) * size) for r in range(FFN_TM // size)]

    def weights():
        if not export:
            return wg_ref[...], wu_ref[...], wd_ref[...]
        ws = tuple(w[...].astype(BF16) for w in (wg_ref, wu_ref, wd_ref))
        for dst, w in zip(rest[:3], ws):
            dst[...] = w
        return ws

    def seed(rows):
        x = x_ref[rows, :]
        o_ref[rows, :] = (2.0 * DN_ALPHA) * x
        xb_ref[rows, :] = x.astype(BF16)

    def accumulate(rows, ws):
        wg, wu, wd = ws
        xb = xb_ref[rows, :]
        gate = jnp.dot(xb, wg, preferred_element_type=F32)
        up = jnp.dot(xb, wu, preferred_element_type=F32)
        h = (gate * jax.nn.sigmoid(gate) * up).astype(BF16)
        o_ref[rows, :] += jnp.dot(h, wd, preferred_element_type=F32)

    def finish(rows):
        for c in range(rows.start, rows.stop, LN_ROWS):
            chunk = slice(c, c + LN_ROWS)
            acc = o_ref[chunk, :]
            mu = jnp.mean(acc, axis=-1, keepdims=True)
            ac = acc - mu
            var = jnp.mean(ac * ac, axis=-1, keepdims=True)
            scale = 0.5 * lax.rsqrt(0.25 * var + LN_EPS)
            o_ref[chunk, :] = ac * scale * g_ref[...] + b_ref[...]

    @pl.when(f == 0)
    def _():
        ws = weights()
        for rows in sub_tiles(FFN_SUB):
            seed(rows)
            accumulate(rows, ws)

    @pl.when(jnp.logical_and(f > 0, f < last))
    def _():
        ws = weights()
        for rows in sub_tiles(FFN_SUB):
            accumulate(rows, ws)

    @pl.when(f == last)
    def _():
        ws = weights()
        for rows in sub_tiles(FFN_SUB_LAST):
            accumulate(rows, ws)
            finish(rows)


def _ffn_ln(x, wg, wu, wd, ln_g, ln_b, *, export=False):
    m = x.shape[0]
    n_tiles = m // FFN_TM
    tf = FFN_TF_EXPORT if export else FFN_TF
    grid = (n_tiles, D_FF // tf)
    out_shape = [jax.ShapeDtypeStruct((m, D_MODEL), F32)]
    out_specs = [pl.BlockSpec((FFN_TM, D_MODEL), lambda i, f: (i, 0))]
    x_mode = {}
    if export:
        out_shape += [jax.ShapeDtypeStruct((n_tiles,) + w.shape, BF16) for w in (wg, wu, wd)]
        out_specs += [
            pl.BlockSpec((None, D_MODEL, tf), lambda i, f: (i, 0, f)),
            pl.BlockSpec((None, D_MODEL, tf), lambda i, f: (i, 0, f)),
            pl.BlockSpec((None, tf, D_MODEL), lambda i, f: (i, f, 0)),
        ]
        x_mode = dict(pipeline_mode=pl.Buffered(1))

    def w_spec(w, block, idx):
        if w.ndim == 2:
            return pl.BlockSpec(block, idx)
        return pl.BlockSpec((None,) + block, lambda i, f: (0,) + idx(i, f))

    out = pl.pallas_call(
        functools.partial(_ffn_ln_kernel, export=export),
        out_shape=out_shape,
        grid=grid,
        in_specs=[
            pl.BlockSpec((FFN_TM, D_MODEL), lambda i, f: (i, 0), **x_mode),
            w_spec(wg, (D_MODEL, tf), lambda i, f: (0, f)),
            w_spec(wu, (D_MODEL, tf), lambda i, f: (0, f)),
            w_spec(wd, (tf, D_MODEL), lambda i, f: (f, 0)),
            pl.BlockSpec((1, D_MODEL), lambda i, f: (0, 0)),
            pl.BlockSpec((1, D_MODEL), lambda i, f: (0, 0)),
        ],
        out_specs=out_specs,
        scratch_shapes=[pltpu.VMEM((FFN_TM, D_MODEL), BF16)],
        compiler_params=pltpu.CompilerParams(
            dimension_semantics=("parallel", "arbitrary"), vmem_limit_bytes=VMEM_LIMIT_BIG),
        name="ffn_ln_export" if export else "ffn_ln",
    )(x, wg, wu, wd, ln_g, ln_b)
    return out if export else out[0]


def _in_proj_kernel(x_ref, w_ref, o_ref, *rest, export, blocks_per_group):
    xb_ref = rest[-1]
    j = pl.program_id(1)
    tm = o_ref.shape[0]

    @pl.when(j == 0)
    def _():
        xb_ref[...] = x_ref[...].astype(BF16)

    def emit(act):
        if export:
            w = w_ref[...].astype(BF16)
            rest[0][...] = w
        else:
            w = w_ref[...]
        if act is None:
            o_ref[...] = jnp.dot(xb_ref[...], w, preferred_element_type=F32)
            return
        for rows in [slice(r * tm // 4, (r + 1) * tm // 4) for r in range(4)]:
            o_ref[rows, :] = act(jnp.dot(xb_ref[rows, :], w, preferred_element_type=F32))

    group = lax.div(j, blocks_per_group)
    is_gelu, is_swish = group == GELU_GROUP, group == SWISH_GROUP
    pl.when(is_gelu)(lambda: emit(jax.nn.gelu))
    pl.when(is_swish)(lambda: emit(lambda t: t * _sigmoid(t)))
    pl.when(jnp.logical_not(jnp.logical_or(is_gelu, is_swish)))(lambda: emit(None))


def _in_proj(x1, w_in, *, export=False):
    m = x1.shape[0]
    tm, tn = (m, PROJ_TN_EXPORT) if export else (PROJ_TM, PROJ_TN)
    assert not export or m == tm
    out_shape = [jax.ShapeDtypeStruct((m, D_IN), F32)]
    out_specs = [pl.BlockSpec((tm, tn), lambda i, j: (i, j))]
    x_mode = {}
    if export:
        out_shape.append(jax.ShapeDtypeStruct(w_in.shape, BF16))
        out_specs.append(pl.BlockSpec((D_MODEL, tn), lambda i, j: (0, j)))
        x_mode = dict(pipeline_mode=pl.Buffered(1))
    out = pl.pallas_call(
        functools.partial(_in_proj_kernel, export=export, blocks_per_group=D_RNN // tn),
        out_shape=out_shape,
        grid=(m // tm, D_IN // tn),
        in_specs=[
            pl.BlockSpec((tm, D_MODEL), lambda i, j: (i, 0), **x_mode),
            pl.BlockSpec((D_MODEL, tn), lambda i, j: (0, j)),
        ],
        out_specs=out_specs,
        scratch_shapes=[pltpu.VMEM((tm, D_MODEL), BF16)],
        compiler_params=pltpu.CompilerParams(
            dimension_semantics=("parallel", "arbitrary"), vmem_limit_bytes=VMEM_LIMIT),
        name="in_proj_export" if export else "in_proj",
    )(x1, w_in)
    return out if export else out[0]


def _mixer_kernel(*refs, is_prompt, n_cast):
    n_chunks = MIX_R // CHUNK
    (xa_ref, ga_ref, q_ref, k_ref, v_ref, g_ref, cos_ref, sin_ref,
     dmask_ref, qdec_ref, kdec_ref, cdec_ref,
     convw_ref, convb_ref, wri_ref, bri_ref, lam_ref, gng_ref, gnb_ref) = refs[:19]
    refs = refs[19:]
    if is_prompt:
        conv0_ref = h0_ref = s0_ref = None
    else:
        conv0_ref, h0_ref, s0_ref = refs[:3]
        refs = refs[3:]
    cast_src, refs = refs[:n_cast], refs[n_cast:]
    ya_ref, yb_ref, convo_ref, ho_ref, so_ref = refs[:5]
    cast_dst, refs = refs[5:5 + n_cast], refs[5 + n_cast:]
    (xpad_ref, xc_ref, a_ref, u_ref, hs_ref,
     qb_ref, kb_ref, qd_ref, kd_ref, vb_ref, sc_ref, oh_ref) = refs

    for src, dst in zip(cast_src, cast_dst):
        dst[...] = src[...].astype(BF16)

    if is_prompt:
        @pl.when(pl.program_id(1) == 0)
        def _():
            convo_ref[...] = jnp.zeros_like(convo_ref)
            ho_ref[...] = jnp.zeros_like(ho_ref)
            so_ref[...] = jnp.zeros_like(so_ref)

    n_seg = 1 if is_prompt else n_chunks
    seg_len = MIX_R // n_seg
    conv_b = convb_ref[...]
    w0, w1, w2, w3 = (convw_ref[j:j + 1, :] for j in range(CONV_W))
    lo = CONV_PAD - (CONV_W - 1)
    for sg in range(n_seg):
        rows = slice(sg * seg_len, (sg + 1) * seg_len)
        xpad_ref[sg, lo:CONV_PAD, :] = convo_ref[0] if is_prompt else conv0_ref[sg]
        xpad_ref[sg, CONV_PAD:CONV_PAD + seg_len, :] = xa_ref[rows, :]
        acc = conv_b + w0 * xpad_ref[sg, lo:lo + seg_len, :]
        acc = acc + w1 * xpad_ref[sg, lo + 1:lo + 1 + seg_len, :]
        acc = acc + w2 * xpad_ref[sg, lo + 2:lo + 2 + seg_len, :]
        acc = acc + w3 * xpad_ref[sg, lo + 3:lo + 3 + seg_len, :]
        xc_ref[rows, :] = acc
        convo_ref[sg] = xpad_ref[sg, lo + seg_len:CONV_PAD + seg_len, :]

    xc = xc_ref[...]
    ri = jnp.dot(xc.astype(BF16), wri_ref[...], preferred_element_type=F32) + bri_ref[...]
    r = _sigmoid(ri[:, :D_RNN])
    i = _sigmoid(ri[:, D_RNN:])
    log_a = -LRU_C * r * jax.nn.softplus(-lam_ref[...])
    a = jnp.exp(log_a)
    a_ref[...] = a
    w = jnp.tanh(-log_a) * (1.0 + a * a)
    u_ref[...] = jnp.where(w > 0.0, w * lax.rsqrt(w), 0.0) * (i * xc)

    for sg in range(n_seg):
        h_init = ho_ref[0] if is_prompt else h0_ref[sg]

        def step(t, h, base=sg * seg_len):
            row = pl.ds(base + t, 1)
            h = a_ref[row, :] * h + u_ref[row, :]
            hs_ref[row, :] = h
            return h

        ho_ref[sg] = lax.fori_loop(0, seg_len, step, h_init, unroll=True)

    ya_ref[...] = (hs_ref[...] * ga_ref[...]).astype(BF16)

    k_scale = RET_DK ** -0.5

    blk = MIX_R if is_prompt else CHUNK
    head_cols = [slice(hd * RET_DK, (hd + 1) * RET_DK) for hd in range(RET_HEADS)]
    blk_rows = [slice(c * blk, (c + 1) * blk) for c in range(MIX_R // blk)]
    cos, sin = cos_ref[...], sin_ref[...]

    for hd, cols in enumerate(head_cols):
        qh, kh = q_ref[:, cols], k_ref[:, cols]
        qr = qh * cos + pltpu.roll(qh, RET_DK // 2, 1) * sin
        kr = (kh * cos + pltpu.roll(kh, RET_DK // 2, 1) * sin) * k_scale
        qb_ref[:, cols] = qr.astype(BF16)
        kb_ref[:, cols] = kr.astype(BF16)
        qd_ref[:, cols] = (qr * qdec_ref[hd]).astype(BF16)
        kd_ref[:, cols] = (kr * kdec_ref[hd]).astype(BF16)
        vb_ref[:, cols] = v_ref[:, cols].astype(BF16)

    for rows in blk_rows:
        for hd, cols in enumerate(head_cols):
            s = lax.dot_general(qb_ref[rows, cols], kb_ref[rows, cols], (((1,), (1,)), ((), ())),
                                preferred_element_type=F32) * dmask_ref[hd]
            sc_ref[hd, rows, :] = s.astype(BF16)

    for c, rows in enumerate(blk_rows):
        for hd, cols in enumerate(head_cols):
            state = so_ref[0, hd] if is_prompt else s0_ref[c, hd]
            vb = vb_ref[rows, cols]
            o = jnp.dot(sc_ref[hd, rows, :], vb, preferred_element_type=F32)
            o = o + jnp.dot(qd_ref[rows, cols], state.astype(BF16), preferred_element_type=F32)
            oh_ref[rows, cols] = o
            new_state = cdec_ref[hd] * state + lax.dot_general(
                kd_ref[rows, cols], vb, (((0,), (0,)), ((), ())), preferred_element_type=F32)
            if is_prompt:
                so_ref[0, hd] = new_state
            else:
                so_ref[c, hd] = new_state

    for cols in head_cols:
        o = oh_ref[:, cols]
        mu = jnp.mean(o, axis=-1, keepdims=True)
        oc = o - mu
        var = jnp.mean(oc * oc, axis=-1, keepdims=True)
        on = oc * lax.rsqrt(var + GN_EPS) * gng_ref[:, cols] + gnb_ref[:, cols]
        yb_ref[:, cols] = (g_ref[:, cols] * on).astype(BF16)


def _mixer(z, tables, params, states, *, is_prompt, cast_srcs=()):
    cos_t, sin_t, dmask, qdec, kdec, cdec = tables
    n_chunks = MIX_R // CHUNK
    if is_prompt:
        tiles_per_seq = SEQ // MIX_R
        grid = (BATCH, tiles_per_seq)
        n_seq = BATCH
        n_state = 1
        row_blk = lambda s, t: s * tiles_per_seq + t
        zspec = lambda col: pl.BlockSpec((MIX_R, D_RNN), lambda s, t, col=col: (row_blk(s, t), col))
        tab_spec = pl.BlockSpec((MIX_R, RET_DK), lambda s, t: (t, 0))
        state_idx = lambda s, t: s
        out_row = lambda s, t: (row_blk(s, t), 0)
        m_rows = M_PROMPT
        sem = ("parallel", "arbitrary")
    else:
        grid = (M_SAMPLE // MIX_R,)
        n_seq = DEC_BATCH
        n_state = n_chunks
        zspec = lambda col: pl.BlockSpec((MIX_R, D_RNN), lambda i, col=col: (i, col))
        tab_spec = pl.BlockSpec((MIX_R, RET_DK), lambda i: (0, 0))
        state_idx = lambda i: i
        out_row = lambda i: (i, 0)
        m_rows = M_SAMPLE
        sem = ("parallel",)

    def const_spec(shape):
        nd = len(shape)
        return pl.BlockSpec(shape, lambda *_: (0,) * nd)

    def state_spec(shape):
        nd = len(shape)
        return pl.BlockSpec((n_state,) + shape, lambda *g: (state_idx(*g),) + (0,) * nd)

    in_specs = [zspec(c) for c in range(6)] + [tab_spec, tab_spec] + [
        const_spec(dmask.shape), const_spec(qdec.shape), const_spec(kdec.shape), const_spec(cdec.shape)
    ] + [const_spec(p.shape) for p in params]
    args = [z] * 6 + [cos_t, sin_t, dmask, qdec, kdec, cdec] + list(params)
    if not is_prompt:
        in_specs += [state_spec((CONV_W - 1, D_RNN)), state_spec((1, D_RNN)),
                     state_spec((RET_HEADS, RET_DK, RET_DV))]
        args += list(states)

    n_steps = 1
    for g in grid:
        n_steps *= g
    step_idx = (lambda s, t: (s * grid[1] + t, 0)) if is_prompt else (lambda i: (i, 0))
    cast_specs = [pl.BlockSpec((w.shape[0] // n_steps, w.shape[1]), step_idx) for w in cast_srcs]
    in_specs += cast_specs
    args += list(cast_srcs)

    n_seg = 1 if is_prompt else n_chunks
    seg_len = MIX_R // n_seg
    out_shape = [
        jax.ShapeDtypeStruct((m_rows, D_RNN), BF16),
        jax.ShapeDtypeStruct((m_rows, D_RET_V), BF16),
        jax.ShapeDtypeStruct((n_seq, CONV_W - 1, D_RNN), F32),
        jax.ShapeDtypeStruct((n_seq, 1, D_RNN), F32),
        jax.ShapeDtypeStruct((n_seq, RET_HEADS, RET_DK, RET_DV), F32),
    ]
    out_specs = [
        pl.BlockSpec((MIX_R, D_RNN), out_row),
        pl.BlockSpec((MIX_R, D_RET_V), out_row),
        state_spec((CONV_W - 1, D_RNN)),
        state_spec((1, D_RNN)),
        state_spec((RET_HEADS, RET_DK, RET_DV)),
    ] + cast_specs
    out_shape += [jax.ShapeDtypeStruct(w.shape, BF16) for w in cast_srcs]
    scratch = [
        pltpu.VMEM((n_seg, CONV_PAD + seg_len, D_RNN), F32),
        pltpu.VMEM((MIX_R, D_RNN), F32),
        pltpu.VMEM((MIX_R, D_RNN), F32),
        pltpu.VMEM((MIX_R, D_RNN), F32),
        pltpu.VMEM((MIX_R, D_RNN), F32),
    ] + [pltpu.VMEM((MIX_R, D_RET_K), BF16)] * 5 + [
        pltpu.VMEM((RET_HEADS, MIX_R, MIX_R if is_prompt else CHUNK), BF16),
        pltpu.VMEM((MIX_R, D_RET_V), F32),
    ]
    return pl.pallas_call(
        functools.partial(_mixer_kernel, is_prompt=is_prompt, n_cast=len(cast_srcs)),
        out_shape=out_shape,
        grid=grid,
        in_specs=in_specs,
        out_specs=out_specs,
        scratch_shapes=scratch,
        compiler_params=pltpu.CompilerParams(dimension_semantics=sem, vmem_limit_bytes=VMEM_LIMIT),
        name="mixer_prompt" if is_prompt else "mixer_sample",
    )(*args)


def _merge_ln_kernel(ya_ref, yb_ref, ga_ref, gb_ref, x_ref, wa_ref, wb_ref, wo_ref, g_ref, b_ref, o_ref):
    for r in range(MERGE_TM // MERGE_SUB):
        rows = slice(r * MERGE_SUB, (r + 1) * MERGE_SUB)
        pa = jnp.dot(ya_ref[rows, :], wa_ref[...], preferred_element_type=F32)
        pb = jnp.dot(yb_ref[rows, :], wb_ref[...], preferred_element_type=F32)
        merged = jax.nn.sigmoid(ga_ref[rows, :]) * pa + jax.nn.sigmoid(gb_ref[rows, :]) * pb
        mo = jnp.dot(merged.astype(BF16), wo_ref[...], preferred_element_type=F32)
        o_ref[rows, :] = _layer_norm_rows(DN_ALPHA * x_ref[rows, :] + mo, g_ref[...], b_ref[...])


def _merge_ln(ya, yb, z, x1, wa, wb, wo, ln_g, ln_b):
    m = x1.shape[0]
    gate_a_blk = (2 * D_RNN + 2 * D_RET_K + 2 * D_RET_V) // D_MODEL
    row = lambda i: (i, 0)
    const = lambda i: (0, 0)
    return pl.pallas_call(
        _merge_ln_kernel,
        out_shape=jax.ShapeDtypeStruct((m, D_MODEL), F32),
        grid=(m // MERGE_TM,),
        in_specs=[
            pl.BlockSpec((MERGE_TM, D_RNN), row),
            pl.BlockSpec((MERGE_TM, D_RET_V), row),
            pl.BlockSpec((MERGE_TM, D_MODEL), lambda i: (i, gate_a_blk)),
            pl.BlockSpec((MERGE_TM, D_MODEL), lambda i: (i, gate_a_blk + 1)),
            pl.BlockSpec((MERGE_TM, D_MODEL), row),
            pl.BlockSpec((D_RNN, D_MODEL), const, pipeline_mode=pl.Buffered(1)),
            pl.BlockSpec((D_RET_V, D_MODEL), const, pipeline_mode=pl.Buffered(1)),
            pl.BlockSpec((D_MODEL, D_MODEL), const, pipeline_mode=pl.Buffered(1)),
            pl.BlockSpec((1, D_MODEL), const),
            pl.BlockSpec((1, D_MODEL), const),
        ],
        out_specs=pl.BlockSpec((MERGE_TM, D_MODEL), row),
        compiler_params=pltpu.CompilerParams(
            dimension_semantics=("parallel",), vmem_limit_bytes=VMEM_LIMIT),
        name="merge_ln",
    )(ya, yb, z, z, x1, wa, wb, wo, ln_g, ln_b)


def _rope_tables(pos):
    d = RET_DK
    inv_freq = ROPE_BASE ** (-jnp.arange(0, d, 2, dtype=F32) / d)
    ang = pos.astype(F32)[:, None] * inv_freq[None, :]
    cos, sin = jnp.cos(ang), jnp.sin(ang)
    return jnp.concatenate([cos, cos], axis=-1), jnp.concatenate([-sin, sin], axis=-1)


def _decay_tables(blk):
    log_g = jnp.log1p(-jnp.exp2(-5.0 - jnp.arange(RET_HEADS, dtype=F32)))
    idx = jnp.arange(blk, dtype=F32)
    diff = idx[:, None] - idx[None, :]
    dmask = jnp.where(diff >= 0, jnp.exp(log_g[:, None, None] * jnp.maximum(diff, 0.0)), 0.0)
    q_dec = jnp.exp(log_g[:, None] * (idx[None, :] + 1.0))
    k_dec = jnp.exp(log_g[:, None] * (blk - 1.0 - idx[None, :]))
    chunk_dec = jnp.exp(log_g * blk)
    lanes = (RET_HEADS, blk, RET_DK)
    return (dmask, jnp.broadcast_to(q_dec[:, :, None], lanes), jnp.broadcast_to(k_dec[:, :, None], lanes),
            jnp.broadcast_to(chunk_dec[:, None, None], (RET_HEADS, 1, RET_DV)))


def _block_diag(w):
    tiled = jnp.tile(w.reshape(D_RNN, LRU_BLOCK), (1, LRU_BLOCKS))
    row_blk = lax.broadcasted_iota(jnp.int32, (D_RNN, D_RNN), 0) // LRU_BLOCK
    col_blk = lax.broadcasted_iota(jnp.int32, (D_RNN, D_RNN), 1) // LRU_BLOCK
    return jnp.where(row_blk == col_blk, tiled, 0.0)


def kernel(x_prompt, x_sample, state_conv, state_lru, state_ret, ffn1_w_gate, ffn1_w_up, ffn1_w_down, ln1_g, ln1_b, w_in, conv_w, conv_b, rg_w, rg_b, ig_w, ig_b, lru_lambda, ret_gn_g, ret_gn_b, w_a_proj, w_b_proj, w_o, ln2_g, ln2_b, ffn2_w_gate, ffn2_w_up, ffn2_w_down, ln3_g, ln3_b):
    assert DEPTH == 1
    w_ri = jnp.concatenate([_block_diag(rg_w[0]), _block_diag(ig_w[0])], axis=1).astype(BF16)
    b_ri = jnp.concatenate([rg_b, ig_b], axis=1)
    params = (conv_w[0], conv_b, w_ri, b_ri, lru_lambda, ret_gn_g, ret_gn_b)
    rope = _rope_tables(jnp.arange(PAST_LEN + DEC_SEQ, dtype=jnp.int32))
    x_p = x_prompt.reshape(M_PROMPT, D_MODEL)
    x_s = x_sample.reshape(M_SAMPLE, D_MODEL)
    states = (state_conv[0], state_lru[0][:, None, :], state_ret[0])

    x1_s, *ffn1_w = _ffn_ln(x_s, ffn1_w_gate[0], ffn1_w_up[0], ffn1_w_down[0], ln1_g, ln1_b, export=True)
    x1_p = _ffn_ln(x_p, *ffn1_w, ln1_g, ln1_b)
    z_s, w_in_b = _in_proj(x1_s, w_in[0], export=True)
    z_p = _in_proj(x1_p, w_in_b)

    later_w = (ffn2_w_gate[0], ffn2_w_up[0], ffn2_w_down[0], w_a_proj[0], w_b_proj[0], w_o[0])
    ya_p, yb_p, conv_p, lru_p, ret_p, *later_b = _mixer(
        z_p, rope + _decay_tables(MIX_R), params, None, is_prompt=True, cast_srcs=later_w)
    reps = MIX_R // CHUNK
    dmask_s, qdec_s, kdec_s, cdec_s = _decay_tables(CHUNK)
    tables_s = tuple(jnp.tile(t[PAST_LEN:], (reps, 1)) for t in rope) + (
        dmask_s, jnp.tile(qdec_s, (1, reps, 1)), jnp.tile(kdec_s, (1, reps, 1)), cdec_s)
    ya_s, yb_s, conv_s, lru_s, ret_s = _mixer(z_s, tables_s, params, states, is_prompt=False)
    ffn2_w, merge_w = later_b[:3], later_b[3:]

    def tail(ya, yb, z, x1):
        x2 = _merge_ln(ya, yb, z, x1, *merge_w, ln2_g, ln2_b)
        return _ffn_ln(x2, *ffn2_w, ln3_g, ln3_b)

    y_p = tail(ya_p, yb_p, z_p, x1_p)
    y_s = tail(ya_s, yb_s, z_s, x1_s)

    return (y_p.reshape(BATCH, SEQ, D_MODEL), y_s.reshape(DEC_BATCH, DEC_SEQ, D_MODEL),
            conv_p[None], lru_p.reshape(1, BATCH, D_RNN), ret_p[None],
            conv_s[None], lru_s.reshape(1, DEC_BATCH, D_RNN), ret_s[None])
```

```python
import functools

import jax
import jax.numpy as jnp
from jax import lax
from jax.experimental import pallas as pl
from jax.experimental.pallas import tpu as pltpu

F32 = jnp.float32
BF16 = jnp.bfloat16

D_MODEL = 2048
BATCH = 4
SEQ = 2048
DEPTH = 1
DEC_BATCH = 32
DEC_SEQ = 64
PAST_LEN = 2048
CHUNK = 64
D_RNN = 1024
LRU_BLOCKS = 16
LRU_BLOCK = D_RNN // LRU_BLOCKS
CONV_W = 4
LRU_C = 8.0
RET_HEADS = 8
RET_DK = 128
RET_DV = 128
D_RET_K = RET_HEADS * RET_DK
D_RET_V = RET_HEADS * RET_DV
D_FF = 5632
DN_ALPHA = (2.0 * DEPTH) ** 0.25
LN_EPS = 1e-5
GN_EPS = 1e-5
ROPE_BASE = 10000.0
D_IN = 2 * D_RNN + 2 * D_RET_K + 2 * D_RET_V + 2 * D_MODEL

M_PROMPT = BATCH * SEQ
M_SAMPLE = DEC_BATCH * DEC_SEQ

V7X_VMEM_BYTES = 64 * 1024 * 1024
VMEM_LIMIT = V7X_VMEM_BYTES - 8 * 1024 * 1024
VMEM_LIMIT_BIG = V7X_VMEM_BYTES - 3 * 1024 * 1024

FFN_TM = 1024
FFN_SUB = 512
FFN_SUB_LAST = 256
FFN_TF = 512
FFN_TF_EXPORT = 256
LN_ROWS = 128
PROJ_TM = 1024
PROJ_TN = 1024
PROJ_TN_EXPORT = 512
MIX_R = 256
MERGE_TM = 256
MERGE_SUB = 128
GELU_GROUP = 1
SWISH_GROUP = 5
CONV_PAD = 8


def _sigmoid(x):
    return 0.5 * jnp.tanh(0.5 * x) + 0.5


def _layer_norm_rows(y, g, b):
    mu = jnp.mean(y, axis=-1, keepdims=True)
    yc = y - mu
    var = jnp.mean(yc * yc, axis=-1, keepdims=True)
    return yc * lax.rsqrt(var + LN_EPS) * g + b


def _ffn_ln_kernel(x_ref, wg_ref, wu_ref, wd_ref, g_ref, b_ref, o_ref, *rest, export):
    f = pl.program_id(1)
    last = pl.num_programs(1) - 1
    xb_ref = rest[-1]

    def sub_tiles(size):
        return [slice(r * size, (r + 1) * size) for r in range(FFN_TM // size)]

    def weights():
        if not export:
            return wg_ref[...], wu_ref[...], wd_ref[...]
        ws = tuple(w[...].astype(BF16) for w in (wg_ref, wu_ref, wd_ref))
        for dst, w in zip(rest[:3], ws):
            dst[...] = w
        return ws

    def seed(rows):
        x = x_ref[rows, :]
        o_ref[rows, :] = (2.0 * DN_ALPHA) * x
        xb_ref[rows, :] = x.astype(BF16)

    def accumulate(rows, ws):
        wg, wu, wd = ws
        xb = xb_ref[rows, :]
        gate = jnp.dot(xb, wg, preferred_element_type=F32)
        up = jnp.dot(xb, wu, preferred_element_type=F32)
        h = (gate * jax.nn.sigmoid(gate) * up).astype(BF16)
        o_ref[rows, :] += jnp.dot(h, wd, preferred_element_type=F32)

    def finish(rows):
        for c in range(rows.start, rows.stop, LN_ROWS):
            chunk = slice(c, c + LN_ROWS)
            acc = o_ref[chunk, :]
            mu = jnp.mean(acc, axis=-1, keepdims=True)
            ac = acc - mu
            var = jnp.mean(ac * ac, axis=-1, keepdims=True)
            scale = 0.5 * lax.rsqrt(0.25 * var + LN_EPS)
            o_ref[chunk, :] = ac * scale * g_ref[...] + b_ref[...]

    @pl.when(f == 0)
    def _():
        ws = weights()
        for rows in sub_tiles(FFN_SUB):
            seed(rows)
            accumulate(rows, ws)

    @pl.when(jnp.logical_and(f > 0, f < last))
    def _():
        ws = weights()
        for rows in sub_tiles(FFN_SUB):
            accumulate(rows, ws)

    @pl.when(f == last)
    def _():
        ws = weights()
        for rows in sub_tiles(FFN_SUB_LAST):
            accumulate(rows, ws)
            finish(rows)


def _ffn_ln(x, wg, wu, wd, ln_g, ln_b, *, export=False):
    m = x.shape[0]
    n_tiles = m // FFN_TM
    tf = FFN_TF_EXPORT if export else FFN_TF
    grid = (n_tiles, D_FF // tf)
    out_shape = [jax.ShapeDtypeStruct((m, D_MODEL), F32)]
    out_specs = [pl.BlockSpec((FFN_TM, D_MODEL), lambda i, f: (i, 0))]
    x_mode = {}
    if export:
        out_shape += [jax.ShapeDtypeStruct((n_tiles,) + w.shape, BF16) for w in (wg, wu, wd)]
        out_specs += [
            pl.BlockSpec((None, D_MODEL, tf), lambda i, f: (i, 0, f)),
            pl.BlockSpec((None, D_MODEL, tf), lambda i, f: (i, 0, f)),
            pl.BlockSpec((None, tf, D_MODEL), lambda i, f: (i, f, 0)),
        ]
        x_mode = dict(pipeline_mode=pl.Buffered(1))

    def w_spec(w, block, idx):
        if w.ndim == 2:
            return pl.BlockSpec(block, idx)
        return pl.BlockSpec((None,) + block, lambda i, f: (0,) + idx(i, f))

    out = pl.pallas_call(
        functools.partial(_ffn_ln_kernel, export=export),
        out_shape=out_shape,
        grid=grid,
        in_specs=[
            pl.BlockSpec((FFN_TM, D_MODEL), lambda i, f: (i, 0), **x_mode),
            w_spec(wg, (D_MODEL, tf), lambda i, f: (0, f)),
            w_spec(wu, (D_MODEL, tf), lambda i, f: (0, f)),
            w_spec(wd, (tf, D_MODEL), lambda i, f: (f, 0)),
            pl.BlockSpec((1, D_MODEL), lambda i, f: (0, 0)),
            pl.BlockSpec((1, D_MODEL), lambda i, f: (0, 0)),
        ],
        out_specs=out_specs,
        scratch_shapes=[pltpu.VMEM((FFN_TM, D_MODEL), BF16)],
        compiler_params=pltpu.CompilerParams(
            dimension_semantics=("parallel", "arbitrary"), vmem_limit_bytes=VMEM_LIMIT_BIG),
        name="ffn_ln_export" if export else "ffn_ln",
    )(x, wg, wu, wd, ln_g, ln_b)
    return out if export else out[0]


def _in_proj_kernel(x_ref, w_ref, o_ref, *rest, export, blocks_per_group):
    xb_ref = rest[-1]
    j = pl.program_id(1)
    tm = o_ref.shape[0]

    @pl.when(j == 0)
    def _():
        xb_ref[...] = x_ref[...].astype(BF16)

    def emit(act):
        if export:
            w = w_ref[...].astype(BF16)
            rest[0][...] = w
        else:
            w = w_ref[...]
        if act is None:
            o_ref[...] = jnp.dot(xb_ref[...], w, preferred_element_type=F32)
            return
        for rows in [slice(r * tm // 4, (r + 1) * tm // 4) for r in range(4)]:
            o_ref[rows, :] = act(jnp.dot(xb_ref[rows, :], w, preferred_element_type=F32))

    group = lax.div(j, blocks_per_group)
    is_gelu, is_swish = group == GELU_GROUP, group == SWISH_GROUP
    pl.when(is_gelu)(lambda: emit(jax.nn.gelu))
    pl.when(is_swish)(lambda: emit(lambda t: t * _sigmoid(t)))
    pl.when(jnp.logical_not(jnp.logical_or(is_gelu, is_swish)))(lambda: emit(None))


def _in_proj(x1, w_in, *, export=False):
    m = x1.shape[0]
    tm, tn = (m, PROJ_TN_EXPORT) if export else (PROJ_TM, PROJ_TN)
    assert not export or m == tm
    out_shape = [jax.ShapeDtypeStruct((m, D_IN), F32)]
    out_specs = [pl.BlockSpec((tm, tn), lambda i, j: (i, j))]
    x_mode = {}
    if export:
        out_shape.append(jax.ShapeDtypeStruct(w_in.shape, BF16))
        out_specs.append(pl.BlockSpec((D_MODEL, tn), lambda i, j: (0, j)))
        x_mode = dict(pipeline_mode=pl.Buffered(1))
    out = pl.pallas_call(
        functools.partial(_in_proj_kernel, export=export, blocks_per_group=D_RNN // tn),
        out_shape=out_shape,
        grid=(m // tm, D_IN // tn),
        in_specs=[
            pl.BlockSpec((tm, D_MODEL), lambda i, j: (i, 0), **x_mode),
            pl.BlockSpec((D_MODEL, tn), lambda i, j: (0, j)),
        ],
        out_specs=out_specs,
        scratch_shapes=[pltpu.VMEM((tm, D_MODEL), BF16)],
        compiler_params=pltpu.CompilerParams(
            dimension_semantics=("parallel", "arbitrary"), vmem_limit_bytes=VMEM_LIMIT),
        name="in_proj_export" if export else "in_proj",
    )(x1, w_in)
    return out if export else out[0]


def _mixer_kernel(*refs, is_prompt, n_cast):
    n_chunks = MIX_R // CHUNK
    (xa_ref, ga_ref, q_ref, k_ref, v_ref, g_ref, cos_ref, sin_ref,
     dmask_ref, qdec_ref, kdec_ref, cdec_ref,
     convw_ref, convb_ref, wri_ref, bri_ref, lam_ref, gng_ref, gnb_ref) = refs[:19]
    refs = refs[19:]
    if is_prompt:
        conv0_ref = h0_ref = s0_ref = None
    else:
        conv0_ref, h0_ref, s0_ref = refs[:3]
        refs = refs[3:]
    cast_src, refs = refs[:n_cast], refs[n_cast:]
    ya_ref, yb_ref, convo_ref, ho_ref, so_ref = refs[:5]
    cast_dst, refs = refs[5:5 + n_cast], refs[5 + n_cast:]
    (xpad_ref, xc_ref, a_ref, u_ref, hs_ref,
     qb_ref, kb_ref, qd_ref, kd_ref, vb_ref, sc_ref, oh_ref) = refs

    for src, dst in zip(cast_src, cast_dst):
        dst[...] = src[...].astype(BF16)

    if is_prompt:
        @pl.when(pl.program_id(1) == 0)
        def _():
            convo_ref[...] = jnp.zeros_like(convo_ref)
            ho_ref[...] = jnp.zeros_like(ho_ref)
            so_ref[...] = jnp.zeros_like(so_ref)

    n_seg = 1 if is_prompt else n_chunks
    seg_len = MIX_R // n_seg
    conv_b = convb_ref[...]
    w0, w1, w2, w3 = (convw_ref[j:j + 1, :] for j in range(CONV_W))
    lo = CONV_PAD - (CONV_W - 1)
    for sg in range(n_seg):
        rows = slice(sg * seg_len, (sg + 1) * seg_len)
        xpad_ref[sg, lo:CONV_PAD, :] = convo_ref[0] if is_prompt else conv0_ref[sg]
        xpad_ref[sg, CONV_PAD:CONV_PAD + seg_len, :] = xa_ref[rows, :]
        acc = conv_b + w0 * xpad_ref[sg, lo:lo + seg_len, :]
        acc = acc + w1 * xpad_ref[sg, lo + 1:lo + 1 + seg_len, :]
        acc = acc + w2 * xpad_ref[sg, lo + 2:lo + 2 + seg_len, :]
        acc = acc + w3 * xpad_ref[sg, lo + 3:lo + 3 + seg_len, :]
        xc_ref[rows, :] = acc
        convo_ref[sg] = xpad_ref[sg, lo + seg_len:CONV_PAD + seg_len, :]

    xc = xc_ref[...]
    ri = jnp.dot(xc.astype(BF16), wri_ref[...], preferred_element_type=F32) + bri_ref[...]
    r = _sigmoid(ri[:, :D_RNN])
    i = _sigmoid(ri[:, D_RNN:])
    log_a = -LRU_C * r * jax.nn.softplus(-lam_ref[...])
    a = jnp.exp(log_a)
    a_ref[...] = a
    w = jnp.tanh(-log_a) * (1.0 + a * a)
    u_ref[...] = jnp.where(w > 0.0, w * lax.rsqrt(w), 0.0) * (i * xc)

    for sg in range(n_seg):
        h_init = ho_ref[0] if is_prompt else h0_ref[sg]

        def step(t, h, base=sg * seg_len):
            row = pl.ds(base + t, 1)
            h = a_ref[row, :] * h + u_ref[row, :]
            hs_ref[row, :] = h
            return h

        ho_ref[sg] = lax.fori_loop(0, seg_len, step, h_init, unroll=True)

    ya_ref[...] = (hs_ref[...] * ga_ref[...]).astype(BF16)

    k_scale = RET_DK ** -0.5

    blk = MIX_R if is_prompt else CHUNK
    head_cols = [slice(hd * RET_DK, (hd + 1) * RET_DK) for hd in range(RET_HEADS)]
    blk_rows = [slice(c * blk, (c + 1) * blk) for c in range(MIX_R // blk)]
    cos, sin = cos_ref[...], sin_ref[...]

    for hd, cols in enumerate(head_cols):
        qh, kh = q_ref[:, cols], k_ref[:, cols]
        qr = qh * cos + pltpu.roll(qh, RET_DK // 2, 1) * sin
        kr = (kh * cos + pltpu.roll(kh, RET_DK // 2, 1) * sin) * k_scale
        qb_ref[:, cols] = qr.astype(BF16)
        kb_ref[:, cols] = kr.astype(BF16)
        qd_ref[:, cols] = (qr * qdec_ref[hd]).astype(BF16)
        kd_ref[:, cols] = (kr * kdec_ref[hd]).astype(BF16)
        vb_ref[:, cols] = v_ref[:, cols].astype(BF16)

    for rows in blk_rows:
        for hd, cols in enumerate(head_cols):
            s = lax.dot_general(qb_ref[rows, cols], kb_ref[rows, cols], (((1,), (1,)), ((), ())),
                                preferred_element_type=F32) * dmask_ref[hd]
            sc_ref[hd, rows, :] = s.astype(BF16)

    for c, rows in enumerate(blk_rows):
        for hd, cols in enumerate(head_cols):
            state = so_ref[0, hd] if is_prompt else s0_ref[c, hd]
            vb = vb_ref[rows, cols]
            o = jnp.dot(sc_ref[hd, rows, :], vb, preferred_element_type=F32)
            o = o + jnp.dot(qd_ref[rows, cols], state.astype(BF16), preferred_element_type=F32)
            oh_ref[rows, cols] = o
            new_state = cdec_ref[hd] * state + lax.dot_general(
                kd_ref[rows, cols], vb, (((0,), (0,)), ((), ())), preferred_element_type=F32)
            if is_prompt:
                so_ref[0, hd] = new_state
            else:
                so_ref[c, hd] = new_state

    for cols in head_cols:
        o = oh_ref[:, cols]
        mu = jnp.mean(o, axis=-1, keepdims=True)
        oc = o - mu
        var = jnp.mean(oc * oc, axis=-1, keepdims=True)
        on = oc * lax.rsqrt(var + GN_EPS) * gng_ref[:, cols] + gnb_ref[:, cols]
        yb_ref[:, cols] = (g_ref[:, cols] * on).astype(BF16)


def _mixer(z, tables, params, states, *, is_prompt, cast_srcs=()):
    cos_t, sin_t, dmask, qdec, kdec, cdec = tables
    n_chunks = MIX_R // CHUNK
    if is_prompt:
        tiles_per_seq = SEQ // MIX_R
        grid = (BATCH, tiles_per_seq)
        n_seq = BATCH
        n_state = 1
        row_blk = lambda s, t: s * tiles_per_seq + t
        zspec = lambda col: pl.BlockSpec((MIX_R, D_RNN), lambda s, t, col=col: (row_blk(s, t), col))
        tab_spec = pl.BlockSpec((MIX_R, RET_DK), lambda s, t: (t, 0))
        state_idx = lambda s, t: s
        out_row = lambda s, t: (row_blk(s, t), 0)
        m_rows = M_PROMPT
        sem = ("parallel", "arbitrary")
    else:
        grid = (M_SAMPLE // MIX_R,)
        n_seq = DEC_BATCH
        n_state = n_chunks
        zspec = lambda col: pl.BlockSpec((MIX_R, D_RNN), lambda i, col=col: (i, col))
        tab_spec = pl.BlockSpec((MIX_R, RET_DK), lambda i: (0, 0))
        state_idx = lambda i: i
        out_row = lambda i: (i, 0)
        m_rows = M_SAMPLE
        sem = ("parallel",)

    def const_spec(shape):
        nd = len(shape)
        return pl.BlockSpec(shape, lambda *_: (0,) * nd)

    def state_spec(shape):
        nd = len(shape)
        return pl.BlockSpec((n_state,) + shape, lambda *g: (state_idx(*g),) + (0,) * nd)

    in_specs = [zspec(c) for c in range(6)] + [tab_spec, tab_spec] + [
        const_spec(dmask.shape), const_spec(qdec.shape), const_spec(kdec.shape), const_spec(cdec.shape)
    ] + [const_spec(p.shape) for p in params]
    args = [z] * 6 + [cos_t, sin_t, dmask, qdec, kdec, cdec] + list(params)
    if not is_prompt:
        in_specs += [state_spec((CONV_W - 1, D_RNN)), state_spec((1, D_RNN)),
                     state_spec((RET_HEADS, RET_DK, RET_DV))]
        args += list(states)

    n_steps = 1
    for g in grid:
        n_steps *= g
    step_idx = (lambda s, t: (s * grid[1] + t, 0)) if is_prompt else (lambda i: (i, 0))
    cast_specs = [pl.BlockSpec((w.shape[0] // n_steps, w.shape[1]), step_idx) for w in cast_srcs]
    in_specs += cast_specs
    args += list(cast_srcs)

    n_seg = 1 if is_prompt else n_chunks
    seg_len = MIX_R // n_seg
    out_shape = [
        jax.ShapeDtypeStruct((m_rows, D_RNN), BF16),
        jax.ShapeDtypeStruct((m_rows, D_RET_V), BF16),
        jax.ShapeDtypeStruct((n_seq, CONV_W - 1, D_RNN), F32),
        jax.ShapeDtypeStruct((n_seq, 1, D_RNN), F32),
        jax.ShapeDtypeStruct((n_seq, RET_HEADS, RET_DK, RET_DV), F32),
    ]
    out_specs = [
        pl.BlockSpec((MIX_R, D_RNN), out_row),
        pl.BlockSpec((MIX_R, D_RET_V), out_row),
        state_spec((CONV_W - 1, D_RNN)),
        state_spec((1, D_RNN)),
        state_spec((RET_HEADS, RET_DK, RET_DV)),
    ] + cast_specs
    out_shape += [jax.ShapeDtypeStruct(w.shape, BF16) for w in cast_srcs]
    scratch = [
        pltpu.VMEM((n_seg, CONV_PAD + seg_len, D_RNN), F32),
        pltpu.VMEM((MIX_R, D_RNN), F32),
        pltpu.VMEM((MIX_R, D_RNN), F32),
        pltpu.VMEM((MIX_R, D_RNN), F32),
        pltpu.VMEM((MIX_R, D_RNN), F32),
    ] + [pltpu.VMEM((MIX_R, D_RET_K), BF16)] * 5 + [
        pltpu.VMEM((RET_HEADS, MIX_R, MIX_R if is_prompt else CHUNK), BF16),
        pltpu.VMEM((MIX_R, D_RET_V), F32),
    ]
    return pl.pallas_call(
        functools.partial(_mixer_kernel, is_prompt=is_prompt, n_cast=len(cast_srcs)),
        out_shape=out_shape,
        grid=grid,
        in_specs=in_specs,
        out_specs=out_specs,
        scratch_shapes=scratch,
        compiler_params=pltpu.CompilerParams(dimension_semantics=sem, vmem_limit_bytes=VMEM_LIMIT),
        name="mixer_prompt" if is_prompt else "mixer_sample",
    )(*args)


def _merge_ln_kernel(ya_ref, yb_ref, ga_ref, gb_ref, x_ref, wa_ref, wb_ref, wo_ref, g_ref, b_ref, *rest):
    n_cast = (len(rest) - 1) // 2
    o_ref = rest[n_cast]
    for src, dst in zip(rest[:n_cast], rest[n_cast + 1:]):
        dst[...] = src[...].astype(BF16)

    for r in range(MERGE_TM // MERGE_SUB):
        rows = slice(r * MERGE_SUB, (r + 1) * MERGE_SUB)
        pa = jnp.dot(ya_ref[rows, :], wa_ref[...], preferred_element_type=F32)
        pb = jnp.dot(yb_ref[rows, :], wb_ref[...], preferred_element_type=F32)
        merged = jax.nn.sigmoid(ga_ref[rows, :]) * pa + jax.nn.sigmoid(gb_ref[rows, :]) * pb
        mo = jnp.dot(merged.astype(BF16), wo_ref[...], preferred_element_type=F32)
        o_ref[rows, :] = _layer_norm_rows(DN_ALPHA * x_ref[rows, :] + mo, g_ref[...], b_ref[...])


def _merge_ln(ya, yb, z, x1, wa, wb, wo, ln_g, ln_b, cast_srcs=()):
    m = x1.shape[0]
    n_steps = m // MERGE_TM
    gate_a_blk = (2 * D_RNN + 2 * D_RET_K + 2 * D_RET_V) // D_MODEL
    row = lambda i: (i, 0)
    const = lambda i: (0, 0)
    cast_specs = [pl.BlockSpec((w.shape[0] // n_steps, w.shape[1]), row) for w in cast_srcs]
    out = pl.pallas_call(
        _merge_ln_kernel,
        out_shape=[jax.ShapeDtypeStruct((m, D_MODEL), F32)]
        + [jax.ShapeDtypeStruct(w.shape, BF16) for w in cast_srcs],
        grid=(n_steps,),
        in_specs=[
            pl.BlockSpec((MERGE_TM, D_RNN), row),
            pl.BlockSpec((MERGE_TM, D_RET_V), row),
            pl.BlockSpec((MERGE_TM, D_MODEL), lambda i: (i, gate_a_blk)),
            pl.BlockSpec((MERGE_TM, D_MODEL), lambda i: (i, gate_a_blk + 1)),
            pl.BlockSpec((MERGE_TM, D_MODEL), row),
            pl.BlockSpec((D_RNN, D_MODEL), const, pipeline_mode=pl.Buffered(1)),
            pl.BlockSpec((D_RET_V, D_MODEL), const, pipeline_mode=pl.Buffered(1)),
            pl.BlockSpec((D_MODEL, D_MODEL), const, pipeline_mode=pl.Buffered(1)),
            pl.BlockSpec((1, D_MODEL), const),
            pl.BlockSpec((1, D_MODEL), const),
        ] + cast_specs,
        out_specs=[pl.BlockSpec((MERGE_TM, D_MODEL), row)] + cast_specs,
        compiler_params=pltpu.CompilerParams(
            dimension_semantics=("parallel",), vmem_limit_bytes=VMEM_LIMIT),
        name="merge_ln",
    )(ya, yb, z, z, x1, wa, wb, wo, ln_g, ln_b, *cast_srcs)
    return out if cast_srcs else out[0]


def _rope_tables(pos):
    d = RET_DK
    inv_freq = ROPE_BASE ** (-jnp.arange(0, d, 2, dtype=F32) / d)
    ang = pos.astype(F32)[:, None] * inv_freq[None, :]
    cos, sin = jnp.cos(ang), jnp.sin(ang)
    return jnp.concatenate([cos, cos], axis=-1), jnp.concatenate([-sin, sin], axis=-1)


def _decay_tables(blk):
    log_g = jnp.log1p(-jnp.exp2(-5.0 - jnp.arange(RET_HEADS, dtype=F32)))
    idx = jnp.arange(blk, dtype=F32)
    diff = idx[:, None] - idx[None, :]
    dmask = jnp.where(diff >= 0, jnp.exp(log_g[:, None, None] * jnp.maximum(diff, 0.0)), 0.0)
    q_dec = jnp.exp(log_g[:, None] * (idx[None, :] + 1.0))
    k_dec = jnp.exp(log_g[:, None] * (blk - 1.0 - idx[None, :]))
    chunk_dec = jnp.exp(log_g * blk)
    lanes = (RET_HEADS, blk, RET_DK)
    return (dmask, jnp.broadcast_to(q_dec[:, :, None], lanes), jnp.broadcast_to(k_dec[:, :, None], lanes),
            jnp.broadcast_to(chunk_dec[:, None, None], (RET_HEADS, 1, RET_DV)))


def _block_diag(w):
    tiled = jnp.tile(w.reshape(D_RNN, LRU_BLOCK), (1, LRU_BLOCKS))
    row_blk = lax.broadcasted_iota(jnp.int32, (D_RNN, D_RNN), 0) // LRU_BLOCK
    col_blk = lax.broadcasted_iota(jnp.int32, (D_RNN, D_RNN), 1) // LRU_BLOCK
    return jnp.where(row_blk == col_blk, tiled, 0.0)


def kernel(x_prompt, x_sample, state_conv, state_lru, state_ret, ffn1_w_gate, ffn1_w_up, ffn1_w_down, ln1_g, ln1_b, w_in, conv_w, conv_b, rg_w, rg_b, ig_w, ig_b, lru_lambda, ret_gn_g, ret_gn_b, w_a_proj, w_b_proj, w_o, ln2_g, ln2_b, ffn2_w_gate, ffn2_w_up, ffn2_w_down, ln3_g, ln3_b):
    assert DEPTH == 1
    w_ri = jnp.concatenate([_block_diag(rg_w[0]), _block_diag(ig_w[0])], axis=1).astype(BF16)
    b_ri = jnp.concatenate([rg_b, ig_b], axis=1)
    params = (conv_w[0], conv_b, w_ri, b_ri, lru_lambda, ret_gn_g, ret_gn_b)
    rope = _rope_tables(jnp.arange(PAST_LEN + DEC_SEQ, dtype=jnp.int32))
    x_p = x_prompt.reshape(M_PROMPT, D_MODEL)
    x_s = x_sample.reshape(M_SAMPLE, D_MODEL)
    states = (state_conv[0], state_lru[0][:, None, :], state_ret[0])

    x1_s, *ffn1_w = _ffn_ln(x_s, ffn1_w_gate[0], ffn1_w_up[0], ffn1_w_down[0], ln1_g, ln1_b, export=True)
    x1_p = _ffn_ln(x_p, *ffn1_w, ln1_g, ln1_b)
    z_s, w_in_b = _in_proj(x1_s, w_in[0], export=True)
    z_p = _in_proj(x1_p, w_in_b)

    ya_p, yb_p, conv_p, lru_p, ret_p, wd2, *merge_w = _mixer(
        z_p, rope + _decay_tables(MIX_R), params, None, is_prompt=True,
        cast_srcs=(ffn2_w_down[0], w_a_proj[0], w_b_proj[0], w_o[0]))
    reps = MIX_R // CHUNK
    dmask_s, qdec_s, kdec_s, cdec_s = _decay_tables(CHUNK)
    tables_s = tuple(jnp.tile(t[PAST_LEN:], (reps, 1)) for t in rope) + (
        dmask_s, jnp.tile(qdec_s, (1, reps, 1)), jnp.tile(kdec_s, (1, reps, 1)), cdec_s)
    ya_s, yb_s, conv_s, lru_s, ret_s = _mixer(z_s, tables_s, params, states, is_prompt=False)

    x2_p, wg2, wu2 = _merge_ln(ya_p, yb_p, z_p, x1_p, *merge_w, ln2_g, ln2_b,
                               cast_srcs=(ffn2_w_gate[0], ffn2_w_up[0]))
    x2_s = _merge_ln(ya_s, yb_s, z_s, x1_s, *merge_w, ln2_g, ln2_b)
    y_p = _ffn_ln(x2_p, wg2, wu2, wd2, ln3_g, ln3_b)
    y_s = _ffn_ln(x2_s, wg2, wu2, wd2, ln3_g, ln3_b)

    return (y_p.reshape(BATCH, SEQ, D_MODEL), y_s.reshape(DEC_BATCH, DEC_SEQ, D_MODEL),
            conv_p[None], lru_p.reshape(1, BATCH, D_RNN), ret_p[None],
            conv_s[None], lru_s.reshape(1, DEC_BATCH, D_RNN), ret_s[None])
```

```python
import functools

import jax
import jax.numpy as jnp
from jax import lax
from jax.experimental import pallas as pl
from jax.experimental.pallas import tpu as pltpu

F32 = jnp.float32
BF16 = jnp.bfloat16

D_MODEL = 2048
BATCH = 4
SEQ = 2048
DEPTH = 1
DEC_BATCH = 32
DEC_SEQ = 64
PAST_LEN = 2048
CHUNK = 64
D_RNN = 1024
LRU_BLOCKS = 16
LRU_BLOCK = D_RNN // LRU_BLOCKS
CONV_W = 4
LRU_C = 8.0
RET_HEADS = 8
RET_DK = 128
RET_DV = 128
D_RET_K = RET_HEADS * RET_DK
D_RET_V = RET_HEADS * RET_DV
D_FF = 5632
DN_ALPHA = (2.0 * DEPTH) ** 0.25
LN_EPS = 1e-5
GN_EPS = 1e-5
ROPE_BASE = 10000.0
D_IN = 2 * D_RNN + 2 * D_RET_K + 2 * D_RET_V + 2 * D_MODEL

M_PROMPT = BATCH * SEQ
M_SAMPLE = DEC_BATCH * DEC_SEQ

V7X_VMEM_BYTES = 64 * 1024 * 1024
VMEM_LIMIT = V7X_VMEM_BYTES - 8 * 1024 * 1024
VMEM_LIMIT_BIG = V7X_VMEM_BYTES - 3 * 1024 * 1024

FFN_TM = 1024
FFN_SUB = 512
FFN_SUB_LAST = 256
FFN_TF = 512
FFN_TF_EXPORT = 256
LN_ROWS = 128
PROJ_TM = 1024
PROJ_TN = 1024
PROJ_TN_EXPORT = 512
MIX_R = 256
MERGE_TM = 256
MERGE_SUB = 128
GELU_GROUP = 1
SWISH_GROUP = 5
CONV_PAD = 8


def _sigmoid(x):
    return 0.5 * jnp.tanh(0.5 * x) + 0.5


def _layer_norm_rows(y, g, b):
    mu = jnp.mean(y, axis=-1, keepdims=True)
    yc = y - mu
    var = jnp.mean(yc * yc, axis=-1, keepdims=True)
    return yc * lax.rsqrt(var + LN_EPS) * g + b


def _ffn_ln_kernel(x_ref, wg_ref, wu_ref, wd_ref, g_ref, b_ref, o_ref, *rest, export):
    f = pl.program_id(1)
    last = pl.num_programs(1) - 1
    xb_ref = rest[-1]

    def sub_tiles(size):
        return [slice(r * size, (r + 1) * size) for r in range(FFN_TM // size)]

    def weights():
        if not export:
            return wg_ref[...], wu_ref[...], wd_ref[...]
        ws = tuple(w[...].astype(BF16) for w in (wg_ref, wu_ref, wd_ref))
        for dst, w in zip(rest[:3], ws):
            dst[...] = w
        return ws

    def seed(rows):
        x = x_ref[rows, :]
        o_ref[rows, :] = (2.0 * DN_ALPHA) * x
        xb_ref[rows, :] = x.astype(BF16)

    def accumulate(rows, ws):
        wg, wu, wd = ws
        xb = xb_ref[rows, :]
        gate = jnp.dot(xb, wg, preferred_element_type=F32)
        up = jnp.dot(xb, wu, preferred_element_type=F32)
        h = (gate * _sigmoid(gate) * up).astype(BF16)
        o_ref[rows, :] += jnp.dot(h, wd, preferred_element_type=F32)

    def finish(rows):
        for c in range(rows.start, rows.stop, LN_ROWS):
            chunk = slice(c, c + LN_ROWS)
            acc = o_ref[chunk, :]
            mu = jnp.mean(acc, axis=-1, keepdims=True)
            ac = acc - mu
            var = jnp.mean(ac * ac, axis=-1, keepdims=True)
            scale = 0.5 * lax.rsqrt(0.25 * var + LN_EPS)
            o_ref[chunk, :] = ac * scale * g_ref[...] + b_ref[...]

    @pl.when(f == 0)
    def _():
        ws = weights()
        for rows in sub_tiles(FFN_SUB):
            seed(rows)
            accumulate(rows, ws)

    @pl.when(jnp.logical_and(f > 0, f < last))
    def _():
        ws = weights()
        for rows in sub_tiles(FFN_SUB):
            accumulate(rows, ws)

    @pl.when(f == last)
    def _():
        ws = weights()
        for rows in sub_tiles(FFN_SUB_LAST):
            accumulate(rows, ws)
            finish(rows)


def _ffn_ln(x, wg, wu, wd, ln_g, ln_b, *, export=False):
    m = x.shape[0]
    n_tiles = m // FFN_TM
    tf = FFN_TF_EXPORT if export else FFN_TF
    grid = (n_tiles, D_FF // tf)
    out_shape = [jax.ShapeDtypeStruct((m, D_MODEL), F32)]
    out_specs = [pl.BlockSpec((FFN_TM, D_MODEL), lambda i, f: (i, 0))]
    x_mode = {}
    if export:
        out_shape += [jax.ShapeDtypeStruct((n_tiles,) + w.shape, BF16) for w in (wg, wu, wd)]
        out_specs += [
            pl.BlockSpec((None, D_MODEL, tf), lambda i, f: (i, 0, f)),
            pl.BlockSpec((None, D_MODEL, tf), lambda i, f: (i, 0, f)),
            pl.BlockSpec((None, tf, D_MODEL), lambda i, f: (i, f, 0)),
        ]
        x_mode = dict(pipeline_mode=pl.Buffered(1))

    def w_spec(w, block, idx):
        if w.ndim == 2:
            return pl.BlockSpec(block, idx)
        return pl.BlockSpec((None,) + block, lambda i, f: (0,) + idx(i, f))

    out = pl.pallas_call(
        functools.partial(_ffn_ln_kernel, export=export),
        out_shape=out_shape,
        grid=grid,
        in_specs=[
            pl.BlockSpec((FFN_TM, D_MODEL), lambda i, f: (i, 0), **x_mode),
            w_spec(wg, (D_MODEL, tf), lambda i, f: (0, f)),
            w_spec(wu, (D_MODEL, tf), lambda i, f: (0, f)),
            w_spec(wd, (tf, D_MODEL), lambda i, f: (f, 0)),
            pl.BlockSpec((1, D_MODEL), lambda i, f: (0, 0)),
            pl.BlockSpec((1, D_MODEL), lambda i, f: (0, 0)),
        ],
        out_specs=out_specs,
        scratch_shapes=[pltpu.VMEM((FFN_TM, D_MODEL), BF16)],
        compiler_params=pltpu.CompilerParams(
            dimension_semantics=("parallel", "arbitrary"), vmem_limit_bytes=VMEM_LIMIT_BIG),
        name="ffn_ln_export" if export else "ffn_ln",
    )(x, wg, wu, wd, ln_g, ln_b)
    return out if export else out[0]


def _in_proj_kernel(x_ref, w_ref, o_ref, *rest, export, blocks_per_group):
    xb_ref = rest[-1]
    j = pl.program_id(1)
    tm = o_ref.shape[0]

    @pl.when(j == 0)
    def _():
        xb_ref[...] = x_ref[...].astype(BF16)

    def emit(act):
        if export:
            w = w_ref[...].astype(BF16)
            rest[0][...] = w
        else:
            w = w_ref[...]
        if act is None:
            o_ref[...] = jnp.dot(xb_ref[...], w, preferred_element_type=F32)
            return
        for rows in [slice(r * tm // 4, (r + 1) * tm // 4) for r in range(4)]:
            o_ref[rows, :] = act(jnp.dot(xb_ref[rows, :], w, preferred_element_type=F32))

    group = lax.div(j, blocks_per_group)
    is_gelu, is_swish = group == GELU_GROUP, group == SWISH_GROUP
    pl.when(is_gelu)(lambda: emit(jax.nn.gelu))
    pl.when(is_swish)(lambda: emit(lambda t: t * _sigmoid(t)))
    pl.when(jnp.logical_not(jnp.logical_or(is_gelu, is_swish)))(lambda: emit(None))


def _in_proj(x1, w_in, *, export=False):
    m = x1.shape[0]
    tm, tn = (m, PROJ_TN_EXPORT) if export else (PROJ_TM, PROJ_TN)
    assert not export or m == tm
    out_shape = [jax.ShapeDtypeStruct((m, D_IN), F32)]
    out_specs = [pl.BlockSpec((tm, tn), lambda i, j: (i, j))]
    x_mode = {}
    if export:
        out_shape.append(jax.ShapeDtypeStruct(w_in.shape, BF16))
        out_specs.append(pl.BlockSpec((D_MODEL, tn), lambda i, j: (0, j)))
        x_mode = dict(pipeline_mode=pl.Buffered(1))
    out = pl.pallas_call(
        functools.partial(_in_proj_kernel, export=export, blocks_per_group=D_RNN // tn),
        out_shape=out_shape,
        grid=(m // tm, D_IN // tn),
        in_specs=[
            pl.BlockSpec((tm, D_MODEL), lambda i, j: (i, 0), **x_mode),
            pl.BlockSpec((D_MODEL, tn), lambda i, j: (0, j)),
        ],
        out_specs=out_specs,
        scratch_shapes=[pltpu.VMEM((tm, D_MODEL), BF16)],
        compiler_params=pltpu.CompilerParams(
            dimension_semantics=("parallel", "arbitrary"), vmem_limit_bytes=VMEM_LIMIT),
        name="in_proj_export" if export else "in_proj",
    )(x1, w_in)
    return out if export else out[0]


def _mixer_kernel(*refs, is_prompt, n_cast):
    n_chunks = MIX_R // CHUNK
    (xa_ref, ga_ref, q_ref, k_ref, v_ref, g_ref, cos_ref, sin_ref,
     dmask_ref, qdec_ref, kdec_ref, cdec_ref,
     convw_ref, convb_ref, wri_ref, bri_ref, lam_ref, gng_ref, gnb_ref) = refs[:19]
    refs = refs[19:]
    if is_prompt:
        conv0_ref = h0_ref = s0_ref = None
    else:
        conv0_ref, h0_ref, s0_ref = refs[:3]
        refs = refs[3:]
    cast_src, refs = refs[:n_cast], refs[n_cast:]
    ya_ref, yb_ref, convo_ref, ho_ref, so_ref = refs[:5]
    cast_dst, refs = refs[5:5 + n_cast], refs[5 + n_cast:]
    (xpad_ref, xc_ref, a_ref, u_ref, hs_ref,
     qb_ref, kb_ref, qd_ref, kd_ref, vb_ref, sc_ref, oh_ref) = refs

    for src, dst in zip(cast_src, cast_dst):
        dst[...] = src[...].astype(BF16)

    if is_prompt:
        @pl.when(pl.program_id(1) == 0)
        def _():
            convo_ref[...] = jnp.zeros_like(convo_ref)
            ho_ref[...] = jnp.zeros_like(ho_ref)
            so_ref[...] = jnp.zeros_like(so_ref)

    n_seg = 1 if is_prompt else n_chunks
    seg_len = MIX_R // n_seg
    conv_b = convb_ref[...]
    w0, w1, w2, w3 = (convw_ref[j:j + 1, :] for j in range(CONV_W))
    lo = CONV_PAD - (CONV_W - 1)
    for sg in range(n_seg):
        rows = slice(sg * seg_len, (sg + 1) * seg_len)
        xpad_ref[sg, lo:CONV_PAD, :] = convo_ref[0] if is_prompt else conv0_ref[sg]
        xpad_ref[sg, CONV_PAD:CONV_PAD + seg_len, :] = xa_ref[rows, :]
        acc = conv_b + w0 * xpad_ref[sg, lo:lo + seg_len, :]
        acc = acc + w1 * xpad_ref[sg, lo + 1:lo + 1 + seg_len, :]
        acc = acc + w2 * xpad_ref[sg, lo + 2:lo + 2 + seg_len, :]
        acc = acc + w3 * xpad_ref[sg, lo + 3:lo + 3 + seg_len, :]
        xc_ref[rows, :] = acc
        convo_ref[sg] = xpad_ref[sg, lo + seg_len:CONV_PAD + seg_len, :]

    xc = xc_ref[...]
    ri = jnp.dot(xc.astype(BF16), wri_ref[...], preferred_element_type=F32) + bri_ref[...]
    r = _sigmoid(ri[:, :D_RNN])
    i = _sigmoid(ri[:, D_RNN:])
    log_a = -LRU_C * r * jax.nn.softplus(-lam_ref[...])
    a = jnp.exp(log_a)
    a_ref[...] = a
    w = jnp.tanh(-log_a) * (1.0 + a * a)
    u_ref[...] = jnp.where(w > 0.0, w * lax.rsqrt(w), 0.0) * (i * xc)

    for sg in range(n_seg):
        h_init = ho_ref[0] if is_prompt else h0_ref[sg]

        def step(t, h, base=sg * seg_len):
            row = pl.ds(base + t, 1)
            h = a_ref[row, :] * h + u_ref[row, :]
            hs_ref[row, :] = h
            return h

        ho_ref[sg] = lax.fori_loop(0, seg_len, step, h_init, unroll=True)

    ya_ref[...] = (hs_ref[...] * ga_ref[...]).astype(BF16)

    k_scale = RET_DK ** -0.5

    blk = MIX_R if is_prompt else CHUNK
    head_cols = [slice(hd * RET_DK, (hd + 1) * RET_DK) for hd in range(RET_HEADS)]
    blk_rows = [slice(c * blk, (c + 1) * blk) for c in range(MIX_R // blk)]
    cos, sin = cos_ref[...], sin_ref[...]

    for hd, cols in enumerate(head_cols):
        qh, kh = q_ref[:, cols], k_ref[:, cols]
        qr = qh * cos + pltpu.roll(qh, RET_DK // 2, 1) * sin
        kr = (kh * cos + pltpu.roll(kh, RET_DK // 2, 1) * sin) * k_scale
        qb_ref[:, cols] = qr.astype(BF16)
        kb_ref[:, cols] = kr.astype(BF16)
        qd_ref[:, cols] = (qr * qdec_ref[hd]).astype(BF16)
        kd_ref[:, cols] = (kr * kdec_ref[hd]).astype(BF16)
        vb_ref[:, cols] = v_ref[:, cols].astype(BF16)

    for rows in blk_rows:
        for hd, cols in enumerate(head_cols):
            s = lax.dot_general(qb_ref[rows, cols], kb_ref[rows, cols], (((1,), (1,)), ((), ())),
                                preferred_element_type=F32) * dmask_ref[hd]
            sc_ref[hd, rows, :] = s.astype(BF16)

    for c, rows in enumerate(blk_rows):
        for hd, cols in enumerate(head_cols):
            state = so_ref[0, hd] if is_prompt else s0_ref[c, hd]
            vb = vb_ref[rows, cols]
            o = jnp.dot(sc_ref[hd, rows, :], vb, preferred_element_type=F32)
            o = o + jnp.dot(qd_ref[rows, cols], state.astype(BF16), preferred_element_type=F32)
            oh_ref[rows, cols] = o
            new_state = cdec_ref[hd] * state + lax.dot_general(
                kd_ref[rows, cols], vb, (((0,), (0,)), ((), ())), preferred_element_type=F32)
            if is_prompt:
                so_ref[0, hd] = new_state
            else:
                so_ref[c, hd] = new_state

    for cols in head_cols:
        o = oh_ref[:, cols]
        mu = jnp.mean(o, axis=-1, keepdims=True)
        oc = o - mu
        var = jnp.mean(oc * oc, axis=-1, keepdims=True)
        on = oc * lax.rsqrt(var + GN_EPS) * gng_ref[:, cols] + gnb_ref[:, cols]
        yb_ref[:, cols] = (g_ref[:, cols] * on).astype(BF16)


def _mixer(z, tables, params, states, *, is_prompt, cast_srcs=()):
    cos_t, sin_t, dmask, qdec, kdec, cdec = tables
    n_chunks = MIX_R // CHUNK
    if is_prompt:
        tiles_per_seq = SEQ // MIX_R
        grid = (BATCH, tiles_per_seq)
        n_seq = BATCH
        n_state = 1
        row_blk = lambda s, t: s * tiles_per_seq + t
        zspec = lambda col: pl.BlockSpec((MIX_R, D_RNN), lambda s, t, col=col: (row_blk(s, t), col))
        tab_spec = pl.BlockSpec((MIX_R, RET_DK), lambda s, t: (t, 0))
        state_idx = lambda s, t: s
        out_row = lambda s, t: (row_blk(s, t), 0)
        m_rows = M_PROMPT
        sem = ("parallel", "arbitrary")
    else:
        grid = (M_SAMPLE // MIX_R,)
        n_seq = DEC_BATCH
        n_state = n_chunks
        zspec = lambda col: pl.BlockSpec((MIX_R, D_RNN), lambda i, col=col: (i, col))
        tab_spec = pl.BlockSpec((MIX_R, RET_DK), lambda i: (0, 0))
        state_idx = lambda i: i
        out_row = lambda i: (i, 0)
        m_rows = M_SAMPLE
        sem = ("parallel",)

    def const_spec(shape):
        nd = len(shape)
        return pl.BlockSpec(shape, lambda *_: (0,) * nd)

    def state_spec(shape):
        nd = len(shape)
        return pl.BlockSpec((n_state,) + shape, lambda *g: (state_idx(*g),) + (0,) * nd)

    in_specs = [zspec(c) for c in range(6)] + [tab_spec, tab_spec] + [
        const_spec(dmask.shape), const_spec(qdec.shape), const_spec(kdec.shape), const_spec(cdec.shape)
    ] + [const_spec(p.shape) for p in params]
    args = [z] * 6 + [cos_t, sin_t, dmask, qdec, kdec, cdec] + list(params)
    if not is_prompt:
        in_specs += [state_spec((CONV_W - 1, D_RNN)), state_spec((1, D_RNN)),
                     state_spec((RET_HEADS, RET_DK, RET_DV))]
        args += list(states)

    n_steps = 1
    for g in grid:
        n_steps *= g
    step_idx = (lambda s, t: (s * grid[1] + t, 0)) if is_prompt else (lambda i: (i, 0))
    cast_specs = [pl.BlockSpec((w.shape[0] // n_steps, w.shape[1]), step_idx) for w in cast_srcs]
    in_specs += cast_specs
    args += list(cast_srcs)

    n_seg = 1 if is_prompt else n_chunks
    seg_len = MIX_R // n_seg
    out_shape = [
        jax.ShapeDtypeStruct((m_rows, D_RNN), BF16),
        jax.ShapeDtypeStruct((m_rows, D_RET_V), BF16),
        jax.ShapeDtypeStruct((n_seq, CONV_W - 1, D_RNN), F32),
        jax.ShapeDtypeStruct((n_seq, 1, D_RNN), F32),
        jax.ShapeDtypeStruct((n_seq, RET_HEADS, RET_DK, RET_DV), F32),
    ]
    out_specs = [
        pl.BlockSpec((MIX_R, D_RNN), out_row),
        pl.BlockSpec((MIX_R, D_RET_V), out_row),
        state_spec((CONV_W - 1, D_RNN)),
        state_spec((1, D_RNN)),
        state_spec((RET_HEADS, RET_DK, RET_DV)),
    ] + cast_specs
    out_shape += [jax.ShapeDtypeStruct(w.shape, BF16) for w in cast_srcs]
    scratch = [
        pltpu.VMEM((n_seg, CONV_PAD + seg_len, D_RNN), F32),
        pltpu.VMEM((MIX_R, D_RNN), F32),
        pltpu.VMEM((MIX_R, D_RNN), F32),
        pltpu.VMEM((MIX_R, D_RNN), F32),
        pltpu.VMEM((MIX_R, D_RNN), F32),
    ] + [pltpu.VMEM((MIX_R, D_RET_K), BF16)] * 5 + [
        pltpu.VMEM((RET_HEADS, MIX_R, MIX_R if is_prompt else CHUNK), BF16),
        pltpu.VMEM((MIX_R, D_RET_V), F32),
    ]
    return pl.pallas_call(
        functools.partial(_mixer_kernel, is_prompt=is_prompt, n_cast=len(cast_srcs)),
        out_shape=out_shape,
        grid=grid,
        in_specs=in_specs,
        out_specs=out_specs,
        scratch_shapes=scratch,
        compiler_params=pltpu.CompilerParams(dimension_semantics=sem, vmem_limit_bytes=VMEM_LIMIT),
        name="mixer_prompt" if is_prompt else "mixer_sample",
    )(*args)


def _merge_ln_kernel(ya_ref, yb_ref, ga_ref, gb_ref, x_ref, wa_ref, wb_ref, wo_ref, g_ref, b_ref, *rest):
    n_cast = (len(rest) - 1) // 2
    o_ref = rest[n_cast]
    for src, dst in zip(rest[:n_cast], rest[n_cast + 1:]):
        dst[...] = src[...].astype(BF16)

    for r in range(MERGE_TM // MERGE_SUB):
        rows = slice(r * MERGE_SUB, (r + 1) * MERGE_SUB)
        pa = jnp.dot(ya_ref[rows, :], wa_ref[...], preferred_element_type=F32)
        pb = jnp.dot(yb_ref[rows, :], wb_ref[...], preferred_element_type=F32)
        merged = jax.nn.sigmoid(ga_ref[rows, :]) * pa + jax.nn.sigmoid(gb_ref[rows, :]) * pb
        mo = jnp.dot(merged.astype(BF16), wo_ref[...], preferred_element_type=F32)
        o_ref[rows, :] = _layer_norm_rows(DN_ALPHA * x_ref[rows, :] + mo, g_ref[...], b_ref[...])


def _merge_ln(ya, yb, z, x1, wa, wb, wo, ln_g, ln_b, cast_srcs=()):
    m = x1.shape[0]
    n_steps = m // MERGE_TM
    gate_a_blk = (2 * D_RNN + 2 * D_RET_K + 2 * D_RET_V) // D_MODEL
    row = lambda i: (i, 0)
    const = lambda i: (0, 0)
    cast_specs = [pl.BlockSpec((w.shape[0] // n_steps, w.shape[1]), row) for w in cast_srcs]
    out = pl.pallas_call(
        _merge_ln_kernel,
        out_shape=[jax.ShapeDtypeStruct((m, D_MODEL), F32)]
        + [jax.ShapeDtypeStruct(w.shape, BF16) for w in cast_srcs],
        grid=(n_steps,),
        in_specs=[
            pl.BlockSpec((MERGE_TM, D_RNN), row),
            pl.BlockSpec((MERGE_TM, D_RET_V), row),
            pl.BlockSpec((MERGE_TM, D_MODEL), lambda i: (i, gate_a_blk)),
            pl.BlockSpec((MERGE_TM, D_MODEL), lambda i: (i, gate_a_blk + 1)),
            pl.BlockSpec((MERGE_TM, D_MODEL), row),
            pl.BlockSpec((D_RNN, D_MODEL), const, pipeline_mode=pl.Buffered(1)),
            pl.BlockSpec((D_RET_V, D_MODEL), const, pipeline_mode=pl.Buffered(1)),
            pl.BlockSpec((D_MODEL, D_MODEL), const, pipeline_mode=pl.Buffered(1)),
            pl.BlockSpec((1, D_MODEL), const),
            pl.BlockSpec((1, D_MODEL), const),
        ] + cast_specs,
        out_specs=[pl.BlockSpec((MERGE_TM, D_MODEL), row)] + cast_specs,
        compiler_params=pltpu.CompilerParams(
            dimension_semantics=("parallel",), vmem_limit_bytes=VMEM_LIMIT),
        name="merge_ln",
    )(ya, yb, z, z, x1, wa, wb, wo, ln_g, ln_b, *cast_srcs)
    return out if cast_srcs else out[0]


def _rope_tables(pos):
    d = RET_DK
    inv_freq = ROPE_BASE ** (-jnp.arange(0, d, 2, dtype=F32) / d)
    ang = pos.astype(F32)[:, None] * inv_freq[None, :]
    cos, sin = jnp.cos(ang), jnp.sin(ang)
    return jnp.concatenate([cos, cos], axis=-1), jnp.concatenate([-sin, sin], axis=-1)


def _decay_tables(blk):
    log_g = jnp.log1p(-jnp.exp2(-5.0 - jnp.arange(RET_HEADS, dtype=F32)))
    idx = jnp.arange(blk, dtype=F32)
    diff = idx[:, None] - idx[None, :]
    dmask = jnp.where(diff >= 0, jnp.exp(log_g[:, None, None] * jnp.maximum(diff, 0.0)), 0.0)
    q_dec = jnp.exp(log_g[:, None] * (idx[None, :] + 1.0))
    k_dec = jnp.exp(log_g[:, None] * (blk - 1.0 - idx[None, :]))
    chunk_dec = jnp.exp(log_g * blk)
    lanes = (RET_HEADS, blk, RET_DK)
    return (dmask, jnp.broadcast_to(q_dec[:, :, None], lanes), jnp.broadcast_to(k_dec[:, :, None], lanes),
            jnp.broadcast_to(chunk_dec[:, None, None], (RET_HEADS, 1, RET_DV)))


def _block_diag(w):
    tiled = jnp.tile(w.reshape(D_RNN, LRU_BLOCK), (1, LRU_BLOCKS))
    row_blk = lax.broadcasted_iota(jnp.int32, (D_RNN, D_RNN), 0) // LRU_BLOCK
    col_blk = lax.broadcasted_iota(jnp.int32, (D_RNN, D_RNN), 1) // LRU_BLOCK
    return jnp.where(row_blk == col_blk, tiled, 0.0)


def kernel(x_prompt, x_sample, state_conv, state_lru, state_ret, ffn1_w_gate, ffn1_w_up, ffn1_w_down, ln1_g, ln1_b, w_in, conv_w, conv_b, rg_w, rg_b, ig_w, ig_b, lru_lambda, ret_gn_g, ret_gn_b, w_a_proj, w_b_proj, w_o, ln2_g, ln2_b, ffn2_w_gate, ffn2_w_up, ffn2_w_down, ln3_g, ln3_b):
    assert DEPTH == 1
    w_ri = jnp.concatenate([_block_diag(rg_w[0]), _block_diag(ig_w[0])], axis=1).astype(BF16)
    b_ri = jnp.concatenate([rg_b, ig_b], axis=1)
    params = (conv_w[0], conv_b, w_ri, b_ri, lru_lambda, ret_gn_g, ret_gn_b)
    rope = _rope_tables(jnp.arange(PAST_LEN + DEC_SEQ, dtype=jnp.int32))
    x_p = x_prompt.reshape(M_PROMPT, D_MODEL)
    x_s = x_sample.reshape(M_SAMPLE, D_MODEL)
    states = (state_conv[0], state_lru[0][:, None, :], state_ret[0])

    x1_s, *ffn1_w = _ffn_ln(x_s, ffn1_w_gate[0], ffn1_w_up[0], ffn1_w_down[0], ln1_g, ln1_b, export=True)
    x1_p = _ffn_ln(x_p, *ffn1_w, ln1_g, ln1_b)
    z_s, w_in_b = _in_proj(x1_s, w_in[0], export=True)
    z_p = _in_proj(x1_p, w_in_b)

    ya_p, yb_p, conv_p, lru_p, ret_p, wd2, *merge_w = _mixer(
        z_p, rope + _decay_tables(MIX_R), params, None, is_prompt=True,
        cast_srcs=(ffn2_w_down[0], w_a_proj[0], w_b_proj[0], w_o[0]))
    reps = MIX_R // CHUNK
    dmask_s, qdec_s, kdec_s, cdec_s = _decay_tables(CHUNK)
    tables_s = tuple(jnp.tile(t[PAST_LEN:], (reps, 1)) for t in rope) + (
        dmask_s, jnp.tile(qdec_s, (1, reps, 1)), jnp.tile(kdec_s, (1, reps, 1)), cdec_s)
    ya_s, yb_s, conv_s, lru_s, ret_s = _mixer(z_s, tables_s, params, states, is_prompt=False)

    x2_p, wg2, wu2 = _merge_ln(ya_p, yb_p, z_p, x1_p, *merge_w, ln2_g, ln2_b,
                               cast_srcs=(ffn2_w_gate[0], ffn2_w_up[0]))
    x2_s = _merge_ln(ya_s, yb_s, z_s, x1_s, *merge_w, ln2_g, ln2_b)
    y_p = _ffn_ln(x2_p, wg2, wu2, wd2, ln3_g, ln3_b)
    y_s = _ffn_ln(x2_s, wg2, wu2, wd2, ln3_g, ln3_b)

    return (y_p.reshape(BATCH, SEQ, D_MODEL), y_s.reshape(DEC_BATCH, DEC_SEQ, D_MODEL),
            conv_p[None], lru_p.reshape(1, BATCH, D_RNN), ret_p[None],
            conv_s[None], lru_s.reshape(1, DEC_BATCH, D_RNN), ret_s[None])
```

```python
import functools

import jax
import jax.numpy as jnp
from jax import lax
from jax.experimental import pallas as pl
from jax.experimental.pallas import tpu as pltpu

F32 = jnp.float32
BF16 = jnp.bfloat16

D_MODEL = 2048
BATCH = 4
SEQ = 2048
DEPTH = 1
DEC_BATCH = 32
DEC_SEQ = 64
PAST_LEN = 2048
CHUNK = 64
D_RNN = 1024
LRU_BLOCKS = 16
LRU_BLOCK = D_RNN // LRU_BLOCKS
CONV_W = 4
LRU_C = 8.0
RET_HEADS = 8
RET_DK = 128
RET_DV = 128
D_RET_K = RET_HEADS * RET_DK
D_RET_V = RET_HEADS * RET_DV
D_FF = 5632
DN_ALPHA = (2.0 * DEPTH) ** 0.25
LN_EPS = 1e-5
GN_EPS = 1e-5
ROPE_BASE = 10000.0
D_IN = 2 * D_RNN + 2 * D_RET_K + 2 * D_RET_V + 2 * D_MODEL

M_PROMPT = BATCH * SEQ
M_SAMPLE = DEC_BATCH * DEC_SEQ

V7X_VMEM_BYTES = 64 * 1024 * 1024
VMEM_LIMIT = V7X_VMEM_BYTES - 8 * 1024 * 1024
VMEM_LIMIT_BIG = V7X_VMEM_BYTES - 3 * 1024 * 1024

FFN_TM = 1024
FFN_SUB = 512
FFN_SUB_LAST = 256
FFN_TF = 512
FFN_TF_EXPORT = 256
LN_ROWS = 128
PROJ_TM = 1024
PROJ_TN = 1024
PROJ_TN_EXPORT = 512
MIX_R = 256
MERGE_TM = 256
MERGE_SUB = 128
GELU_GROUP = 1
SWISH_GROUP = 5
GATE_GRP = 256
CONV_PAD = 8


def _sigmoid(x):
    return 0.5 * jnp.tanh(0.5 * x) + 0.5


def _layer_norm_rows(y, g, b):
    mu = jnp.mean(y, axis=-1, keepdims=True)
    yc = y - mu
    var = jnp.mean(yc * yc, axis=-1, keepdims=True)
    return yc * lax.rsqrt(var + LN_EPS) * g + b


def _ffn_ln_kernel(x_ref, wg_ref, wu_ref, wd_ref, g_ref, b_ref, o_ref, *rest, export):
    f = pl.program_id(1)
    last = pl.num_programs(1) - 1
    xb_ref = rest[-1]

    def sub_tiles(size):
        return [slice(r * size, (r + 1) * size) for r in range(FFN_TM // size)]

    def weights():
        if not export:
            return wg_ref[...], wu_ref[...], wd_ref[...]
        ws = tuple(w[...].astype(BF16) for w in (wg_ref, wu_ref, wd_ref))
        for dst, w in zip(rest[:3], ws):
            dst[...] = w
        return ws

    def seed(rows):
        x = x_ref[rows, :]
        o_ref[rows, :] = (2.0 * DN_ALPHA) * x
        xb_ref[rows, :] = x.astype(BF16)

    def accumulate(rows, ws):
        wg, wu, wd = ws
        xb = xb_ref[rows, :]
        gate = jnp.dot(xb, wg, preferred_element_type=F32)
        up = jnp.dot(xb, wu, preferred_element_type=F32)
        h = (gate * _sigmoid(gate) * up).astype(BF16)
        o_ref[rows, :] += jnp.dot(h, wd, preferred_element_type=F32)

    def finish(rows):
        for c in range(rows.start, rows.stop, LN_ROWS):
            chunk = slice(c, c + LN_ROWS)
            acc = o_ref[chunk, :]
            mu = jnp.mean(acc, axis=-1, keepdims=True)
            ac = acc - mu
            var = jnp.mean(ac * ac, axis=-1, keepdims=True)
            scale = 0.5 * lax.rsqrt(0.25 * var + LN_EPS)
            o_ref[chunk, :] = ac * scale * g_ref[...] + b_ref[...]

    @pl.when(f == 0)
    def _():
        ws = weights()
        for rows in sub_tiles(FFN_SUB):
            seed(rows)
            accumulate(rows, ws)

    @pl.when(jnp.logical_and(f > 0, f < last))
    def _():
        ws = weights()
        for rows in sub_tiles(FFN_SUB):
            accumulate(rows, ws)

    @pl.when(f == last)
    def _():
        ws = weights()
        for rows in sub_tiles(FFN_SUB_LAST):
            accumulate(rows, ws)
            finish(rows)


def _ffn_ln(x, wg, wu, wd, ln_g, ln_b, *, export=False):
    m = x.shape[0]
    n_tiles = m // FFN_TM
    tf = FFN_TF_EXPORT if export else FFN_TF
    grid = (n_tiles, D_FF // tf)
    out_shape = [jax.ShapeDtypeStruct((m, D_MODEL), F32)]
    out_specs = [pl.BlockSpec((FFN_TM, D_MODEL), lambda i, f: (i, 0))]
    x_mode = {}
    if export:
        out_shape += [jax.ShapeDtypeStruct((n_tiles,) + w.shape, BF16) for w in (wg, wu, wd)]
        out_specs += [
            pl.BlockSpec((None, D_MODEL, tf), lambda i, f: (i, 0, f)),
            pl.BlockSpec((None, D_MODEL, tf), lambda i, f: (i, 0, f)),
            pl.BlockSpec((None, tf, D_MODEL), lambda i, f: (i, f, 0)),
        ]
        x_mode = dict(pipeline_mode=pl.Buffered(1))

    def w_spec(w, block, idx):
        if w.ndim == 2:
            return pl.BlockSpec(block, idx)
        return pl.BlockSpec((None,) + block, lambda i, f: (0,) + idx(i, f))

    out = pl.pallas_call(
        functools.partial(_ffn_ln_kernel, export=export),
        out_shape=out_shape,
        grid=grid,
        in_specs=[
            pl.BlockSpec((FFN_TM, D_MODEL), lambda i, f: (i, 0), **x_mode),
            w_spec(wg, (D_MODEL, tf), lambda i, f: (0, f)),
            w_spec(wu, (D_MODEL, tf), lambda i, f: (0, f)),
            w_spec(wd, (tf, D_MODEL), lambda i, f: (f, 0)),
            pl.BlockSpec((1, D_MODEL), lambda i, f: (0, 0)),
            pl.BlockSpec((1, D_MODEL), lambda i, f: (0, 0)),
        ],
        out_specs=out_specs,
        scratch_shapes=[pltpu.VMEM((FFN_TM, D_MODEL), BF16)],
        compiler_params=pltpu.CompilerParams(
            dimension_semantics=("parallel", "arbitrary"), vmem_limit_bytes=VMEM_LIMIT_BIG),
        name="ffn_ln_export" if export else "ffn_ln",
    )(x, wg, wu, wd, ln_g, ln_b)
    return out if export else out[0]


def _in_proj_kernel(x_ref, w_ref, o_ref, *rest, export, blocks_per_group):
    xb_ref = rest[-1]
    j = pl.program_id(1)
    tm = o_ref.shape[0]

    @pl.when(j == 0)
    def _():
        xb_ref[...] = x_ref[...].astype(BF16)

    def emit(act):
        if export:
            w = w_ref[...].astype(BF16)
            rest[0][...] = w
        else:
            w = w_ref[...]
        if act is None:
            o_ref[...] = jnp.dot(xb_ref[...], w, preferred_element_type=F32)
            return
        for rows in [slice(r * tm // 4, (r + 1) * tm // 4) for r in range(4)]:
            o_ref[rows, :] = act(jnp.dot(xb_ref[rows, :], w, preferred_element_type=F32))

    group = lax.div(j, blocks_per_group)
    is_gelu, is_swish = group == GELU_GROUP, group == SWISH_GROUP
    pl.when(is_gelu)(lambda: emit(jax.nn.gelu))
    pl.when(is_swish)(lambda: emit(lambda t: t * _sigmoid(t)))
    pl.when(jnp.logical_not(jnp.logical_or(is_gelu, is_swish)))(lambda: emit(None))


def _in_proj(x1, w_in, *, export=False):
    m = x1.shape[0]
    tm, tn = (m, PROJ_TN_EXPORT) if export else (PROJ_TM, PROJ_TN)
    assert not export or m == tm
    out_shape = [jax.ShapeDtypeStruct((m, D_IN), F32)]
    out_specs = [pl.BlockSpec((tm, tn), lambda i, j: (i, j))]
    x_mode = {}
    if export:
        out_shape.append(jax.ShapeDtypeStruct(w_in.shape, BF16))
        out_specs.append(pl.BlockSpec((D_MODEL, tn), lambda i, j: (0, j)))
        x_mode = dict(pipeline_mode=pl.Buffered(1))
    out = pl.pallas_call(
        functools.partial(_in_proj_kernel, export=export, blocks_per_group=D_RNN // tn),
        out_shape=out_shape,
        grid=(m // tm, D_IN // tn),
        in_specs=[
            pl.BlockSpec((tm, D_MODEL), lambda i, j: (i, 0), **x_mode),
            pl.BlockSpec((D_MODEL, tn), lambda i, j: (0, j)),
        ],
        out_specs=out_specs,
        scratch_shapes=[pltpu.VMEM((tm, D_MODEL), BF16)],
        compiler_params=pltpu.CompilerParams(
            dimension_semantics=("parallel", "arbitrary"), vmem_limit_bytes=VMEM_LIMIT),
        name="in_proj_export" if export else "in_proj",
    )(x1, w_in)
    return out if export else out[0]


def _mixer_kernel(*refs, is_prompt, n_cast):
    n_chunks = MIX_R // CHUNK
    (xa_ref, ga_ref, q_ref, k_ref, v_ref, g_ref, cos_ref, sin_ref,
     dmask_ref, qdec_ref, kdec_ref, cdec_ref,
     convw_ref, convb_ref, wri_ref, bri_ref, lam_ref, gng_ref, gnb_ref) = refs[:19]
    refs = refs[19:]
    if is_prompt:
        conv0_ref = h0_ref = s0_ref = None
    else:
        conv0_ref, h0_ref, s0_ref = refs[:3]
        refs = refs[3:]
    cast_src, refs = refs[:n_cast], refs[n_cast:]
    ya_ref, yb_ref, convo_ref, ho_ref, so_ref = refs[:5]
    cast_dst, refs = refs[5:5 + n_cast], refs[5 + n_cast:]
    (xpad_ref, xc_ref, a_ref, u_ref, hs_ref,
     qb_ref, kb_ref, qd_ref, kd_ref, vb_ref, sc_ref, oh_ref) = refs

    for src, dst in zip(cast_src, cast_dst):
        dst[...] = src[...].astype(BF16)

    if is_prompt:
        @pl.when(pl.program_id(1) == 0)
        def _():
            convo_ref[...] = jnp.zeros_like(convo_ref)
            ho_ref[...] = jnp.zeros_like(ho_ref)
            so_ref[...] = jnp.zeros_like(so_ref)

    n_seg = 1 if is_prompt else n_chunks
    seg_len = MIX_R // n_seg
    conv_b = convb_ref[...]
    w0, w1, w2, w3 = (convw_ref[j:j + 1, :] for j in range(CONV_W))
    lo = CONV_PAD - (CONV_W - 1)
    for sg in range(n_seg):
        rows = slice(sg * seg_len, (sg + 1) * seg_len)
        xpad_ref[sg, lo:CONV_PAD, :] = convo_ref[0] if is_prompt else conv0_ref[sg]
        xpad_ref[sg, CONV_PAD:CONV_PAD + seg_len, :] = xa_ref[rows, :]
        acc = conv_b + w0 * xpad_ref[sg, lo:lo + seg_len, :]
        acc = acc + w1 * xpad_ref[sg, lo + 1:lo + 1 + seg_len, :]
        acc = acc + w2 * xpad_ref[sg, lo + 2:lo + 2 + seg_len, :]
        acc = acc + w3 * xpad_ref[sg, lo + 3:lo + 3 + seg_len, :]
        xc_ref[rows, :] = acc
        convo_ref[sg] = xpad_ref[sg, lo + seg_len:CONV_PAD + seg_len, :]

    xc = xc_ref[...]
    xcb = xc.astype(BF16)
    n_grp = D_RNN // GATE_GRP
    ri = jnp.concatenate(
        [jnp.dot(xcb[:, (g % n_grp) * GATE_GRP:(g % n_grp + 1) * GATE_GRP], wri_ref[g],
                 preferred_element_type=F32) for g in range(2 * n_grp)], axis=1) + bri_ref[...]
    r = _sigmoid(ri[:, :D_RNN])
    i = _sigmoid(ri[:, D_RNN:])
    log_a = -LRU_C * r * jax.nn.softplus(-lam_ref[...])
    a = jnp.exp(log_a)
    a_ref[...] = a
    w = jnp.tanh(-log_a) * (1.0 + a * a)
    u_ref[...] = jnp.where(w > 0.0, w * lax.rsqrt(w), 0.0) * (i * xc)

    for sg in range(n_seg):
        h_init = ho_ref[0] if is_prompt else h0_ref[sg]

        def step(t, h, base=sg * seg_len):
            row = pl.ds(base + t, 1)
            h = a_ref[row, :] * h + u_ref[row, :]
            hs_ref[row, :] = h
            return h

        ho_ref[sg] = lax.fori_loop(0, seg_len, step, h_init, unroll=True)

    ya_ref[...] = (hs_ref[...] * ga_ref[...]).astype(BF16)

    k_scale = RET_DK ** -0.5

    blk = MIX_R if is_prompt else CHUNK
    head_cols = [slice(hd * RET_DK, (hd + 1) * RET_DK) for hd in range(RET_HEADS)]
    blk_rows = [slice(c * blk, (c + 1) * blk) for c in range(MIX_R // blk)]
    cos, sin = cos_ref[...], sin_ref[...]

    for hd, cols in enumerate(head_cols):
        qh, kh = q_ref[:, cols], k_ref[:, cols]
        qr = qh * cos + pltpu.roll(qh, RET_DK // 2, 1) * sin
        kr = (kh * cos + pltpu.roll(kh, RET_DK // 2, 1) * sin) * k_scale
        qb_ref[:, cols] = qr.astype(BF16)
        kb_ref[:, cols] = kr.astype(BF16)
        qd_ref[:, cols] = (qr * qdec_ref[hd]).astype(BF16)
        kd_ref[:, cols] = (kr * kdec_ref[hd]).astype(BF16)
        vb_ref[:, cols] = v_ref[:, cols].astype(BF16)

    for rows in blk_rows:
        for hd, cols in enumerate(head_cols):
            s = lax.dot_general(qb_ref[rows, cols], kb_ref[rows, cols], (((1,), (1,)), ((), ())),
                                preferred_element_type=F32) * dmask_ref[hd]
            sc_ref[hd, rows, :] = s.astype(BF16)

    for c, rows in enumerate(blk_rows):
        for hd, cols in enumerate(head_cols):
            state = so_ref[0, hd] if is_prompt else s0_ref[c, hd]
            vb = vb_ref[rows, cols]
            o = jnp.dot(sc_ref[hd, rows, :], vb, preferred_element_type=F32)
            o = o + jnp.dot(qd_ref[rows, cols], state.astype(BF16), preferred_element_type=F32)
            oh_ref[rows, cols] = o
            new_state = cdec_ref[hd] * state + lax.dot_general(
                kd_ref[rows, cols], vb, (((0,), (0,)), ((), ())), preferred_element_type=F32)
            if is_prompt:
                so_ref[0, hd] = new_state
            else:
                so_ref[c, hd] = new_state

    for cols in head_cols:
        o = oh_ref[:, cols]
        mu = jnp.mean(o, axis=-1, keepdims=True)
        oc = o - mu
        var = jnp.mean(oc * oc, axis=-1, keepdims=True)
        on = oc * lax.rsqrt(var + GN_EPS) * gng_ref[:, cols] + gnb_ref[:, cols]
        yb_ref[:, cols] = (g_ref[:, cols] * on).astype(BF16)


def _mixer(z, tables, params, states, *, is_prompt, cast_srcs=()):
    cos_t, sin_t, dmask, qdec, kdec, cdec = tables
    n_chunks = MIX_R // CHUNK
    if is_prompt:
        tiles_per_seq = SEQ // MIX_R
        grid = (BATCH, tiles_per_seq)
        n_seq = BATCH
        n_state = 1
        row_blk = lambda s, t: s * tiles_per_seq + t
        zspec = lambda col: pl.BlockSpec((MIX_R, D_RNN), lambda s, t, col=col: (row_blk(s, t), col))
        tab_spec = pl.BlockSpec((MIX_R, RET_DK), lambda s, t: (t, 0))
        state_idx = lambda s, t: s
        out_row = lambda s, t: (row_blk(s, t), 0)
        m_rows = M_PROMPT
        sem = ("parallel", "arbitrary")
    else:
        grid = (M_SAMPLE // MIX_R,)
        n_seq = DEC_BATCH
        n_state = n_chunks
        zspec = lambda col: pl.BlockSpec((MIX_R, D_RNN), lambda i, col=col: (i, col))
        tab_spec = pl.BlockSpec((MIX_R, RET_DK), lambda i: (0, 0))
        state_idx = lambda i: i
        out_row = lambda i: (i, 0)
        m_rows = M_SAMPLE
        sem = ("parallel",)

    def const_spec(shape):
        nd = len(shape)
        return pl.BlockSpec(shape, lambda *_: (0,) * nd)

    def state_spec(shape):
        nd = len(shape)
        return pl.BlockSpec((n_state,) + shape, lambda *g: (state_idx(*g),) + (0,) * nd)

    in_specs = [zspec(c) for c in range(6)] + [tab_spec, tab_spec] + [
        const_spec(dmask.shape), const_spec(qdec.shape), const_spec(kdec.shape), const_spec(cdec.shape)
    ] + [const_spec(p.shape) for p in params]
    args = [z] * 6 + [cos_t, sin_t, dmask, qdec, kdec, cdec] + list(params)
    if not is_prompt:
        in_specs += [state_spec((CONV_W - 1, D_RNN)), state_spec((1, D_RNN)),
                     state_spec((RET_HEADS, RET_DK, RET_DV))]
        args += list(states)

    n_steps = 1
    for g in grid:
        n_steps *= g
    step_idx = (lambda s, t: (s * grid[1] + t, 0)) if is_prompt else (lambda i: (i, 0))
    cast_specs = [pl.BlockSpec((w.shape[0] // n_steps, w.shape[1]), step_idx) for w in cast_srcs]
    in_specs += cast_specs
    args += list(cast_srcs)

    n_seg = 1 if is_prompt else n_chunks
    seg_len = MIX_R // n_seg
    out_shape = [
        jax.ShapeDtypeStruct((m_rows, D_RNN), BF16),
        jax.ShapeDtypeStruct((m_rows, D_RET_V), BF16),
        jax.ShapeDtypeStruct((n_seq, CONV_W - 1, D_RNN), F32),
        jax.ShapeDtypeStruct((n_seq, 1, D_RNN), F32),
        jax.ShapeDtypeStruct((n_seq, RET_HEADS, RET_DK, RET_DV), F32),
    ]
    out_specs = [
        pl.BlockSpec((MIX_R, D_RNN), out_row),
        pl.BlockSpec((MIX_R, D_RET_V), out_row),
        state_spec((CONV_W - 1, D_RNN)),
        state_spec((1, D_RNN)),
        state_spec((RET_HEADS, RET_DK, RET_DV)),
    ] + cast_specs
    out_shape += [jax.ShapeDtypeStruct(w.shape, BF16) for w in cast_srcs]
    scratch = [
        pltpu.VMEM((n_seg, CONV_PAD + seg_len, D_RNN), F32),
        pltpu.VMEM((MIX_R, D_RNN), F32),
        pltpu.VMEM((MIX_R, D_RNN), F32),
        pltpu.VMEM((MIX_R, D_RNN), F32),
        pltpu.VMEM((MIX_R, D_RNN), F32),
    ] + [pltpu.VMEM((MIX_R, D_RET_K), BF16)] * 5 + [
        pltpu.VMEM((RET_HEADS, MIX_R, MIX_R if is_prompt else CHUNK), BF16),
        pltpu.VMEM((MIX_R, D_RET_V), F32),
    ]
    return pl.pallas_call(
        functools.partial(_mixer_kernel, is_prompt=is_prompt, n_cast=len(cast_srcs)),
        out_shape=out_shape,
        grid=grid,
        in_specs=in_specs,
        out_specs=out_specs,
        scratch_shapes=scratch,
        compiler_params=pltpu.CompilerParams(dimension_semantics=sem, vmem_limit_bytes=VMEM_LIMIT),
        name="mixer_prompt" if is_prompt else "mixer_sample",
    )(*args)


def _merge_ln_kernel(ya_ref, yb_ref, ga_ref, gb_ref, x_ref, wa_ref, wb_ref, wo_ref, g_ref, b_ref, *rest):
    n_cast = (len(rest) - 1) // 2
    o_ref = rest[n_cast]
    for src, dst in zip(rest[:n_cast], rest[n_cast + 1:]):
        dst[...] = src[...].astype(BF16)

    for r in range(MERGE_TM // MERGE_SUB):
        rows = slice(r * MERGE_SUB, (r + 1) * MERGE_SUB)
        pa = jnp.dot(ya_ref[rows, :], wa_ref[...], preferred_element_type=F32)
        pb = jnp.dot(yb_ref[rows, :], wb_ref[...], preferred_element_type=F32)
        merged = jax.nn.sigmoid(ga_ref[rows, :]) * pa + jax.nn.sigmoid(gb_ref[rows, :]) * pb
        mo = jnp.dot(merged.astype(BF16), wo_ref[...], preferred_element_type=F32)
        o_ref[rows, :] = _layer_norm_rows(DN_ALPHA * x_ref[rows, :] + mo, g_ref[...], b_ref[...])


def _merge_ln(ya, yb, z, x1, wa, wb, wo, ln_g, ln_b, cast_srcs=()):
    m = x1.shape[0]
    n_steps = m // MERGE_TM
    gate_a_blk = (2 * D_RNN + 2 * D_RET_K + 2 * D_RET_V) // D_MODEL
    row = lambda i: (i, 0)
    const = lambda i: (0, 0)
    cast_specs = [pl.BlockSpec((w.shape[0] // n_steps, w.shape[1]), row) for w in cast_srcs]
    out = pl.pallas_call(
        _merge_ln_kernel,
        out_shape=[jax.ShapeDtypeStruct((m, D_MODEL), F32)]
        + [jax.ShapeDtypeStruct(w.shape, BF16) for w in cast_srcs],
        grid=(n_steps,),
        in_specs=[
            pl.BlockSpec((MERGE_TM, D_RNN), row),
            pl.BlockSpec((MERGE_TM, D_RET_V), row),
            pl.BlockSpec((MERGE_TM, D_MODEL), lambda i: (i, gate_a_blk)),
            pl.BlockSpec((MERGE_TM, D_MODEL), lambda i: (i, gate_a_blk + 1)),
            pl.BlockSpec((MERGE_TM, D_MODEL), row),
            pl.BlockSpec((D_RNN, D_MODEL), const, pipeline_mode=pl.Buffered(1)),
            pl.BlockSpec((D_RET_V, D_MODEL), const, pipeline_mode=pl.Buffered(1)),
            pl.BlockSpec((D_MODEL, D_MODEL), const, pipeline_mode=pl.Buffered(1)),
            pl.BlockSpec((1, D_MODEL), const),
            pl.BlockSpec((1, D_MODEL), const),
        ] + cast_specs,
        out_specs=[pl.BlockSpec((MERGE_TM, D_MODEL), row)] + cast_specs,
        compiler_params=pltpu.CompilerParams(
            dimension_semantics=("parallel",), vmem_limit_bytes=VMEM_LIMIT),
        name="merge_ln",
    )(ya, yb, z, z, x1, wa, wb, wo, ln_g, ln_b, *cast_srcs)
    return out if cast_srcs else out[0]


def _rope_tables(pos):
    d = RET_DK
    inv_freq = ROPE_BASE ** (-jnp.arange(0, d, 2, dtype=F32) / d)
    ang = pos.astype(F32)[:, None] * inv_freq[None, :]
    cos, sin = jnp.cos(ang), jnp.sin(ang)
    return jnp.concatenate([cos, cos], axis=-1), jnp.concatenate([-sin, sin], axis=-1)


def _decay_tables(blk):
    log_g = jnp.log1p(-jnp.exp2(-5.0 - jnp.arange(RET_HEADS, dtype=F32)))
    idx = jnp.arange(blk, dtype=F32)
    diff = idx[:, None] - idx[None, :]
    dmask = jnp.where(diff >= 0, jnp.exp(log_g[:, None, None] * jnp.maximum(diff, 0.0)), 0.0)
    q_dec = jnp.exp(log_g[:, None] * (idx[None, :] + 1.0))
    k_dec = jnp.exp(log_g[:, None] * (blk - 1.0 - idx[None, :]))
    chunk_dec = jnp.exp(log_g * blk)
    lanes = (RET_HEADS, blk, RET_DK)
    return (dmask, jnp.broadcast_to(q_dec[:, :, None], lanes), jnp.broadcast_to(k_dec[:, :, None], lanes),
            jnp.broadcast_to(chunk_dec[:, None, None], (RET_HEADS, 1, RET_DV)))


def _block_diag(w):
    per = GATE_GRP // LRU_BLOCK
    tiled = jnp.tile(w.reshape(D_RNN // GATE_GRP, GATE_GRP, LRU_BLOCK), (1, 1, per))
    row_blk = lax.broadcasted_iota(jnp.int32, (GATE_GRP, GATE_GRP), 0) // LRU_BLOCK
    col_blk = lax.broadcasted_iota(jnp.int32, (GATE_GRP, GATE_GRP), 1) // LRU_BLOCK
    return jnp.where((row_blk == col_blk)[None], tiled, 0.0)


def kernel(x_prompt, x_sample, state_conv, state_lru, state_ret, ffn1_w_gate, ffn1_w_up, ffn1_w_down, ln1_g, ln1_b, w_in, conv_w, conv_b, rg_w, rg_b, ig_w, ig_b, lru_lambda, ret_gn_g, ret_gn_b, w_a_proj, w_b_proj, w_o, ln2_g, ln2_b, ffn2_w_gate, ffn2_w_up, ffn2_w_down, ln3_g, ln3_b):
    assert DEPTH == 1
    w_ri = jnp.concatenate([_block_diag(rg_w[0]), _block_diag(ig_w[0])], axis=0).astype(BF16)
    b_ri = jnp.concatenate([rg_b, ig_b], axis=1)
    params = (conv_w[0], conv_b, w_ri, b_ri, lru_lambda, ret_gn_g, ret_gn_b)
    rope = _rope_tables(jnp.arange(PAST_LEN + DEC_SEQ, dtype=jnp.int32))
    x_p = x_prompt.reshape(M_PROMPT, D_MODEL)
    x_s = x_sample.reshape(M_SAMPLE, D_MODEL)
    states = (state_conv[0], state_lru[0][:, None, :], state_ret[0])

    x1_s, *ffn1_w = _ffn_ln(x_s, ffn1_w_gate[0], ffn1_w_up[0], ffn1_w_down[0], ln1_g, ln1_b, export=True)
    x1_p = _ffn_ln(x_p, *ffn1_w, ln1_g, ln1_b)
    z_s, w_in_b = _in_proj(x1_s, w_in[0], export=True)
    z_p = _in_proj(x1_p, w_in_b)

    ya_p, yb_p, conv_p, lru_p, ret_p, wd2, *merge_w = _mixer(
        z_p, rope + _decay_tables(MIX_R), params, None, is_prompt=True,
        cast_srcs=(ffn2_w_down[0], w_a_proj[0], w_b_proj[0], w_o[0]))
    reps = MIX_R // CHUNK
    dmask_s, qdec_s, kdec_s, cdec_s = _decay_tables(CHUNK)
    tables_s = tuple(jnp.tile(t[PAST_LEN:], (reps, 1)) for t in rope) + (
        dmask_s, jnp.tile(qdec_s, (1, reps, 1)), jnp.tile(kdec_s, (1, reps, 1)), cdec_s)
    ya_s, yb_s, conv_s, lru_s, ret_s = _mixer(z_s, tables_s, params, states, is_prompt=False)

    x2_p, wg2, wu2 = _merge_ln(ya_p, yb_p, z_p, x1_p, *merge_w, ln2_g, ln2_b,
                               cast_srcs=(ffn2_w_gate[0], ffn2_w_up[0]))
    x2_s = _merge_ln(ya_s, yb_s, z_s, x1_s, *merge_w, ln2_g, ln2_b)
    y_p = _ffn_ln(x2_p, wg2, wu2, wd2, ln3_g, ln3_b)
    y_s = _ffn_ln(x2_s, wg2, wu2, wd2, ln3_g, ln3_b)

    return (y_p.reshape(BATCH, SEQ, D_MODEL), y_s.reshape(DEC_BATCH, DEC_SEQ, D_MODEL),
            conv_p[None], lru_p.reshape(1, BATCH, D_RNN), ret_p[None],
            conv_s[None], lru_s.reshape(1, DEC_BATCH, D_RNN), ret_s[None])
```
